```python
import jax, jax.numpy as jnp
from jax import lax
import numpy as np

D_MODEL = 1024
BATCH = 8
SEQ = 2048
DEPTH = 1

HEAD_DIM = 64
N_SLOT_HEADS = 8
DILATED_PATTERNS = ((128, 1), (512, 4), (2048, 16))
N_GROUPS = len(DILATED_PATTERNS)
ATTN_HEADS = N_GROUPS * N_SLOT_HEADS
ATTN_QKV_WIDTH = ATTN_HEADS * HEAD_DIM
ATTN_OUT_WIDTH = N_SLOT_HEADS * HEAD_DIM
BLK = max(w // (2 * d) for (w, d) in DILATED_PATTERNS)
ROPE_THETA = 500000.0
ROT_DIM = HEAD_DIM // 4
CONV_CH = D_MODEL // 2
CONV_WIDTH = 31
N_BRANCH = 2
D_FF = -(-8 * D_MODEL // 768) * 256
IN_WIDTH = 3 * ATTN_QKV_WIDTH + 2 * CONV_CH + N_BRANCH * D_MODEL
EPS = 1e-6
NEG_INF = -1e30

kernel_name = 'hybrid_dilated_attn_conformer_conv_block'


def rmsnorm(t, w):
    tf = t.astype(jnp.float32)
    y = tf * lax.rsqrt(jnp.mean(tf * tf, axis=-1, keepdims=True) + EPS)
    return (y * w.astype(jnp.float32)).astype(t.dtype)


def layernorm(t, w, b):
    tf = t.astype(jnp.float32)
    mu = jnp.mean(tf, axis=-1, keepdims=True)
    var = jnp.mean(jnp.square(tf - mu), axis=-1, keepdims=True)
    y = (tf - mu) * lax.rsqrt(var + EPS)
    return (y * w.astype(jnp.float32) + b.astype(jnp.float32)).astype(t.dtype)


def partial_rope(t, cos, sin):
    tf = t.astype(jnp.float32)
    half = ROT_DIM // 2
    t1, t2, rest = tf[..., :half], tf[..., half:ROT_DIM], tf[..., ROT_DIM:]
    rot = jnp.concatenate([t1 * cos - t2 * sin, t2 * cos + t1 * sin, rest], axis=-1)
    return rot.astype(t.dtype)


def dilated_window_attention(q, k, v, dilation, half_span):
    B, S, H, Dh = q.shape
    L = S // dilation
    nb = -(-L // BLK)
    Lp = nb * BLK

    def residue_major(t):
        return t.reshape(B, L, dilation, H, Dh).transpose(0, 2, 3, 1, 4)

    qr, kr, vr = residue_major(q), residue_major(k), residue_major(v)
    qb = jnp.pad(qr, [(0, 0)] * 3 + [(0, Lp - L), (0, 0)]).reshape(B, dilation, H, nb, BLK, Dh)

    def banded(t):
        tb = jnp.pad(t, [(0, 0)] * 3 + [(BLK, Lp - L + BLK), (0, 0)])
        tb = tb.reshape(B, dilation, H, nb + 2, BLK, Dh)
        return jnp.concatenate([tb[:, :, :, j:j + nb] for j in range(3)], axis=4)

    kb, vb = banded(kr), banded(vr)
    qpos = jnp.arange(nb)[:, None] * BLK + jnp.arange(BLK)[None, :]
    kpos = jnp.arange(nb)[:, None] * BLK - BLK + jnp.arange(3 * BLK)[None, :]
    dist = jnp.abs(qpos[:, :, None] - kpos[:, None, :])
    valid = (dist <= half_span) & (kpos[:, None, :] >= 0) & (kpos[:, None, :] < L)

    s = jnp.einsum('bdhnqc,bdhnkc->bdhnqk', qb.astype(jnp.float32), kb.astype(jnp.float32))
    s = jnp.where(valid, s * (HEAD_DIM ** -0.5), NEG_INF)
    m = jnp.max(s, axis=-1, keepdims=True)
    p = jnp.exp(s - m)
    den = jnp.sum(p, axis=-1, keepdims=True)
    o = jnp.einsum('bdhnqk,bdhnkc->bdhnqc', p, vb.astype(jnp.float32)) / den
    lse = (m + jnp.log(den))[..., 0]
    o = o.reshape(B, dilation, H, Lp, Dh)[:, :, :, :L].transpose(0, 3, 1, 2, 4).reshape(B, S, H, Dh)
    lse = lse.reshape(B, dilation, H, Lp)[..., :L].transpose(0, 3, 1, 2).reshape(B, S, H)
    return o, lse


def depthwise_conv(u, w, b):
    pad = (CONV_WIDTH - 1) // 2
    y = lax.conv_general_dilated(
        u, w[:, None, :].astype(u.dtype), window_strides=(1,), padding=[(pad, pad)],
        dimension_numbers=('NWC', 'WIO', 'NWC'), feature_group_count=CONV_CH)
    return y + b.astype(u.dtype)


def setup_inputs(seed: int = 0) -> dict:
    key = jax.random.key(seed)
    ks = jax.random.split(key, 20)
    f32 = jnp.float32

    def nrm(k, shape, scale):
        return jax.random.normal(k, shape, f32) * scale

    x = jax.random.normal(ks[0], (BATCH, SEQ, D_MODEL), f32)
    offsets = jax.random.randint(ks[1], (BATCH, 1), 0, 4096, dtype=jnp.int32)
    positions = (offsets + jnp.arange(SEQ, dtype=jnp.int32)[None, :]).astype(jnp.int32)
    return {
        'x': x,
        'positions': positions,
        'norm1_w': 1.0 + nrm(ks[2], (DEPTH, D_MODEL), 0.02),
        'w_in': nrm(ks[3], (DEPTH, D_MODEL, IN_WIDTH), D_MODEL ** -0.5),
        'b_gate': nrm(ks[4], (DEPTH, N_BRANCH, D_MODEL), 0.02),
        'q_norm_w': 1.0 + nrm(ks[5], (DEPTH, HEAD_DIM), 0.02),
        'k_norm_w': 1.0 + nrm(ks[6], (DEPTH, HEAD_DIM), 0.02),
        'w_o_attn': nrm(ks[7], (DEPTH, ATTN_OUT_WIDTH, D_MODEL), ATTN_OUT_WIDTH ** -0.5),
        'conv_w': nrm(ks[8], (DEPTH, CONV_WIDTH, CONV_CH), CONV_WIDTH ** -0.5),
        'conv_b': nrm(ks[9], (DEPTH, CONV_CH), 0.02),
        'conv_ln_w': 1.0 + nrm(ks[10], (DEPTH, CONV_CH), 0.02),
        'conv_ln_b': nrm(ks[11], (DEPTH, CONV_CH), 0.02),
        'w_pw_conv': nrm(ks[12], (DEPTH, CONV_CH, D_MODEL), CONV_CH ** -0.5),
        'w_out': nrm(ks[13], (DEPTH, D_MODEL, D_MODEL), D_MODEL ** -0.5),
        'norm2_w': 1.0 + nrm(ks[14], (DEPTH, D_MODEL), 0.02),
        'w_ffn_in': nrm(ks[15], (DEPTH, D_MODEL, 2 * D_FF), D_MODEL ** -0.5),
        'w_ffn_out': nrm(ks[16], (DEPTH, D_FF, D_MODEL), D_FF ** -0.5),
    }


def reference(x, positions, norm1_w, w_in, b_gate, q_norm_w, k_norm_w, w_o_attn,
              conv_w, conv_b, conv_ln_w, conv_ln_b, w_pw_conv, w_out, norm2_w,
              w_ffn_in, w_ffn_out):
    B, S, _ = x.shape
    inv_freq = ROPE_THETA ** (-jnp.arange(0, ROT_DIM, 2, dtype=jnp.float32) / ROT_DIM)
    ang = positions.astype(jnp.float32)[..., None] * inv_freq
    cos = jnp.cos(ang)[:, :, None, None, :]
    sin = jnp.sin(ang)[:, :, None, None, :]
    split_at = [ATTN_QKV_WIDTH, 2 * ATTN_QKV_WIDTH, 3 * ATTN_QKV_WIDTH,
                3 * ATTN_QKV_WIDTH + 2 * CONV_CH]

    for l in range(DEPTH):
        h = rmsnorm(x, norm1_w[l])
        proj = h @ w_in[l].astype(h.dtype)
        q, k, v, conv_in, gate_logits = jnp.split(proj, split_at, axis=-1)
        hshape = (B, S, N_GROUPS, N_SLOT_HEADS, HEAD_DIM)
        q = partial_rope(rmsnorm(q.reshape(hshape), q_norm_w[l]), cos, sin)
        k = partial_rope(rmsnorm(k.reshape(hshape), k_norm_w[l]), cos, sin)
        v = v.reshape(hshape)

        outs, lses = [], []
        for g, (window, dilation) in enumerate(DILATED_PATTERNS):
            o_g, lse_g = dilated_window_attention(q[:, :, g], k[:, :, g], v[:, :, g],
                                                  dilation, window // (2 * dilation))
            outs.append(o_g)
            lses.append(lse_g)
        mix = jax.nn.softmax(jnp.stack(lses, axis=0), axis=0)
        attn = jnp.sum(mix[..., None] * jnp.stack(outs, axis=0), axis=0)
        attn = attn.reshape(B, S, ATTN_OUT_WIDTH).astype(x.dtype)
        y_a = attn @ w_o_attn[l].astype(x.dtype)

        a, b = jnp.split(conv_in, 2, axis=-1)
        u = a * jax.nn.sigmoid(b)
        u = depthwise_conv(u, conv_w[l], conv_b[l])
        u = jax.nn.silu(layernorm(u, conv_ln_w[l], conv_ln_b[l]))
        y_b = u @ w_pw_conv[l].astype(u.dtype)

        gates = jax.nn.sigmoid(gate_logits + b_gate[l].reshape(N_BRANCH * D_MODEL).astype(x.dtype))
        g_a, g_b = jnp.split(gates, 2, axis=-1)
        x = x + (g_a * y_a + g_b * y_b) @ w_out[l].astype(x.dtype)

        h2 = rmsnorm(x, norm2_w[l])
        gt, up = jnp.split(h2 @ w_ffn_in[l].astype(h2.dtype), 2, axis=-1)
        x = x + (jax.nn.silu(gt) * up) @ w_ffn_out[l].astype(x.dtype)
    return x
```

```python
import functools

import jax
import jax.numpy as jnp
from jax import lax
from jax.experimental import pallas as pl
from jax.experimental.pallas import tpu as pltpu

D_MODEL = 1024
HEAD_DIM = 64
N_SLOT_HEADS = 8
DILATED_PATTERNS = ((128, 1), (512, 4), (2048, 16))
N_GROUPS = len(DILATED_PATTERNS)
GROUP_WIDTH = N_SLOT_HEADS * HEAD_DIM
QKV_WIDTH = N_GROUPS * GROUP_WIDTH
ROPE_THETA = 500000.0
ROT_DIM = HEAD_DIM // 4
CONV_CH = D_MODEL // 2
CONV_WIDTH = 31
D_FF = 2816
EPS = 1e-6
NEG_INF = -1e30

LANES = 128
Q_BLOCK = 128
HALF_SPAN = 64
ROW_TILE = 512
VMEM_LIMIT = 56 * 1024 * 1024

F32 = jnp.float32
BF16 = jnp.bfloat16


def _dot(a, b):
    return jnp.dot(a, b, preferred_element_type=F32)


def _split_dot(a_f32, b_bf16):
    hi = a_f32.astype(BF16)
    lo = (a_f32 - hi.astype(F32)).astype(BF16)
    return _dot(hi, b_bf16) + _dot(lo, b_bf16)


def _resident(shape):
    nd = len(shape)
    return pl.BlockSpec(shape, lambda *_: (0,) * nd, pipeline_mode=pl.Buffered(1))


def _proj_kernel(x_ref, n1_ref, w_ref, cos_ref, sa_ref, sb_ref, qn_ref, kn_ref, seg_ref,
                 bg_ref, q_ref, k_ref, v_ref, u_ref, g_ref, h_ref):
    xt = x_ref[...]
    ms = jnp.mean(xt * xt, axis=-1, keepdims=True)
    h_ref[...] = (xt * lax.rsqrt(ms + EPS) * n1_ref[...]).astype(BF16)

    def proj(c0, width):
        return _dot(h_ref[...], w_ref[:, c0:c0 + width])

    cos, sa, sb = cos_ref[...], sa_ref[...], sb_ref[...]
    seg = seg_ref[...]

    def head_norm_rope(t, nw, scale):
        ssq = _split_dot(t * t, seg)
        y = t * lax.rsqrt(ssq * (1.0 / HEAD_DIM) + EPS) * nw
        r = y * cos + pltpu.roll(y, LANES - ROT_DIM // 2, 1) * sa + pltpu.roll(y, ROT_DIM // 2, 1) * sb
        return (r * scale).astype(BF16)

    for out_ref, base, nw_ref, scale in ((q_ref, 0, qn_ref, HEAD_DIM ** -0.5),
                                         (k_ref, QKV_WIDTH, kn_ref, 1.0)):
        nw = nw_ref[...]
        for j in range(QKV_WIDTH // GROUP_WIDTH):
            acc = proj(base + j * GROUP_WIDTH, GROUP_WIDTH)
            for c in range(GROUP_WIDTH // LANES):
                lo = j * GROUP_WIDTH + c * LANES
                out_ref[:, lo:lo + LANES] = head_norm_rope(acc[:, c * LANES:(c + 1) * LANES], nw, scale)

    for j in range(QKV_WIDTH // GROUP_WIDTH):
        lo = j * GROUP_WIDTH
        v_ref[:, lo:lo + GROUP_WIDTH] = proj(2 * QKV_WIDTH + lo, GROUP_WIDTH).astype(BF16)

    conv0 = 3 * QKV_WIDTH
    a = proj(conv0, CONV_CH)
    b = proj(conv0 + CONV_CH, CONV_CH)
    u_ref[...] = a * jax.nn.sigmoid(b)

    gate0 = conv0 + 2 * CONV_CH
    for j in range(2 * D_MODEL // GROUP_WIDTH):
        lo = j * GROUP_WIDTH
        g_ref[:, lo:lo + GROUP_WIDTH] = jax.nn.sigmoid(
            proj(gate0 + lo, GROUP_WIDTH) + bg_ref[:, lo:lo + GROUP_WIDTH])


def _proj_call(x2, n1, w_in, cos, sa, sb, qn, kn, seg, bg):
    t = x2.shape[0]
    in_width = w_in.shape[1]
    row = lambda w: pl.BlockSpec((ROW_TILE, w), lambda i: (i, 0))
    return pl.pallas_call(
        _proj_kernel,
        grid=(t // ROW_TILE,),
        in_specs=[row(D_MODEL), _resident((1, D_MODEL)), _resident((D_MODEL, in_width)),
                  row(LANES), row(LANES), row(LANES),
                  _resident((1, LANES)), _resident((1, LANES)), _resident((LANES, LANES)),
                  _resident((1, 2 * D_MODEL))],
        out_specs=[row(QKV_WIDTH), row(QKV_WIDTH), row(QKV_WIDTH), row(CONV_CH), row(2 * D_MODEL)],
        out_shape=[jax.ShapeDtypeStruct((t, QKV_WIDTH), BF16)] * 3
        + [jax.ShapeDtypeStruct((t, CONV_CH), F32), jax.ShapeDtypeStruct((t, 2 * D_MODEL), F32)],
        scratch_shapes=[pltpu.VMEM((ROW_TILE, D_MODEL), BF16)],
        compiler_params=pltpu.CompilerParams(dimension_semantics=("arbitrary",),
                                             vmem_limit_bytes=VMEM_LIMIT),
        name="proj",
    )(x2, n1, w_in, cos, sa, sb, qn, kn, seg, bg)


def _attn_kernel(q_ref, k_ref, v_ref, o_ref, lse_ref, *, seq_len):
    kw = min(2 * Q_BLOCK, seq_len)
    n_blocks = seq_len // Q_BLOCK
    lane = lax.broadcasted_iota(jnp.int32, (1, LANES), 1)
    first_head = lane < HEAD_DIM
    rel = (lax.broadcasted_iota(jnp.int32, (Q_BLOCK, kw), 0)
           - lax.broadcasted_iota(jnp.int32, (Q_BLOCK, kw), 1))

    def block(i, carry):
        q0 = pl.multiple_of(i * Q_BLOCK, Q_BLOCK)
        k0 = pl.multiple_of(jnp.clip(q0 - HALF_SPAN, 0, seq_len - kw), HALF_SPAN)
        dist = rel + (q0 - k0)
        bias = jnp.where((dist <= HALF_SPAN) & (dist >= -HALF_SPAN), 0.0, NEG_INF).astype(F32)
        lse_tile = jnp.zeros((Q_BLOCK, LANES), F32)
        for hp in range(GROUP_WIDTH // LANES):
            cols = slice(hp * LANES, (hp + 1) * LANES)
            qp = q_ref[0, pl.ds(q0, Q_BLOCK), cols]
            kp = k_ref[0, pl.ds(k0, kw), cols]
            vp = v_ref[0, pl.ds(k0, kw), cols]
            acc = jnp.zeros((Q_BLOCK, LANES), F32)
            inv = []
            for e in range(2):
                sel = first_head if e == 0 else jnp.logical_not(first_head)
                qm = jnp.where(sel, qp, jnp.zeros_like(qp))
                s = lax.dot_general(qm, kp, (((1,), (1,)), ((), ())),
                                    preferred_element_type=F32) + bias
                m = jnp.max(s, axis=-1, keepdims=True)
                p = jnp.exp(s - m)
                den = jnp.sum(p, axis=-1, keepdims=True)
                vm = jnp.where(sel, vp, jnp.zeros_like(vp))
                acc = acc + _dot(p.astype(BF16), vm)
                inv.append(1.0 / den)
                lse_tile = jnp.where(lane == 2 * hp + e, m + jnp.log(den), lse_tile)
            o_ref[0, pl.ds(q0, Q_BLOCK), cols] = (
                acc * jnp.where(first_head, inv[0], inv[1])).astype(BF16)
        lse_ref[0, pl.ds(q0, Q_BLOCK), :] = lse_tile
        return carry

    lax.fori_loop(0, n_blocks, block, 0)


def _attn_call(q, k, v, group, dilation):
    b, s, _ = q.shape
    seq_len = s // dilation
    n_col = QKV_WIDTH // GROUP_WIDTH
    view = lambda t: t.reshape(b, seq_len, dilation * t.shape[-1])
    in_spec = pl.BlockSpec((1, seq_len, GROUP_WIDTH), lambda i, r: (i, 0, r * n_col + group))
    o, lse = pl.pallas_call(
        functools.partial(_attn_kernel, seq_len=seq_len),
        grid=(b, dilation),
        in_specs=[in_spec, in_spec, in_spec],
        out_specs=[pl.BlockSpec((1, seq_len, GROUP_WIDTH), lambda i, r: (i, 0, r)),
                   pl.BlockSpec((1, seq_len, LANES), lambda i, r: (i, 0, r))],
        out_shape=[jax.ShapeDtypeStruct((b, seq_len, dilation * GROUP_WIDTH), BF16),
                   jax.ShapeDtypeStruct((b, seq_len, dilation * LANES), F32)],
        compiler_params=pltpu.CompilerParams(dimension_semantics=("arbitrary", "arbitrary"),
                                             vmem_limit_bytes=VMEM_LIMIT),
        name=f"attn_d{dilation}",
    )(view(q), view(k), view(v))
    return o.reshape(b, s, GROUP_WIDTH), lse.reshape(b, s, LANES)


SUBLANES = 8
CONV_PAD = 16
CONV_ROWS = 128


def _conv_kernel(u_ref, w_ref, b_ref, lnw_ref, lnb_ref, o_ref, pad_ref, conv_ref):
    s = u_ref.shape[1]
    zeros = jnp.zeros((CONV_PAD, CONV_CH), F32)
    pad_ref[0:CONV_PAD, :] = zeros
    pad_ref[CONV_PAD + s:2 * CONV_PAD + s, :] = zeros
    pad_ref[CONV_PAD:CONV_PAD + s, :] = u_ref[0]
    n_p = (CONV_WIDTH + SUBLANES) // SUBLANES
    span = CONV_ROWS + (n_p - 1) * SUBLANES

    def step(i, carry):
        r0 = pl.multiple_of(i * CONV_ROWS, CONV_ROWS)
        for c in range(CONV_CH // LANES):
            cols = slice(c * LANES, (c + 1) * LANES)
            win = pad_ref[pl.ds(r0, CONV_ROWS + 2 * CONV_PAD), cols]
            acc = jnp.zeros((CONV_ROWS, LANES), F32) + b_ref[:, cols]
            for j in range(SUBLANES):
                shifted = win[j:j + span]
                for p in range(n_p):
                    t = SUBLANES * p + j - 1
                    if 0 <= t < CONV_WIDTH:
                        acc = acc + shifted[SUBLANES * p:SUBLANES * p + CONV_ROWS] * w_ref[t:t + 1, cols]
            conv_ref[:, cols] = acc
        y = conv_ref[...]
        mu = jnp.mean(y, axis=-1, keepdims=True)
        cen = y - mu
        var = jnp.mean(cen * cen, axis=-1, keepdims=True)
        y = cen * lax.rsqrt(var + EPS) * lnw_ref[...] + lnb_ref[...]
        o_ref[0, pl.ds(r0, CONV_ROWS), :] = (y * jax.nn.sigmoid(y)).astype(BF16)
        return carry

    lax.fori_loop(0, s // CONV_ROWS, step, 0)


def _conv_call(u, w, b, lnw, lnb):
    bsz, s, _ = u.shape
    return pl.pallas_call(
        _conv_kernel,
        grid=(bsz,),
        in_specs=[pl.BlockSpec((1, s, CONV_CH), lambda i: (i, 0, 0)),
                  _resident((CONV_WIDTH, CONV_CH)), _resident((1, CONV_CH)),
                  _resident((1, CONV_CH)), _resident((1, CONV_CH))],
        out_specs=pl.BlockSpec((1, s, CONV_CH), lambda i: (i, 0, 0)),
        out_shape=jax.ShapeDtypeStruct((bsz, s, CONV_CH), BF16),
        scratch_shapes=[pltpu.VMEM((s + 2 * CONV_PAD, CONV_CH), F32),
                        pltpu.VMEM((CONV_ROWS, CONV_CH), F32)],
        compiler_params=pltpu.CompilerParams(dimension_semantics=("arbitrary",),
                                             vmem_limit_bytes=VMEM_LIMIT),
        name="conv_ln",
    )(u, w, b, lnw, lnb)


FF_CHUNK = 256


def _mix_ffn_kernel(x_ref, o0_ref, o1_ref, o2_ref, l0_ref, l1_ref, l2_ref, u_ref, g_ref,
                    expand_ref, wo_ref, wpw_ref, wout_ref, n2_ref, w1_ref, w2_ref,
                    out_ref, act_ref):
    l0, l1, l2 = l0_ref[...], l1_ref[...], l2_ref[...]
    m = jnp.maximum(jnp.maximum(l0, l1), l2)
    e0, e1, e2 = jnp.exp(l0 - m), jnp.exp(l1 - m), jnp.exp(l2 - m)
    inv = 1.0 / (e0 + e1 + e2)
    expand = expand_ref[...]
    attn = jnp.zeros(o0_ref.shape, F32)
    for e, o_ref in ((e0, o0_ref), (e1, o1_ref), (e2, o2_ref)):
        attn = attn + _split_dot(e * inv, expand) * o_ref[...].astype(F32)
    y_a = _dot(attn.astype(BF16), wo_ref[...])
    y_b = _dot(u_ref[...], wpw_ref[...])
    z = g_ref[:, 0:D_MODEL] * y_a + g_ref[:, D_MODEL:2 * D_MODEL] * y_b
    x1 = x_ref[...] + _dot(z.astype(BF16), wout_ref[...])

    ms = jnp.mean(x1 * x1, axis=-1, keepdims=True)
    h2 = (x1 * lax.rsqrt(ms + EPS) * n2_ref[...]).astype(BF16)
    for n in range(0, D_FF, FF_CHUNK):
        gt = _dot(h2, w1_ref[:, n:n + FF_CHUNK])
        up = _dot(h2, w1_ref[:, D_FF + n:D_FF + n + FF_CHUNK])
        act_ref[:, n:n + FF_CHUNK] = (gt * jax.nn.sigmoid(gt) * up).astype(BF16)
    out_ref[...] = x1 + _dot(act_ref[...], w2_ref[...])


def _mix_ffn_call(x2, o, lse, u_ln, gates, expand, wo, wpw, wout, n2, w1, w2):
    t = x2.shape[0]
    row = lambda w: pl.BlockSpec((ROW_TILE, w), lambda i: (i, 0))
    return pl.pallas_call(
        _mix_ffn_kernel,
        grid=(t // ROW_TILE,),
        in_specs=[row(D_MODEL)] + [row(GROUP_WIDTH)] * 3 + [row(LANES)] * 3
        + [row(CONV_CH), row(2 * D_MODEL),
           _resident(expand.shape), _resident(wo.shape), _resident(wpw.shape),
           _resident(wout.shape), _resident(n2.shape), _resident(w1.shape), _resident(w2.shape)],
        out_specs=row(D_MODEL),
        out_shape=jax.ShapeDtypeStruct((t, D_MODEL), F32),
        scratch_shapes=[pltpu.VMEM((ROW_TILE, D_FF), BF16)],
        compiler_params=pltpu.CompilerParams(dimension_semantics=("arbitrary",),
                                             vmem_limit_bytes=VMEM_LIMIT),
        name="mix_ffn",
    )(x2, *o, *lse, u_ln, gates, expand, wo, wpw, wout, n2, w1, w2)


def _rope_tables(positions):
    half = ROT_DIM // 2
    inv_freq = ROPE_THETA ** (-jnp.arange(0, ROT_DIM, 2, dtype=F32) / ROT_DIM)
    ang = positions.astype(F32)[..., None] * inv_freq
    cos, sin = jnp.cos(ang), jnp.sin(ang)
    shape = ang.shape[:-1]
    ones = jnp.ones(shape + (HEAD_DIM - ROT_DIM,), F32)
    z = lambda n: jnp.zeros(shape + (n,), F32)
    c = jnp.concatenate([cos, cos, ones], axis=-1)
    sa = jnp.concatenate([-sin, z(HEAD_DIM - half)], axis=-1)
    sb = jnp.concatenate([z(half), sin, z(HEAD_DIM - ROT_DIM)], axis=-1)
    rep = lambda t: jnp.tile(t, (1,) * len(shape) + (LANES // HEAD_DIM,)).reshape(-1, LANES)
    return rep(c), rep(sa), rep(sb)


def kernel(x, positions, norm1_w, w_in, b_gate, q_norm_w, k_norm_w, w_o_attn, conv_w, conv_b,
           conv_ln_w, conv_ln_b, w_pw_conv, w_out, norm2_w, w_ffn_in, w_ffn_out):
    bsz, s, d = x.shape
    depth = norm1_w.shape[0]
    cos, sa, sb = _rope_tables(positions)
    lane = jnp.arange(LANES)
    seg = (lane[:, None] // HEAD_DIM == lane[None, :] // HEAD_DIM).astype(BF16)
    expand = (lane[:, None] == jnp.arange(GROUP_WIDTH)[None, :] // HEAD_DIM).astype(BF16)
    pair = lambda w: jnp.tile(w.astype(F32), LANES // HEAD_DIM).reshape(1, LANES)

    x2 = x.reshape(bsz * s, d)
    for l in range(depth):
        q, k, v, u, gates = _proj_call(
            x2, norm1_w[l].reshape(1, d), w_in[l].astype(BF16), cos, sa, sb,
            pair(q_norm_w[l]), pair(k_norm_w[l]), seg, b_gate[l].reshape(1, 2 * d))
        q3, k3, v3 = (t.reshape(bsz, s, QKV_WIDTH) for t in (q, k, v))
        outs, lses = [], []
        for g, (_, dilation) in enumerate(DILATED_PATTERNS):
            o_g, lse_g = _attn_call(q3, k3, v3, g, dilation)
            outs.append(o_g.reshape(bsz * s, GROUP_WIDTH))
            lses.append(lse_g.reshape(bsz * s, LANES))
        u_ln = _conv_call(u.reshape(bsz, s, CONV_CH), conv_w[l], conv_b[l].reshape(1, CONV_CH),
                          conv_ln_w[l].reshape(1, CONV_CH), conv_ln_b[l].reshape(1, CONV_CH))
        x2 = _mix_ffn_call(
            x2, outs, lses, u_ln.reshape(bsz * s, CONV_CH), gates, expand,
            w_o_attn[l].astype(BF16), w_pw_conv[l].astype(BF16), w_out[l].astype(BF16),
            norm2_w[l].reshape(1, d), w_ffn_in[l].astype(BF16), w_ffn_out[l].astype(BF16))
    return x2.reshape(bsz, s, d)
```

```python
import functools

import jax
import jax.numpy as jnp
from jax import lax
from jax.experimental import pallas as pl
from jax.experimental.pallas import tpu as pltpu

D_MODEL = 1024
HEAD_DIM = 64
N_SLOT_HEADS = 8
DILATED_PATTERNS = ((128, 1), (512, 4), (2048, 16))
N_GROUPS = len(DILATED_PATTERNS)
GROUP_WIDTH = N_SLOT_HEADS * HEAD_DIM
QKV_WIDTH = N_GROUPS * GROUP_WIDTH
ROPE_THETA = 500000.0
ROT_DIM = HEAD_DIM // 4
CONV_CH = D_MODEL // 2
CONV_WIDTH = 31
D_FF = 2816
EPS = 1e-6
NEG_INF = -1e30

LANES = 128
SUBLANES = 8
Q_BLOCK = 128
HALF_SPAN = 64
ROW_TILE = 512
GROUP_CHUNKS = GROUP_WIDTH // LANES
VMEM_LIMIT = 56 * 1024 * 1024

F32 = jnp.float32
BF16 = jnp.bfloat16


def _dot(a, b):
    return jnp.dot(a, b, preferred_element_type=F32)


def _split_dot(a_f32, b_bf16):
    hi = a_f32.astype(BF16)
    lo = (a_f32 - hi.astype(F32)).astype(BF16)
    return _dot(hi, b_bf16) + _dot(lo, b_bf16)


def _resident(shape):
    nd = len(shape)
    return pl.BlockSpec(shape, lambda *_: (0,) * nd, pipeline_mode=pl.Buffered(1))


def _row_block(width):
    return pl.BlockSpec((None, ROW_TILE, width), lambda b, i: (b, i, 0))


def _group_block(dilation, width, lead=()):
    n_lead = len(lead)
    return pl.BlockSpec(lead + (None, dilation, ROW_TILE // dilation, width),
                        lambda b, i: (0,) * n_lead + (b, 0, i, 0))


def _proj_kernel(x_ref, n1_ref, w_ref, cos_ref, sa_ref, sb_ref, qn_ref, kn_ref, seg_ref,
                 bg_ref, qkv0_ref, qkv1_ref, qkv2_ref, u_ref, g_ref, h_ref, perm_ref):
    xt = x_ref[...]
    ms = jnp.mean(xt * xt, axis=-1, keepdims=True)
    h_ref[...] = (xt * lax.rsqrt(ms + EPS) * n1_ref[...]).astype(BF16)

    def proj(c0, width):
        return _dot(h_ref[...], w_ref[:, c0:c0 + width])

    cos, sa, sb = cos_ref[...], sa_ref[...], sb_ref[...]
    seg = seg_ref[...]

    def head_norm_rope(t, nw, scale):
        ssq = _split_dot(t * t, seg)
        y = t * lax.rsqrt(ssq * (1.0 / HEAD_DIM) + EPS) * nw
        r = y * cos + pltpu.roll(y, LANES - ROT_DIM // 2, 1) * sa + pltpu.roll(y, ROT_DIM // 2, 1) * sb
        return r * scale

    def emit(kind, group, c, val):
        out_ref = (qkv0_ref, qkv1_ref, qkv2_ref)[group]
        dilation = DILATED_PATTERNS[group][1]
        cols = slice(c * LANES, (c + 1) * LANES)
        if dilation == 1:
            out_ref[kind, 0, :, cols] = val.astype(BF16)
            return
        perm_ref[c] = val
        n = ROW_TILE // dilation
        for r in range(dilation):
            out_ref[kind, r, :, cols] = perm_ref[c, pl.ds(r, n, stride=dilation), :].astype(BF16)

    for kind, nw_ref, scale in ((0, qn_ref, HEAD_DIM ** -0.5), (1, kn_ref, 1.0)):
        nw = nw_ref[...]
        for j in range(N_GROUPS):
            acc = proj(kind * QKV_WIDTH + j * GROUP_WIDTH, GROUP_WIDTH)
            for c in range(GROUP_CHUNKS):
                emit(kind, j, c, head_norm_rope(acc[:, c * LANES:(c + 1) * LANES], nw, scale))

    for j in range(N_GROUPS):
        acc = proj(2 * QKV_WIDTH + j * GROUP_WIDTH, GROUP_WIDTH)
        for c in range(GROUP_CHUNKS):
            emit(2, j, c, acc[:, c * LANES:(c + 1) * LANES])

    conv0 = 3 * QKV_WIDTH
    a = proj(conv0, CONV_CH)
    b = proj(conv0 + CONV_CH, CONV_CH)
    u_ref[...] = a * jax.nn.sigmoid(b)

    gate0 = conv0 + 2 * CONV_CH
    for j in range(2 * D_MODEL // GROUP_WIDTH):
        lo = j * GROUP_WIDTH
        g_ref[:, lo:lo + GROUP_WIDTH] = jax.nn.sigmoid(
            proj(gate0 + lo, GROUP_WIDTH) + bg_ref[:, lo:lo + GROUP_WIDTH])


def _proj_call(x3, n1, w_in, cos, sa, sb, qn, kn, seg, bg):
    bsz, s, _ = x3.shape
    in_width = w_in.shape[1]
    qkv_shapes = [jax.ShapeDtypeStruct((3, bsz, d, s // d, GROUP_WIDTH), BF16)
                  for _, d in DILATED_PATTERNS]
    return pl.pallas_call(
        _proj_kernel,
        grid=(bsz, s // ROW_TILE),
        in_specs=[_row_block(D_MODEL), _resident((1, D_MODEL)), _resident((D_MODEL, in_width)),
                  _row_block(LANES), _row_block(LANES), _row_block(LANES),
                  _resident((1, LANES)), _resident((1, LANES)), _resident((LANES, LANES)),
                  _resident((1, 2 * D_MODEL))],
        out_specs=[_group_block(d, GROUP_WIDTH, lead=(3,)) for _, d in DILATED_PATTERNS]
        + [_row_block(CONV_CH), _row_block(2 * D_MODEL)],
        out_shape=qkv_shapes + [jax.ShapeDtypeStruct((bsz, s, CONV_CH), F32),
                                jax.ShapeDtypeStruct((bsz, s, 2 * D_MODEL), F32)],
        scratch_shapes=[pltpu.VMEM((ROW_TILE, D_MODEL), BF16),
                        pltpu.VMEM((GROUP_CHUNKS, ROW_TILE, LANES), F32)],
        compiler_params=pltpu.CompilerParams(dimension_semantics=("arbitrary", "arbitrary"),
                                             vmem_limit_bytes=VMEM_LIMIT),
        name="proj",
    )(x3, n1, w_in, cos, sa, sb, qn, kn, seg, bg)


def _attn_kernel(q_ref, k_ref, v_ref, o_ref, lse_ref):
    seq_len = q_ref.shape[0]
    kw = min(2 * Q_BLOCK, seq_len)
    n_blocks = seq_len // Q_BLOCK
    lane = lax.broadcasted_iota(jnp.int32, (1, LANES), 1)
    first_head = lane < HEAD_DIM
    rel = (lax.broadcasted_iota(jnp.int32, (Q_BLOCK, kw), 0)
           - lax.broadcasted_iota(jnp.int32, (Q_BLOCK, kw), 1))

    def block(i, carry):
        q0 = pl.multiple_of(i * Q_BLOCK, Q_BLOCK)
        k0 = pl.multiple_of(jnp.clip(q0 - HALF_SPAN, 0, seq_len - kw), HALF_SPAN)
        dist = rel + (q0 - k0)
        bias = jnp.where((dist <= HALF_SPAN) & (dist >= -HALF_SPAN), 0.0, NEG_INF).astype(F32)
        lse_tile = jnp.zeros((Q_BLOCK, LANES), F32)
        for hp in range(GROUP_CHUNKS):
            cols = slice(hp * LANES, (hp + 1) * LANES)
            qp = q_ref[pl.ds(q0, Q_BLOCK), cols]
            kp = k_ref[pl.ds(k0, kw), cols]
            vp = v_ref[pl.ds(k0, kw), cols]
            acc = jnp.zeros((Q_BLOCK, LANES), F32)
            inv = []
            for e in range(2):
                sel = first_head if e == 0 else jnp.logical_not(first_head)
                qm = jnp.where(sel, qp, jnp.zeros_like(qp))
                s = lax.dot_general(qm, kp, (((1,), (1,)), ((), ())),
                                    preferred_element_type=F32) + bias
                m = jnp.max(s, axis=-1, keepdims=True)
                p = jnp.exp(s - m)
                den = jnp.sum(p, axis=-1, keepdims=True)
                vm = jnp.where(sel, vp, jnp.zeros_like(vp))
                acc = acc + _dot(p.astype(BF16), vm)
                inv.append(1.0 / den)
                lse_tile = jnp.where(lane == 2 * hp + e, m + jnp.log(den), lse_tile)
            o_ref[pl.ds(q0, Q_BLOCK), cols] = (
                acc * jnp.where(first_head, inv[0], inv[1])).astype(BF16)
        lse_ref[pl.ds(q0, Q_BLOCK), :] = lse_tile
        return carry

    lax.fori_loop(0, n_blocks, block, 0)


def _attn_call(qkv):
    _, bsz, dilation, seq_len, _ = qkv.shape
    in_spec = lambda kind: pl.BlockSpec((None, None, None, seq_len, GROUP_WIDTH),
                                        lambda b, r: (kind, b, r, 0, 0))
    out_spec = lambda w: pl.BlockSpec((None, None, seq_len, w), lambda b, r: (b, r, 0, 0))
    return pl.pallas_call(
        _attn_kernel,
        grid=(bsz, dilation),
        in_specs=[in_spec(0), in_spec(1), in_spec(2)],
        out_specs=[out_spec(GROUP_WIDTH), out_spec(LANES)],
        out_shape=[jax.ShapeDtypeStruct((bsz, dilation, seq_len, GROUP_WIDTH), BF16),
                   jax.ShapeDtypeStruct((bsz, dilation, seq_len, LANES), F32)],
        compiler_params=pltpu.CompilerParams(dimension_semantics=("arbitrary", "arbitrary"),
                                             vmem_limit_bytes=VMEM_LIMIT),
        name=f"attn_d{dilation}",
    )(qkv, qkv, qkv)


CONV_PAD = 16
CONV_ROWS = 128


def _conv_kernel(u_ref, w_ref, b_ref, lnw_ref, lnb_ref, o_ref, pad_ref, conv_ref):
    s = u_ref.shape[1]
    zeros = jnp.zeros((CONV_PAD, CONV_CH), F32)
    pad_ref[0:CONV_PAD, :] = zeros
    pad_ref[CONV_PAD + s:2 * CONV_PAD + s, :] = zeros
    pad_ref[CONV_PAD:CONV_PAD + s, :] = u_ref[0]
    n_p = (CONV_WIDTH + SUBLANES) // SUBLANES
    span = CONV_ROWS + (n_p - 1) * SUBLANES

    def step(i, carry):
        r0 = pl.multiple_of(i * CONV_ROWS, CONV_ROWS)
        for c in range(CONV_CH // LANES):
            cols = slice(c * LANES, (c + 1) * LANES)
            win = pad_ref[pl.ds(r0, CONV_ROWS + 2 * CONV_PAD), cols]
            acc = jnp.zeros((CONV_ROWS, LANES), F32) + b_ref[:, cols]
            for j in range(SUBLANES):
                shifted = win[j:j + span]
                for p in range(n_p):
                    t = SUBLANES * p + j - 1
                    if 0 <= t < CONV_WIDTH:
                        acc = acc + shifted[SUBLANES * p:SUBLANES * p + CONV_ROWS] * w_ref[t:t + 1, cols]
            conv_ref[:, cols] = acc
        y = conv_ref[...]
        mu = jnp.mean(y, axis=-1, keepdims=True)
        cen = y - mu
        var = jnp.mean(cen * cen, axis=-1, keepdims=True)
        y = cen * lax.rsqrt(var + EPS) * lnw_ref[...] + lnb_ref[...]
        o_ref[0, pl.ds(r0, CONV_ROWS), :] = (y * jax.nn.sigmoid(y)).astype(BF16)
        return carry

    lax.fori_loop(0, s // CONV_ROWS, step, 0)


def _conv_call(u, w, b, lnw, lnb):
    bsz, s, _ = u.shape
    return pl.pallas_call(
        _conv_kernel,
        grid=(bsz,),
        in_specs=[pl.BlockSpec((1, s, CONV_CH), lambda i: (i, 0, 0)),
                  _resident((CONV_WIDTH, CONV_CH)), _resident((1, CONV_CH)),
                  _resident((1, CONV_CH)), _resident((1, CONV_CH))],
        out_specs=pl.BlockSpec((1, s, CONV_CH), lambda i: (i, 0, 0)),
        out_shape=jax.ShapeDtypeStruct((bsz, s, CONV_CH), BF16),
        scratch_shapes=[pltpu.VMEM((s + 2 * CONV_PAD, CONV_CH), F32),
                        pltpu.VMEM((CONV_ROWS, CONV_CH), F32)],
        compiler_params=pltpu.CompilerParams(dimension_semantics=("arbitrary",),
                                             vmem_limit_bytes=VMEM_LIMIT),
        name="conv_ln",
    )(u, w, b, lnw, lnb)


FF_CHUNK = 256


def _mix_ffn_kernel(x_ref, o0_ref, o1_ref, o2_ref, l0_ref, l1_ref, l2_ref, u_ref, g_ref,
                    expand_ref, wo_ref, wpw_ref, wout_ref, n2_ref, w1_ref, w2_ref,
                    out_ref, attn_ref, act_ref, perm_ref):
    def token_order(ref, group, slab, cols):
        dilation = DILATED_PATTERNS[group][1]
        if dilation == 1:
            return ref[0, :, cols].astype(F32)
        n = ROW_TILE // dilation
        for r in range(dilation):
            perm_ref[slab, pl.ds(r, n, stride=dilation), :] = ref[r, :, cols].astype(F32)
        return perm_ref[slab]

    o_refs = (o0_ref, o1_ref, o2_ref)
    all_lanes = slice(0, LANES)
    n_slabs = GROUP_CHUNKS + 1
    lse = [token_order(ref, g, g * n_slabs + GROUP_CHUNKS, all_lanes)
           for g, ref in enumerate((l0_ref, l1_ref, l2_ref))]
    m = jnp.maximum(jnp.maximum(lse[0], lse[1]), lse[2])
    e = [jnp.exp(l - m) for l in lse]
    inv = 1.0 / (e[0] + e[1] + e[2])
    for c in range(GROUP_CHUNKS):
        cols = slice(c * LANES, (c + 1) * LANES)
        acc = jnp.zeros((ROW_TILE, LANES), F32)
        for g in range(N_GROUPS):
            weight = _split_dot(e[g] * inv, expand_ref[:, cols])
            acc = acc + weight * token_order(o_refs[g], g, g * n_slabs + c, cols)
        attn_ref[:, cols] = acc.astype(BF16)

    y_a = _dot(attn_ref[...], wo_ref[...])
    y_b = _dot(u_ref[...], wpw_ref[...])
    z = g_ref[:, 0:D_MODEL] * y_a + g_ref[:, D_MODEL:2 * D_MODEL] * y_b
    x1 = x_ref[...] + _dot(z.astype(BF16), wout_ref[...])

    ms = jnp.mean(x1 * x1, axis=-1, keepdims=True)
    h2 = (x1 * lax.rsqrt(ms + EPS) * n2_ref[...]).astype(BF16)
    for n in range(0, D_FF, FF_CHUNK):
        gt = _dot(h2, w1_ref[:, n:n + FF_CHUNK])
        up = _dot(h2, w1_ref[:, D_FF + n:D_FF + n + FF_CHUNK])
        act_ref[:, n:n + FF_CHUNK] = (gt * jax.nn.sigmoid(gt) * up).astype(BF16)
    out_ref[...] = x1 + _dot(act_ref[...], w2_ref[...])


def _mix_ffn_call(x3, o, lse, u_ln, gates, expand, wo, wpw, wout, n2, w1, w2):
    bsz, s, _ = x3.shape
    dilations = [d for _, d in DILATED_PATTERNS]
    return pl.pallas_call(
        _mix_ffn_kernel,
        grid=(bsz, s // ROW_TILE),
        in_specs=[_row_block(D_MODEL)]
        + [_group_block(d, GROUP_WIDTH) for d in dilations]
        + [_group_block(d, LANES) for d in dilations]
        + [_row_block(CONV_CH), _row_block(2 * D_MODEL),
           _resident(expand.shape), _resident(wo.shape), _resident(wpw.shape),
           _resident(wout.shape), _resident(n2.shape), _resident(w1.shape), _resident(w2.shape)],
        out_specs=_row_block(D_MODEL),
        out_shape=jax.ShapeDtypeStruct((bsz, s, D_MODEL), F32),
        scratch_shapes=[pltpu.VMEM((ROW_TILE, GROUP_WIDTH), BF16),
                        pltpu.VMEM((ROW_TILE, D_FF), BF16),
                        pltpu.VMEM((N_GROUPS * (GROUP_CHUNKS + 1), ROW_TILE, LANES), F32)],
        compiler_params=pltpu.CompilerParams(dimension_semantics=("arbitrary", "arbitrary"),
                                             vmem_limit_bytes=VMEM_LIMIT),
        name="mix_ffn",
    )(x3, *o, *lse, u_ln, gates, expand, wo, wpw, wout, n2, w1, w2)


def _rope_tables(positions):
    half = ROT_DIM // 2
    inv_freq = ROPE_THETA ** (-jnp.arange(0, ROT_DIM, 2, dtype=F32) / ROT_DIM)
    ang = positions.astype(F32)[..., None] * inv_freq
    cos, sin = jnp.cos(ang), jnp.sin(ang)
    shape = ang.shape[:-1]
    ones = jnp.ones(shape + (HEAD_DIM - ROT_DIM,), F32)
    z = lambda n: jnp.zeros(shape + (n,), F32)
    c = jnp.concatenate([cos, cos, ones], axis=-1)
    sa = jnp.concatenate([-sin, z(HEAD_DIM - half)], axis=-1)
    sb = jnp.concatenate([z(half), sin, z(HEAD_DIM - ROT_DIM)], axis=-1)
    rep = lambda t: jnp.tile(t, (1,) * len(shape) + (LANES // HEAD_DIM,))
    return rep(c), rep(sa), rep(sb)


def kernel(x, positions, norm1_w, w_in, b_gate, q_norm_w, k_norm_w, w_o_attn, conv_w, conv_b,
           conv_ln_w, conv_ln_b, w_pw_conv, w_out, norm2_w, w_ffn_in, w_ffn_out):
    d = x.shape[-1]
    depth = norm1_w.shape[0]
    cos, sa, sb = _rope_tables(positions)
    lane = jnp.arange(LANES)
    seg = (lane[:, None] // HEAD_DIM == lane[None, :] // HEAD_DIM).astype(BF16)
    expand = (lane[:, None] == jnp.arange(GROUP_WIDTH)[None, :] // HEAD_DIM).astype(BF16)
    pair = lambda w: jnp.tile(w.astype(F32), LANES // HEAD_DIM).reshape(1, LANES)

    for l in range(depth):
        qkv0, qkv1, qkv2, u, gates = _proj_call(
            x, norm1_w[l].reshape(1, d), w_in[l].astype(BF16), cos, sa, sb,
            pair(q_norm_w[l]), pair(k_norm_w[l]), seg, b_gate[l].reshape(1, 2 * d))
        outs, lses = zip(*[_attn_call(qkv) for qkv in (qkv0, qkv1, qkv2)])
        u_ln = _conv_call(u, conv_w[l], conv_b[l].reshape(1, CONV_CH),
                          conv_ln_w[l].reshape(1, CONV_CH), conv_ln_b[l].reshape(1, CONV_CH))
        x = _mix_ffn_call(
            x, outs, lses, u_ln, gates, expand,
            w_o_attn[l].astype(BF16), w_pw_conv[l].astype(BF16), w_out[l].astype(BF16),
            norm2_w[l].reshape(1, d), w_ffn_in[l].astype(BF16), w_ffn_out[l].astype(BF16))
    return x
```

```python
import functools

import jax
import jax.numpy as jnp
from jax import lax
from jax.experimental import pallas as pl
from jax.experimental.pallas import tpu as pltpu

D_MODEL = 1024
HEAD_DIM = 64
N_SLOT_HEADS = 8
DILATED_PATTERNS = ((128, 1), (512, 4), (2048, 16))
N_GROUPS = len(DILATED_PATTERNS)
GROUP_WIDTH = N_SLOT_HEADS * HEAD_DIM
QKV_WIDTH = N_GROUPS * GROUP_WIDTH
ROPE_THETA = 500000.0
ROT_DIM = HEAD_DIM // 4
CONV_CH = D_MODEL // 2
CONV_WIDTH = 31
D_FF = 2816
EPS = 1e-6
NEG_INF = -1e30

LANES = 128
SUBLANES = 8
SUB_Q = 64
KEY_BLOCK = 128
HALF_SPAN = 64
LSE_LANES = 16
LOG2E = 1.4426950408889634
LN2 = 0.6931471805599453
ROW_TILE = 512
GROUP_CHUNKS = GROUP_WIDTH // LANES
VMEM_LIMIT = 56 * 1024 * 1024

F32 = jnp.float32
BF16 = jnp.bfloat16


def _dot(a, b):
    return jnp.dot(a, b, preferred_element_type=F32)


def _split_dot(a_f32, b_bf16):
    hi = a_f32.astype(BF16)
    lo = (a_f32 - hi.astype(F32)).astype(BF16)
    return _dot(hi, b_bf16) + _dot(lo, b_bf16)


def _resident(shape):
    nd = len(shape)
    return pl.BlockSpec(shape, lambda *_: (0,) * nd, pipeline_mode=pl.Buffered(1))


def _row_block(width):
    return pl.BlockSpec((None, ROW_TILE, width), lambda b, i: (b, i, 0))


def _group_block(dilation, width, lead=()):
    n_lead = len(lead)
    return pl.BlockSpec(lead + (None, dilation, ROW_TILE // dilation, width),
                        lambda b, i: (0,) * n_lead + (b, 0, i, 0))


def _proj_kernel(x_ref, n1_ref, w_ref, cos_ref, sa_ref, sb_ref, qn_ref, kn_ref, seg_ref,
                 bg_ref, qkv0_ref, qkv1_ref, qkv2_ref, u_ref, g_ref, h_ref, perm_ref):
    xt = x_ref[...]
    ms = jnp.mean(xt * xt, axis=-1, keepdims=True)
    h_ref[...] = (xt * lax.rsqrt(ms + EPS) * n1_ref[...]).astype(BF16)

    def proj(c0, width):
        return _dot(h_ref[...], w_ref[:, c0:c0 + width])

    cos, sa, sb = cos_ref[...], sa_ref[...], sb_ref[...]
    seg = seg_ref[...]

    def head_norm_rope(t, nw, scale):
        ssq = _split_dot(t * t, seg)
        y = t * lax.rsqrt(ssq * (1.0 / HEAD_DIM) + EPS) * nw
        r = y * cos + pltpu.roll(y, LANES - ROT_DIM // 2, 1) * sa + pltpu.roll(y, ROT_DIM // 2, 1) * sb
        return r * scale

    def emit(kind, group, c, val):
        out_ref = (qkv0_ref, qkv1_ref, qkv2_ref)[group]
        dilation = DILATED_PATTERNS[group][1]
        cols = slice(c * LANES, (c + 1) * LANES)
        if dilation == 1:
            out_ref[kind, 0, :, cols] = val.astype(BF16)
            return
        perm_ref[c] = val
        n = ROW_TILE // dilation
        for r in range(dilation):
            out_ref[kind, r, :, cols] = perm_ref[c, pl.ds(r, n, stride=dilation), :].astype(BF16)

    for kind, nw_ref, scale in ((0, qn_ref, HEAD_DIM ** -0.5 * LOG2E), (1, kn_ref, 1.0)):
        nw = nw_ref[...]
        for j in range(N_GROUPS):
            acc = proj(kind * QKV_WIDTH + j * GROUP_WIDTH, GROUP_WIDTH)
            for c in range(GROUP_CHUNKS):
                emit(kind, j, c, head_norm_rope(acc[:, c * LANES:(c + 1) * LANES], nw, scale))

    for j in range(N_GROUPS):
        acc = proj(2 * QKV_WIDTH + j * GROUP_WIDTH, GROUP_WIDTH)
        for c in range(GROUP_CHUNKS):
            emit(2, j, c, acc[:, c * LANES:(c + 1) * LANES])

    conv0 = 3 * QKV_WIDTH
    a = proj(conv0, CONV_CH)
    b = proj(conv0 + CONV_CH, CONV_CH)
    u_ref[...] = a * jax.nn.sigmoid(b)

    gate0 = conv0 + 2 * CONV_CH
    for j in range(2 * D_MODEL // GROUP_WIDTH):
        lo = j * GROUP_WIDTH
        g_ref[:, lo:lo + GROUP_WIDTH] = jax.nn.sigmoid(
            proj(gate0 + lo, GROUP_WIDTH) + bg_ref[:, lo:lo + GROUP_WIDTH])


def _proj_call(x3, n1, w_in, cos, sa, sb, qn, kn, seg, bg):
    bsz, s, _ = x3.shape
    in_width = w_in.shape[1]
    qkv_shapes = [jax.ShapeDtypeStruct((3, bsz, d, s // d, GROUP_WIDTH), BF16)
                  for _, d in DILATED_PATTERNS]
    return pl.pallas_call(
        _proj_kernel,
        grid=(bsz, s // ROW_TILE),
        in_specs=[_row_block(D_MODEL), _resident((1, D_MODEL)), _resident((D_MODEL, in_width)),
                  _row_block(LANES), _row_block(LANES), _row_block(LANES),
                  _resident((1, LANES)), _resident((1, LANES)), _resident((LANES, LANES)),
                  _resident((1, 2 * D_MODEL))],
        out_specs=[_group_block(d, GROUP_WIDTH, lead=(3,)) for _, d in DILATED_PATTERNS]
        + [_row_block(CONV_CH), _row_block(2 * D_MODEL)],
        out_shape=qkv_shapes + [jax.ShapeDtypeStruct((bsz, s, CONV_CH), F32),
                                jax.ShapeDtypeStruct((bsz, s, 2 * D_MODEL), F32)],
        scratch_shapes=[pltpu.VMEM((ROW_TILE, D_MODEL), BF16),
                        pltpu.VMEM((GROUP_CHUNKS, ROW_TILE, LANES), F32)],
        compiler_params=pltpu.CompilerParams(dimension_semantics=("arbitrary", "arbitrary"),
                                             vmem_limit_bytes=VMEM_LIMIT),
        name="proj",
    )(x3, n1, w_in, cos, sa, sb, qn, kn, seg, bg)


def _attn_kernel(q_ref, k_ref, v_ref, o_ref, lse_ref, kpad_ref, vt_ref, bias_ref):
    n_seq, seq_len, _ = q_ref.shape
    whole = seq_len == KEY_BLOCK
    kw = KEY_BLOCK if whole else 3 * SUB_Q
    n_key_blocks = seq_len // KEY_BLOCK
    vt_pad = 0 if whole else 1
    last_sub = seq_len // SUB_Q - 1
    lane = lax.broadcasted_iota(jnp.int32, (1, LANES), 1)
    first_head = lane < HEAD_DIM
    block_diag = (lax.broadcasted_iota(jnp.int32, (LANES, LANES), 0) < HEAD_DIM) == first_head

    @pl.when(pl.program_id(0) == 0)
    def _init():
        row = lax.broadcasted_iota(jnp.int32, (kw, LANES), 0)
        lq = lax.broadcasted_iota(jnp.int32, (kw, LANES), 1) % SUB_Q
        as_bias = lambda valid: jnp.where(valid, 0.0, NEG_INF).astype(F32)
        if whole:
            for case in range(2):
                bias_ref[case] = as_bias(jnp.abs(lq + case * SUB_Q - row) <= HALF_SPAN)
        else:
            band = jnp.abs(lq + SUB_Q - row) <= HALF_SPAN
            bias_ref[0] = as_bias(band & (row >= SUB_Q))
            bias_ref[1] = as_bias(band)
            bias_ref[2] = as_bias(band & (row < 2 * SUB_Q))
            zeros = jnp.zeros((SUB_Q, GROUP_WIDTH), BF16)
            kpad_ref[0:SUB_Q, :] = zeros
            kpad_ref[SUB_Q + seq_len:2 * SUB_Q + seq_len, :] = zeros
            zero_block = jnp.zeros((GROUP_WIDTH, KEY_BLOCK), BF16)
            vt_ref[0] = zero_block
            vt_ref[n_key_blocks + 1] = zero_block

    def sequence(r, carry):
        if not whole:
            kpad_ref[SUB_Q:SUB_Q + seq_len, :] = k_ref[r]

        def stage_vt(b, c):
            rows = pl.ds(pl.multiple_of(b * KEY_BLOCK, KEY_BLOCK), KEY_BLOCK)
            for hp in range(GROUP_CHUNKS):
                cols = slice(hp * LANES, (hp + 1) * LANES)
                vt_ref[b + vt_pad, cols, :] = v_ref[r, rows, cols].astype(F32).T.astype(BF16)
            return c

        lax.fori_loop(0, n_key_blocks, stage_vt, 0)

        def query_block(i, c):
            for sub in range(KEY_BLOCK // SUB_Q):
                t = 2 * i + sub
                q0 = pl.multiple_of(t * SUB_Q, SUB_Q)
                if whole:
                    bias = bias_ref[sub]
                else:
                    bias = bias_ref[jnp.where(t == 0, 0, jnp.where(t == last_sub, 2, 1))]
                lse_tile = jnp.zeros((SUB_Q, LANES), F32)
                for hp in range(GROUP_CHUNKS):
                    cols = slice(hp * LANES, (hp + 1) * LANES)
                    qb = q_ref[r, pl.ds(q0, SUB_Q), cols]
                    zq = jnp.zeros_like(qb)
                    qv = jnp.concatenate([jnp.where(first_head, qb, zq),
                                          jnp.where(first_head, zq, qb)], axis=0)
                    if whole:
                        kwin = k_ref[r, :, cols]
                    else:
                        kwin = kpad_ref[pl.ds(q0, kw), cols]
                    s = lax.dot_general(kwin, qv, (((1,), (1,)), ((), ())),
                                        preferred_element_type=F32) + bias
                    m = jnp.max(s, axis=0, keepdims=True)
                    p = jnp.exp2(s - m)
                    den = jnp.sum(p, axis=0, keepdims=True)
                    pb = p.astype(BF16)
                    if whole:
                        ot = _dot(vt_ref[0, cols, :], pb)
                    else:
                        zp = jnp.zeros((SUB_Q, LANES), BF16)
                        if sub == 0:
                            ot = (_dot(vt_ref[i, cols, :], jnp.concatenate([zp, pb[0:SUB_Q]], axis=0))
                                  + _dot(vt_ref[i + 1, cols, :], pb[SUB_Q:kw]))
                        else:
                            ot = (_dot(vt_ref[i + 1, cols, :], pb[0:KEY_BLOCK])
                                  + _dot(vt_ref[i + 2, cols, :],
                                         jnp.concatenate([pb[KEY_BLOCK:kw], zp], axis=0)))
                    lse = (m + jnp.log2(den)) * LN2
                    tile = jnp.where(block_diag, ot * (1.0 / den), lse).T
                    top, bot = tile[0:SUB_Q], tile[SUB_Q:2 * SUB_Q]
                    o_ref[r, pl.ds(q0, SUB_Q), cols] = jnp.where(first_head, top, bot).astype(BF16)
                    lse_tile = jnp.where((lane % HEAD_DIM) // LSE_LANES == hp,
                                         jnp.where(first_head, bot, top), lse_tile)
                lse_ref[r, pl.ds(q0, SUB_Q), :] = lse_tile
            return c

        lax.fori_loop(0, n_key_blocks, query_block, 0)
        return carry

    lax.fori_loop(0, n_seq, sequence, 0)


def _attn_call(qkv):
    _, bsz, dilation, seq_len, _ = qkv.shape
    whole = seq_len == KEY_BLOCK
    in_spec = lambda kind: pl.BlockSpec((None, None, dilation, seq_len, GROUP_WIDTH),
                                        lambda b: (kind, b, 0, 0, 0))
    out_spec = lambda w: pl.BlockSpec((None, dilation, seq_len, w), lambda b: (b, 0, 0, 0))
    return pl.pallas_call(
        _attn_kernel,
        grid=(bsz,),
        in_specs=[in_spec(0), in_spec(1), in_spec(2)],
        out_specs=[out_spec(GROUP_WIDTH), out_spec(LANES)],
        out_shape=[jax.ShapeDtypeStruct((bsz, dilation, seq_len, GROUP_WIDTH), BF16),
                   jax.ShapeDtypeStruct((bsz, dilation, seq_len, LANES), F32)],
        scratch_shapes=[
            pltpu.VMEM((seq_len + 2 * SUB_Q, GROUP_WIDTH), BF16),
            pltpu.VMEM((seq_len // KEY_BLOCK + (0 if whole else 2), GROUP_WIDTH, KEY_BLOCK), BF16),
            pltpu.VMEM((2 if whole else 3, KEY_BLOCK if whole else 3 * SUB_Q, LANES), F32)],
        compiler_params=pltpu.CompilerParams(dimension_semantics=("arbitrary",),
                                             vmem_limit_bytes=VMEM_LIMIT),
        name=f"attn_d{dilation}",
    )(qkv, qkv, qkv)


CONV_PAD = 16
CONV_ROWS = 128


def _conv_kernel(u_ref, w_ref, b_ref, lnw_ref, lnb_ref, o_ref, pad_ref, conv_ref):
    s = u_ref.shape[1]
    zeros = jnp.zeros((CONV_PAD, CONV_CH), F32)
    pad_ref[0:CONV_PAD, :] = zeros
    pad_ref[CONV_PAD + s:2 * CONV_PAD + s, :] = zeros
    pad_ref[CONV_PAD:CONV_PAD + s, :] = u_ref[0]
    n_p = (CONV_WIDTH + SUBLANES) // SUBLANES
    span = CONV_ROWS + (n_p - 1) * SUBLANES

    def step(i, carry):
        r0 = pl.multiple_of(i * CONV_ROWS, CONV_ROWS)
        for c in range(CONV_CH // LANES):
            cols = slice(c * LANES, (c + 1) * LANES)
            win = pad_ref[pl.ds(r0, CONV_ROWS + 2 * CONV_PAD), cols]
            acc = jnp.zeros((CONV_ROWS, LANES), F32) + b_ref[:, cols]
            for j in range(SUBLANES):
                shifted = win[j:j + span]
                for p in range(n_p):
                    t = SUBLANES * p + j - 1
                    if 0 <= t < CONV_WIDTH:
                        acc = acc + shifted[SUBLANES * p:SUBLANES * p + CONV_ROWS] * w_ref[t:t + 1, cols]
            conv_ref[:, cols] = acc
        y = conv_ref[...]
        mu = jnp.mean(y, axis=-1, keepdims=True)
        cen = y - mu
        var = jnp.mean(cen * cen, axis=-1, keepdims=True)
        y = cen * lax.rsqrt(var + EPS) * lnw_ref[...] + lnb_ref[...]
        o_ref[0, pl.ds(r0, CONV_ROWS), :] = (y * jax.nn.sigmoid(y)).astype(BF16)
        return carry

    lax.fori_loop(0, s // CONV_ROWS, step, 0)


def _conv_call(u, w, b, lnw, lnb):
    bsz, s, _ = u.shape
    return pl.pallas_call(
        _conv_kernel,
        grid=(bsz,),
        in_specs=[pl.BlockSpec((1, s, CONV_CH), lambda i: (i, 0, 0)),
                  _resident((CONV_WIDTH, CONV_CH)), _resident((1, CONV_CH)),
                  _resident((1, CONV_CH)), _resident((1, CONV_CH))],
        out_specs=pl.BlockSpec((1, s, CONV_CH), lambda i: (i, 0, 0)),
        out_shape=jax.ShapeDtypeStruct((bsz, s, CONV_CH), BF16),
        scratch_shapes=[pltpu.VMEM((s + 2 * CONV_PAD, CONV_CH), F32),
                        pltpu.VMEM((CONV_ROWS, CONV_CH), F32)],
        compiler_params=pltpu.CompilerParams(dimension_semantics=("arbitrary",),
                                             vmem_limit_bytes=VMEM_LIMIT),
        name="conv_ln",
    )(u, w, b, lnw, lnb)


FF_CHUNK = 256


def _mix_ffn_kernel(x_ref, o0_ref, o1_ref, o2_ref, l0_ref, l1_ref, l2_ref, u_ref, g_ref,
                    expand_ref, wo_ref, wpw_ref, wout_ref, n2_ref, w1_ref, w2_ref,
                    out_ref, attn_ref, act_ref, perm_ref):
    def token_order(ref, group, slab, cols):
        dilation = DILATED_PATTERNS[group][1]
        if dilation == 1:
            return ref[0, :, cols].astype(F32)
        n = ROW_TILE // dilation
        for r in range(dilation):
            perm_ref[slab, pl.ds(r, n, stride=dilation), :] = ref[r, :, cols].astype(F32)
        return perm_ref[slab]

    o_refs = (o0_ref, o1_ref, o2_ref)
    all_lanes = slice(0, LANES)
    n_slabs = GROUP_CHUNKS + 1
    lse = [token_order(ref, g, g * n_slabs + GROUP_CHUNKS, all_lanes)
           for g, ref in enumerate((l0_ref, l1_ref, l2_ref))]
    m = jnp.maximum(jnp.maximum(lse[0], lse[1]), lse[2])
    e = [jnp.exp(l - m) for l in lse]
    inv = 1.0 / (e[0] + e[1] + e[2])
    for c in range(GROUP_CHUNKS):
        cols = slice(c * LANES, (c + 1) * LANES)
        acc = jnp.zeros((ROW_TILE, LANES), F32)
        for g in range(N_GROUPS):
            weight = _split_dot(e[g] * inv, expand_ref[:, cols])
            acc = acc + weight * token_order(o_refs[g], g, g * n_slabs + c, cols)
        attn_ref[:, cols] = acc.astype(BF16)

    y_a = _dot(attn_ref[...], wo_ref[...])
    y_b = _dot(u_ref[...], wpw_ref[...])
    z = g_ref[:, 0:D_MODEL] * y_a + g_ref[:, D_MODEL:2 * D_MODEL] * y_b
    x1 = x_ref[...] + _dot(z.astype(BF16), wout_ref[...])

    ms = jnp.mean(x1 * x1, axis=-1, keepdims=True)
    h2 = (x1 * lax.rsqrt(ms + EPS) * n2_ref[...]).astype(BF16)
    for n in range(0, D_FF, FF_CHUNK):
        gt = _dot(h2, w1_ref[:, n:n + FF_CHUNK])
        up = _dot(h2, w1_ref[:, D_FF + n:D_FF + n + FF_CHUNK])
        act_ref[:, n:n + FF_CHUNK] = (gt * jax.nn.sigmoid(gt) * up).astype(BF16)
    out_ref[...] = x1 + _dot(act_ref[...], w2_ref[...])


def _mix_ffn_call(x3, o, lse, u_ln, gates, expand, wo, wpw, wout, n2, w1, w2):
    bsz, s, _ = x3.shape
    dilations = [d for _, d in DILATED_PATTERNS]
    return pl.pallas_call(
        _mix_ffn_kernel,
        grid=(bsz, s // ROW_TILE),
        in_specs=[_row_block(D_MODEL)]
        + [_group_block(d, GROUP_WIDTH) for d in dilations]
        + [_group_block(d, LANES) for d in dilations]
        + [_row_block(CONV_CH), _row_block(2 * D_MODEL),
           _resident(expand.shape), _resident(wo.shape), _resident(wpw.shape),
           _resident(wout.shape), _resident(n2.shape), _resident(w1.shape), _resident(w2.shape)],
        out_specs=_row_block(D_MODEL),
        out_shape=jax.ShapeDtypeStruct((bsz, s, D_MODEL), F32),
        scratch_shapes=[pltpu.VMEM((ROW_TILE, GROUP_WIDTH), BF16),
                        pltpu.VMEM((ROW_TILE, D_FF), BF16),
                        pltpu.VMEM((N_GROUPS * (GROUP_CHUNKS + 1), ROW_TILE, LANES), F32)],
        compiler_params=pltpu.CompilerParams(dimension_semantics=("arbitrary", "arbitrary"),
                                             vmem_limit_bytes=VMEM_LIMIT),
        name="mix_ffn",
    )(x3, *o, *lse, u_ln, gates, expand, wo, wpw, wout, n2, w1, w2)


def _rope_tables(positions):
    half = ROT_DIM // 2
    inv_freq = ROPE_THETA ** (-jnp.arange(0, ROT_DIM, 2, dtype=F32) / ROT_DIM)
    ang = positions.astype(F32)[..., None] * inv_freq
    cos, sin = jnp.cos(ang), jnp.sin(ang)
    shape = ang.shape[:-1]
    ones = jnp.ones(shape + (HEAD_DIM - ROT_DIM,), F32)
    z = lambda n: jnp.zeros(shape + (n,), F32)
    c = jnp.concatenate([cos, cos, ones], axis=-1)
    sa = jnp.concatenate([-sin, z(HEAD_DIM - half)], axis=-1)
    sb = jnp.concatenate([z(half), sin, z(HEAD_DIM - ROT_DIM)], axis=-1)
    rep = lambda t: jnp.tile(t, (1,) * len(shape) + (LANES // HEAD_DIM,))
    return rep(c), rep(sa), rep(sb)


def kernel(x, positions, norm1_w, w_in, b_gate, q_norm_w, k_norm_w, w_o_attn, conv_w, conv_b,
           conv_ln_w, conv_ln_b, w_pw_conv, w_out, norm2_w, w_ffn_in, w_ffn_out):
    d = x.shape[-1]
    depth = norm1_w.shape[0]
    cos, sa, sb = _rope_tables(positions)
    lane = jnp.arange(LANES)
    seg = (lane[:, None] // HEAD_DIM == lane[None, :] // HEAD_DIM).astype(BF16)
    head = jnp.arange(GROUP_WIDTH) // HEAD_DIM
    lse_lane = LSE_LANES * (head // 2) + HEAD_DIM * (1 - head % 2)
    expand = (lane[:, None] == lse_lane[None, :]).astype(BF16)
    pair = lambda w: jnp.tile(w.astype(F32), LANES // HEAD_DIM).reshape(1, LANES)

    for l in range(depth):
        qkv0, qkv1, qkv2, u, gates = _proj_call(
            x, norm1_w[l].reshape(1, d), w_in[l].astype(BF16), cos, sa, sb,
            pair(q_norm_w[l]), pair(k_norm_w[l]), seg, b_gate[l].reshape(1, 2 * d))
        outs, lses = zip(*[_attn_call(qkv) for qkv in (qkv0, qkv1, qkv2)])
        u_ln = _conv_call(u, conv_w[l], conv_b[l].reshape(1, CONV_CH),
                          conv_ln_w[l].reshape(1, CONV_CH), conv_ln_b[l].reshape(1, CONV_CH))
        x = _mix_ffn_call(
            x, outs, lses, u_ln, gates, expand,
            w_o_attn[l].astype(BF16), w_pw_conv[l].astype(BF16), w_out[l].astype(BF16),
            norm2_w[l].reshape(1, d), w_ffn_in[l].astype(BF16), w_ffn_out[l].astype(BF16))
    return x
```

```python
import functools

import jax
import jax.numpy as jnp
from jax import lax
from jax.experimental import pallas as pl
from jax.experimental.pallas import tpu as pltpu

D_MODEL = 1024
HEAD_DIM = 64
N_SLOT_HEADS = 8
DILATED_PATTERNS = ((128, 1), (512, 4), (2048, 16))
N_GROUPS = len(DILATED_PATTERNS)
GROUP_WIDTH = N_SLOT_HEADS * HEAD_DIM
QKV_WIDTH = N_GROUPS * GROUP_WIDTH
ROPE_THETA = 500000.0
ROT_DIM = HEAD_DIM // 4
CONV_CH = D_MODEL // 2
CONV_WIDTH = 31
D_FF = 2816
EPS = 1e-6
NEG_INF = -1e30

LANES = 128
SUBLANES = 8
SUB_Q = 64
KEY_BLOCK = 128
HALF_SPAN = 64
LSE_LANES = 16
LOG2E = 1.4426950408889634
LN2 = 0.6931471805599453
ROW_TILE = 512
GROUP_CHUNKS = GROUP_WIDTH // LANES
VMEM_LIMIT = 56 * 1024 * 1024

F32 = jnp.float32
BF16 = jnp.bfloat16


def _dot(a, b):
    return jnp.dot(a, b, preferred_element_type=F32)


def _split_dot(a_f32, b_bf16):
    hi = a_f32.astype(BF16)
    lo = (a_f32 - hi.astype(F32)).astype(BF16)
    return _dot(hi, b_bf16) + _dot(lo, b_bf16)


def _resident(shape):
    nd = len(shape)
    return pl.BlockSpec(shape, lambda *_: (0,) * nd, pipeline_mode=pl.Buffered(1))


def _row_block(width):
    return pl.BlockSpec((None, ROW_TILE, width), lambda b, i: (b, i, 0))


def _group_block(dilation, width, lead=()):
    n_lead = len(lead)
    return pl.BlockSpec(lead + (None, dilation, ROW_TILE // dilation, width),
                        lambda b, i: (0,) * n_lead + (b, 0, i, 0))


def _proj_kernel(x_ref, n1_ref, w_ref, cos_ref, sa_ref, sb_ref, qn_ref, kn_ref, seg_ref,
                 bg_ref, qkv0_ref, qkv1_ref, qkv2_ref, u_ref, g_ref, h_ref, perm_ref):
    xt = x_ref[...]
    ms = jnp.mean(xt * xt, axis=-1, keepdims=True)
    h_ref[...] = (xt * lax.rsqrt(ms + EPS) * n1_ref[...]).astype(BF16)

    def proj(c0, width):
        return _dot(h_ref[...], w_ref[:, c0:c0 + width])

    cos, sa, sb = cos_ref[...], sa_ref[...], sb_ref[...]
    seg = seg_ref[...]

    def head_norm_rope(t, nw, scale):
        ssq = _split_dot(t * t, seg)
        y = t * lax.rsqrt(ssq * (1.0 / HEAD_DIM) + EPS) * nw
        r = y * cos + pltpu.roll(y, LANES - ROT_DIM // 2, 1) * sa + pltpu.roll(y, ROT_DIM // 2, 1) * sb
        return r * scale

    def emit(kind, group, c, val):
        out_ref = (qkv0_ref, qkv1_ref, qkv2_ref)[group]
        dilation = DILATED_PATTERNS[group][1]
        cols = slice(c * LANES, (c + 1) * LANES)
        if dilation == 1:
            out_ref[kind, 0, :, cols] = val.astype(BF16)
            return
        perm_ref[c] = val
        n = ROW_TILE // dilation
        for r in range(dilation):
            out_ref[kind, r, :, cols] = perm_ref[c, pl.ds(r, n, stride=dilation), :].astype(BF16)

    for kind, nw_ref, scale in ((0, qn_ref, HEAD_DIM ** -0.5 * LOG2E), (1, kn_ref, 1.0)):
        nw = nw_ref[...]
        for j in range(N_GROUPS):
            acc = proj(kind * QKV_WIDTH + j * GROUP_WIDTH, GROUP_WIDTH)
            for c in range(GROUP_CHUNKS):
                emit(kind, j, c, head_norm_rope(acc[:, c * LANES:(c + 1) * LANES], nw, scale))

    for j in range(N_GROUPS):
        acc = proj(2 * QKV_WIDTH + j * GROUP_WIDTH, GROUP_WIDTH)
        for c in range(GROUP_CHUNKS):
            emit(2, j, c, acc[:, c * LANES:(c + 1) * LANES])

    conv0 = 3 * QKV_WIDTH
    a = proj(conv0, CONV_CH)
    b = proj(conv0 + CONV_CH, CONV_CH)
    u_ref[...] = a * jax.nn.sigmoid(b)

    gate0 = conv0 + 2 * CONV_CH
    for j in range(2 * D_MODEL // GROUP_WIDTH):
        lo = j * GROUP_WIDTH
        g_ref[:, lo:lo + GROUP_WIDTH] = jax.nn.sigmoid(
            proj(gate0 + lo, GROUP_WIDTH) + bg_ref[:, lo:lo + GROUP_WIDTH])


def _proj_call(x3, n1, w_in, cos, sa, sb, qn, kn, seg, bg):
    bsz, s, _ = x3.shape
    in_width = w_in.shape[1]
    qkv_shapes = [jax.ShapeDtypeStruct((3, bsz, d, s // d, GROUP_WIDTH), BF16)
                  for _, d in DILATED_PATTERNS]
    return pl.pallas_call(
        _proj_kernel,
        grid=(bsz, s // ROW_TILE),
        in_specs=[_row_block(D_MODEL), _resident((1, D_MODEL)), _resident((D_MODEL, in_width)),
                  _row_block(LANES), _row_block(LANES), _row_block(LANES),
                  _resident((1, LANES)), _resident((1, LANES)), _resident((LANES, LANES)),
                  _resident((1, 2 * D_MODEL))],
        out_specs=[_group_block(d, GROUP_WIDTH, lead=(3,)) for _, d in DILATED_PATTERNS]
        + [_row_block(CONV_CH), _row_block(2 * D_MODEL)],
        out_shape=qkv_shapes + [jax.ShapeDtypeStruct((bsz, s, CONV_CH), F32),
                                jax.ShapeDtypeStruct((bsz, s, 2 * D_MODEL), F32)],
        scratch_shapes=[pltpu.VMEM((ROW_TILE, D_MODEL), BF16),
                        pltpu.VMEM((GROUP_CHUNKS, ROW_TILE, LANES), F32)],
        compiler_params=pltpu.CompilerParams(dimension_semantics=("arbitrary", "arbitrary"),
                                             vmem_limit_bytes=VMEM_LIMIT),
        name="proj",
    )(x3, n1, w_in, cos, sa, sb, qn, kn, seg, bg)


def _attn_kernel(q_ref, k_ref, v_ref, o_ref, lse_ref, kpad_ref, vt_ref, bias_ref, p_ref, stat_ref):
    n_seq, seq_len, _ = q_ref.shape
    whole = seq_len == KEY_BLOCK
    kw = KEY_BLOCK if whole else 3 * SUB_Q
    n_key_blocks = seq_len // KEY_BLOCK
    n_blocks = n_seq * n_key_blocks
    vt_pad = 0 if whole else 1
    last_sub = seq_len // SUB_Q - 1
    subs = KEY_BLOCK // SUB_Q
    lane = lax.broadcasted_iota(jnp.int32, (1, LANES), 1)
    first_head = lane < HEAD_DIM
    block_diag = (lax.broadcasted_iota(jnp.int32, (LANES, LANES), 0) < HEAD_DIM) == first_head

    @pl.when(pl.program_id(0) == 0)
    def _init():
        row = lax.broadcasted_iota(jnp.int32, (kw, LANES), 0)
        lq = lax.broadcasted_iota(jnp.int32, (kw, LANES), 1) % SUB_Q
        as_bias = lambda valid: jnp.where(valid, 0.0, NEG_INF).astype(F32)
        if whole:
            for case in range(subs):
                bias_ref[case] = as_bias(jnp.abs(lq + case * SUB_Q - row) <= HALF_SPAN)
        else:
            band = jnp.abs(lq + SUB_Q - row) <= HALF_SPAN
            bias_ref[0] = as_bias(band & (row >= SUB_Q))
            bias_ref[1] = as_bias(band)
            bias_ref[2] = as_bias(band & (row < 2 * SUB_Q))
            zeros = jnp.zeros((SUB_Q, GROUP_WIDTH), BF16)
            zero_block = jnp.zeros((GROUP_WIDTH, KEY_BLOCK), BF16)
            for r in range(n_seq):
                kpad_ref[r, 0:SUB_Q, :] = zeros
                kpad_ref[r, SUB_Q + seq_len:2 * SUB_Q + seq_len, :] = zeros
                vt_ref[r, 0] = zero_block
                vt_ref[r, n_key_blocks + 1] = zero_block

    def split(j):
        return j // n_key_blocks, j % n_key_blocks

    def stage(j, c):
        r, b = split(j)
        rows = pl.ds(pl.multiple_of(b * KEY_BLOCK, KEY_BLOCK), KEY_BLOCK)
        if not whole:
            kpad_ref[r, pl.ds(pl.multiple_of(b * KEY_BLOCK + SUB_Q, SUB_Q), KEY_BLOCK), :] = k_ref[r, rows, :]
        for hp in range(GROUP_CHUNKS):
            cols = slice(hp * LANES, (hp + 1) * LANES)
            vt_ref[r, b + vt_pad, cols, :] = v_ref[r, rows, cols].astype(F32).T.astype(BF16)
        return c

    lax.fori_loop(0, n_blocks, stage, 0)

    def scores(j):
        r, i = split(j)
        for sub in range(subs):
            t = subs * i + sub
            q0 = pl.multiple_of(t * SUB_Q, SUB_Q)
            if whole:
                bias = bias_ref[sub]
            else:
                bias = bias_ref[jnp.where(t == 0, 0, jnp.where(t == last_sub, 2, 1))]
            for hp in range(GROUP_CHUNKS):
                unit = sub * GROUP_CHUNKS + hp
                cols = slice(hp * LANES, (hp + 1) * LANES)
                qb = q_ref[r, pl.ds(q0, SUB_Q), cols]
                zq = jnp.zeros_like(qb)
                qv = jnp.concatenate([jnp.where(first_head, qb, zq),
                                      jnp.where(first_head, zq, qb)], axis=0)
                kwin = k_ref[r, :, cols] if whole else kpad_ref[r, pl.ds(q0, kw), cols]
                s = lax.dot_general(kwin, qv, (((1,), (1,)), ((), ())),
                                    preferred_element_type=F32) + bias
                m = jnp.max(s, axis=0, keepdims=True)
                p = jnp.exp2(s - m)
                den = jnp.sum(p, axis=0, keepdims=True)
                p_ref[unit] = p.astype(BF16)
                stat_ref[unit, 0:1, :] = 1.0 / den
                stat_ref[unit, 1:2, :] = (m + jnp.log2(den)) * LN2

    def outputs(j):
        r, i = split(j)
        for sub in range(subs):
            q0 = pl.multiple_of((subs * i + sub) * SUB_Q, SUB_Q)
            lse_tile = jnp.zeros((SUB_Q, LANES), F32)
            for hp in range(GROUP_CHUNKS):
                unit = sub * GROUP_CHUNKS + hp
                cols = slice(hp * LANES, (hp + 1) * LANES)
                pb = p_ref[unit]
                if whole:
                    ot = _dot(vt_ref[r, 0, cols, :], pb)
                else:
                    first = i + sub
                    vt2 = jnp.concatenate([vt_ref[r, first, cols, :], vt_ref[r, first + 1, cols, :]], axis=1)
                    zp = jnp.zeros((SUB_Q, LANES), BF16)
                    ot = _dot(vt2, jnp.concatenate([zp, pb] if sub == 0 else [pb, zp], axis=0))
                tile = jnp.where(block_diag, ot * stat_ref[unit, 0:1, :], stat_ref[unit, 1:2, :]).T
                top, bot = tile[0:SUB_Q], tile[SUB_Q:2 * SUB_Q]
                o_ref[r, pl.ds(q0, SUB_Q), cols] = jnp.where(first_head, top, bot).astype(BF16)
                lse_tile = jnp.where((lane % HEAD_DIM) // LSE_LANES == hp,
                                     jnp.where(first_head, bot, top), lse_tile)
            lse_ref[r, pl.ds(q0, SUB_Q), :] = lse_tile

    scores(0)

    def pipelined(j, c):
        outputs(j - 1)
        scores(j)
        return c

    lax.fori_loop(1, n_blocks, pipelined, 0)
    outputs(n_blocks - 1)


def _attn_call(qkv):
    _, bsz, dilation, seq_len, _ = qkv.shape
    whole = seq_len == KEY_BLOCK
    kw = KEY_BLOCK if whole else 3 * SUB_Q
    units = (KEY_BLOCK // SUB_Q) * GROUP_CHUNKS
    in_spec = lambda kind: pl.BlockSpec((None, None, dilation, seq_len, GROUP_WIDTH),
                                        lambda b: (kind, b, 0, 0, 0))
    out_spec = lambda w: pl.BlockSpec((None, dilation, seq_len, w), lambda b: (b, 0, 0, 0))
    return pl.pallas_call(
        _attn_kernel,
        grid=(bsz,),
        in_specs=[in_spec(0), in_spec(1), in_spec(2)],
        out_specs=[out_spec(GROUP_WIDTH), out_spec(LANES)],
        out_shape=[jax.ShapeDtypeStruct((bsz, dilation, seq_len, GROUP_WIDTH), BF16),
                   jax.ShapeDtypeStruct((bsz, dilation, seq_len, LANES), F32)],
        scratch_shapes=[
            pltpu.VMEM((dilation, seq_len + 2 * SUB_Q, GROUP_WIDTH), BF16),
            pltpu.VMEM((dilation, seq_len // KEY_BLOCK + (0 if whole else 2), GROUP_WIDTH, KEY_BLOCK), BF16),
            pltpu.VMEM((2 if whole else 3, kw, LANES), F32),
            pltpu.VMEM((units, kw, LANES), BF16),
            pltpu.VMEM((units, SUBLANES, LANES), F32)],
        compiler_params=pltpu.CompilerParams(dimension_semantics=("arbitrary",),
                                             vmem_limit_bytes=VMEM_LIMIT),
        name=f"attn_d{dilation}",
    )(qkv, qkv, qkv)


CONV_PAD = 16
CONV_ROWS = 128


def _conv_kernel(u_ref, w_ref, b_ref, lnw_ref, lnb_ref, o_ref, pad_ref, conv_ref):
    s = u_ref.shape[1]
    zeros = jnp.zeros((CONV_PAD, CONV_CH), F32)
    pad_ref[0:CONV_PAD, :] = zeros
    pad_ref[CONV_PAD + s:2 * CONV_PAD + s, :] = zeros
    pad_ref[CONV_PAD:CONV_PAD + s, :] = u_ref[0]
    n_p = (CONV_WIDTH + SUBLANES) // SUBLANES
    span = CONV_ROWS + (n_p - 1) * SUBLANES

    def step(i, carry):
        r0 = pl.multiple_of(i * CONV_ROWS, CONV_ROWS)
        for c in range(CONV_CH // LANES):
            cols = slice(c * LANES, (c + 1) * LANES)
            win = pad_ref[pl.ds(r0, CONV_ROWS + 2 * CONV_PAD), cols]
            acc = jnp.zeros((CONV_ROWS, LANES), F32) + b_ref[:, cols]
            for j in range(SUBLANES):
                shifted = win[j:j + span]
                for p in range(n_p):
                    t = SUBLANES * p + j - 1
                    if 0 <= t < CONV_WIDTH:
                        acc = acc + shifted[SUBLANES * p:SUBLANES * p + CONV_ROWS] * w_ref[t:t + 1, cols]
            conv_ref[:, cols] = acc
        y = conv_ref[...]
        mu = jnp.mean(y, axis=-1, keepdims=True)
        cen = y - mu
        var = jnp.mean(cen * cen, axis=-1, keepdims=True)
        y = cen * lax.rsqrt(var + EPS) * lnw_ref[...] + lnb_ref[...]
        o_ref[0, pl.ds(r0, CONV_ROWS), :] = (y * jax.nn.sigmoid(y)).astype(BF16)
        return carry

    lax.fori_loop(0, s // CONV_ROWS, step, 0)


def _conv_call(u, w, b, lnw, lnb):
    bsz, s, _ = u.shape
    return pl.pallas_call(
        _conv_kernel,
        grid=(bsz,),
        in_specs=[pl.BlockSpec((1, s, CONV_CH), lambda i: (i, 0, 0)),
                  _resident((CONV_WIDTH, CONV_CH)), _resident((1, CONV_CH)),
                  _resident((1, CONV_CH)), _resident((1, CONV_CH))],
        out_specs=pl.BlockSpec((1, s, CONV_CH), lambda i: (i, 0, 0)),
        out_shape=jax.ShapeDtypeStruct((bsz, s, CONV_CH), BF16),
        scratch_shapes=[pltpu.VMEM((s + 2 * CONV_PAD, CONV_CH), F32),
                        pltpu.VMEM((CONV_ROWS, CONV_CH), F32)],
        compiler_params=pltpu.CompilerParams(dimension_semantics=("arbitrary",),
                                             vmem_limit_bytes=VMEM_LIMIT),
        name="conv_ln",
    )(u, w, b, lnw, lnb)


FF_CHUNK = 256


def _mix_ffn_kernel(x_ref, o0_ref, o1_ref, o2_ref, l0_ref, l1_ref, l2_ref, u_ref, g_ref,
                    expand_ref, wo_ref, wpw_ref, wout_ref, n2_ref, w1_ref, w2_ref,
                    out_ref, attn_ref, act_ref, perm_ref):
    def token_order(ref, group, slab, cols):
        dilation = DILATED_PATTERNS[group][1]
        if dilation == 1:
            return ref[0, :, cols].astype(F32)
        n = ROW_TILE // dilation
        for r in range(dilation):
            perm_ref[slab, pl.ds(r, n, stride=dilation), :] = ref[r, :, cols].astype(F32)
        return perm_ref[slab]

    o_refs = (o0_ref, o1_ref, o2_ref)
    all_lanes = slice(0, LANES)
    n_slabs = GROUP_CHUNKS + 1
    lse = [token_order(ref, g, g * n_slabs + GROUP_CHUNKS, all_lanes)
           for g, ref in enumerate((l0_ref, l1_ref, l2_ref))]
    m = jnp.maximum(jnp.maximum(lse[0], lse[1]), lse[2])
    e = [jnp.exp(l - m) for l in lse]
    inv = 1.0 / (e[0] + e[1] + e[2])
    for c in range(GROUP_CHUNKS):
        cols = slice(c * LANES, (c + 1) * LANES)
        acc = jnp.zeros((ROW_TILE, LANES), F32)
        for g in range(N_GROUPS):
            weight = _split_dot(e[g] * inv, expand_ref[:, cols])
            acc = acc + weight * token_order(o_refs[g], g, g * n_slabs + c, cols)
        attn_ref[:, cols] = acc.astype(BF16)

    y_a = _dot(attn_ref[...], wo_ref[...])
    y_b = _dot(u_ref[...], wpw_ref[...])
    z = g_ref[:, 0:D_MODEL] * y_a + g_ref[:, D_MODEL:2 * D_MODEL] * y_b
    x1 = x_ref[...] + _dot(z.astype(BF16), wout_ref[...])

    ms = jnp.mean(x1 * x1, axis=-1, keepdims=True)
    h2 = (x1 * lax.rsqrt(ms + EPS) * n2_ref[...]).astype(BF16)
    for n in range(0, D_FF, FF_CHUNK):
        gt = _dot(h2, w1_ref[:, n:n + FF_CHUNK])
        up = _dot(h2, w1_ref[:, D_FF + n:D_FF + n + FF_CHUNK])
        act_ref[:, n:n + FF_CHUNK] = (gt * jax.nn.sigmoid(gt) * up).astype(BF16)
    out_ref[...] = x1 + _dot(act_ref[...], w2_ref[...])


def _mix_ffn_call(x3, o, lse, u_ln, gates, expand, wo, wpw, wout, n2, w1, w2):
    bsz, s, _ = x3.shape
    dilations = [d for _, d in DILATED_PATTERNS]
    return pl.pallas_call(
        _mix_ffn_kernel,
        grid=(bsz, s // ROW_TILE),
        in_specs=[_row_block(D_MODEL)]
        + [_group_block(d, GROUP_WIDTH) for d in dilations]
        + [_group_block(d, LANES) for d in dilations]
        + [_row_block(CONV_CH), _row_block(2 * D_MODEL),
           _resident(expand.shape), _resident(wo.shape), _resident(wpw.shape),
           _resident(wout.shape), _resident(n2.shape), _resident(w1.shape), _resident(w2.shape)],
        out_specs=_row_block(D_MODEL),
        out_shape=jax.ShapeDtypeStruct((bsz, s, D_MODEL), F32),
        scratch_shapes=[pltpu.VMEM((ROW_TILE, GROUP_WIDTH), BF16),
                        pltpu.VMEM((ROW_TILE, D_FF), BF16),
                        pltpu.VMEM((N_GROUPS * (GROUP_CHUNKS + 1), ROW_TILE, LANES), F32)],
        compiler_params=pltpu.CompilerParams(dimension_semantics=("arbitrary", "arbitrary"),
                                             vmem_limit_bytes=VMEM_LIMIT),
        name="mix_ffn",
    )(x3, *o, *lse, u_ln, gates, expand, wo, wpw, wout, n2, w1, w2)


def _rope_tables(positions):
    half = ROT_DIM // 2
    inv_freq = ROPE_THETA ** (-jnp.arange(0, ROT_DIM, 2, dtype=F32) / ROT_DIM)
    ang = positions.astype(F32)[..., None] * inv_freq
    cos, sin = jnp.cos(ang), jnp.sin(ang)
    shape = ang.shape[:-1]
    ones = jnp.ones(shape + (HEAD_DIM - ROT_DIM,), F32)
    z = lambda n: jnp.zeros(shape + (n,), F32)
    c = jnp.concatenate([cos, cos, ones], axis=-1)
    sa = jnp.concatenate([-sin, z(HEAD_DIM - half)], axis=-1)
    sb = jnp.concatenate([z(half), sin, z(HEAD_DIM - ROT_DIM)], axis=-1)
    rep = lambda t: jnp.tile(t, (1,) * len(shape) + (LANES // HEAD_DIM,))
    return rep(c), rep(sa), rep(sb)


def kernel(x, positions, norm1_w, w_in, b_gate, q_norm_w, k_norm_w, w_o_attn, conv_w, conv_b,
           conv_ln_w, conv_ln_b, w_pw_conv, w_out, norm2_w, w_ffn_in, w_ffn_out):
    d = x.shape[-1]
    depth = norm1_w.shape[0]
    cos, sa, sb = _rope_tables(positions)
    lane = jnp.arange(LANES)
    seg = (lane[:, None] // HEAD_DIM == lane[None, :] // HEAD_DIM).astype(BF16)
    head = jnp.arange(GROUP_WIDTH) // HEAD_DIM
    lse_lane = LSE_LANES * (head // 2) + HEAD_DIM * (1 - head % 2)
    expand = (lane[:, None] == lse_lane[None, :]).astype(BF16)
    pair = lambda w: jnp.tile(w.astype(F32), LANES // HEAD_DIM).reshape(1, LANES)

    for l in range(depth):
        qkv0, qkv1, qkv2, u, gates = _proj_call(
            x, norm1_w[l].reshape(1, d), w_in[l].astype(BF16), cos, sa, sb,
            pair(q_norm_w[l]), pair(k_norm_w[l]), seg, b_gate[l].reshape(1, 2 * d))
        outs, lses = zip(*[_attn_call(qkv) for qkv in (qkv0, qkv1, qkv2)])
        u_ln = _conv_call(u, conv_w[l], conv_b[l].reshape(1, CONV_CH),
                          conv_ln_w[l].reshape(1, CONV_CH), conv_ln_b[l].reshape(1, CONV_CH))
        x = _mix_ffn_call(
            x, outs, lses, u_ln, gates, expand,
            w_o_attn[l].astype(BF16), w_pw_conv[l].astype(BF16), w_out[l].astype(BF16),
            norm2_w[l].reshape(1, d), w_ffn_in[l].astype(BF16), w_ffn_out[l].astype(BF16))
    return x
```

```python
import functools

import jax
import jax.numpy as jnp
from jax import lax
from jax.experimental import pallas as pl
from jax.experimental.pallas import tpu as pltpu

D_MODEL = 1024
HEAD_DIM = 64
N_SLOT_HEADS = 8
DILATED_PATTERNS = ((128, 1), (512, 4), (2048, 16))
N_GROUPS = len(DILATED_PATTERNS)
GROUP_WIDTH = N_SLOT_HEADS * HEAD_DIM
QKV_WIDTH = N_GROUPS * GROUP_WIDTH
ROPE_THETA = 500000.0
ROT_DIM = HEAD_DIM // 4
CONV_CH = D_MODEL // 2
CONV_WIDTH = 31
D_FF = 2816
EPS = 1e-6
NEG_INF = -1e30

LANES = 128
SUBLANES = 8
SUB_Q = 64
KEY_BLOCK = 128
HALF_SPAN = 64
LSE_LANES = 16
LOG2E = 1.4426950408889634
LN2 = 0.6931471805599453
ROW_TILE = 512
EPI_ROWS = 128
GROUP_CHUNKS = GROUP_WIDTH // LANES
VMEM_LIMIT = 56 * 1024 * 1024

F32 = jnp.float32
BF16 = jnp.bfloat16


def _dot(a, b):
    return jnp.dot(a, b, preferred_element_type=F32)


def _split_dot(a_f32, b_bf16):
    hi = a_f32.astype(BF16)
    lo = (a_f32 - hi.astype(F32)).astype(BF16)
    return _dot(hi, b_bf16) + _dot(lo, b_bf16)


def _resident(shape):
    nd = len(shape)
    return pl.BlockSpec(shape, lambda *_: (0,) * nd, pipeline_mode=pl.Buffered(1))


def _row_block(width):
    return pl.BlockSpec((None, ROW_TILE, width), lambda b, i: (b, i, 0))


def _group_block(dilation, width, lead=()):
    n_lead = len(lead)
    return pl.BlockSpec(lead + (None, dilation, ROW_TILE // dilation, width),
                        lambda b, i: (0,) * n_lead + (b, 0, i, 0))


def _proj_kernel(x_ref, n1_ref, w_ref, cos_ref, sa_ref, sb_ref, qn_ref, kn_ref,
                 bg_ref, qkv0_ref, qkv1_ref, qkv2_ref, u_ref, g_ref, h_ref, perm_ref, acc_ref):
    xt = x_ref[...]
    ms = jnp.mean(xt * xt, axis=-1, keepdims=True)
    h_ref[...] = (xt * lax.rsqrt(ms + EPS) * n1_ref[...]).astype(BF16)

    n_proj = [0]

    def proj(c0):
        slot = n_proj[0] % acc_ref.shape[0]
        n_proj[0] += 1
        acc_ref[slot] = _dot(h_ref[...], w_ref[:, c0:c0 + GROUP_WIDTH])
        return acc_ref.at[slot]

    row_blocks = [slice(r, r + EPI_ROWS) for r in range(0, ROW_TILE, EPI_ROWS)]
    chunks = [slice(c * LANES, (c + 1) * LANES) for c in range(GROUP_CHUNKS)]

    same_head = (lax.broadcasted_iota(jnp.int32, (LANES, LANES), 0) // HEAD_DIM
                 == lax.broadcasted_iota(jnp.int32, (LANES, LANES), 1) // HEAD_DIM)
    seg = jnp.where(same_head, 1.0, 0.0).astype(BF16)

    def head_norm_rope(t, nw, scale, rows):
        ssq = _dot((t * t).astype(BF16), seg)
        y = t * lax.rsqrt(ssq * (1.0 / HEAD_DIM) + EPS) * nw
        r = (y * cos_ref[rows, :] + pltpu.roll(y, LANES - ROT_DIM // 2, 1) * sa_ref[rows, :]
             + pltpu.roll(y, ROT_DIM // 2, 1) * sb_ref[rows, :])
        return r * scale

    def emit(kind, group, acc, fn):
        out_ref = (qkv0_ref, qkv1_ref, qkv2_ref)[group]
        dilation = DILATED_PATTERNS[group][1]
        n = ROW_TILE // dilation
        for c, cols in enumerate(chunks):
            for rows in row_blocks:
                val = fn(acc[rows, cols], rows)
                if dilation == 1:
                    out_ref[kind, 0, rows, cols] = val.astype(BF16)
                else:
                    perm_ref[c, rows, :] = val
            for r in range(dilation if dilation > 1 else 0):
                out_ref[kind, r, :, cols] = perm_ref[c, pl.ds(r, n, stride=dilation), :].astype(BF16)

    for kind, nw_ref, scale in ((0, qn_ref, HEAD_DIM ** -0.5 * LOG2E), (1, kn_ref, 1.0)):
        nw = nw_ref[...]
        for j in range(N_GROUPS):
            acc = proj(kind * QKV_WIDTH + j * GROUP_WIDTH)
            emit(kind, j, acc, lambda t, rows: head_norm_rope(t, nw, scale, rows))

    for j in range(N_GROUPS):
        emit(2, j, proj(2 * QKV_WIDTH + j * GROUP_WIDTH), lambda t, rows: t)

    conv0 = 3 * QKV_WIDTH
    a = proj(conv0)
    b = proj(conv0 + CONV_CH)
    for rows in row_blocks:
        u_ref[rows, :] = a[rows, :] * jax.nn.sigmoid(b[rows, :])

    gate0 = conv0 + 2 * CONV_CH
    for j in range(2 * D_MODEL // GROUP_WIDTH):
        lo = j * GROUP_WIDTH
        acc = proj(gate0 + lo)
        for rows in row_blocks:
            g_ref[rows, lo:lo + GROUP_WIDTH] = jax.nn.sigmoid(
                acc[rows, :] + bg_ref[:, lo:lo + GROUP_WIDTH])


def _proj_call(x3, n1, w_in, cos, sa, sb, qn, kn, bg):
    bsz, s, _ = x3.shape
    in_width = w_in.shape[1]
    qkv_shapes = [jax.ShapeDtypeStruct((3, bsz, d, s // d, GROUP_WIDTH), BF16)
                  for _, d in DILATED_PATTERNS]
    return pl.pallas_call(
        _proj_kernel,
        grid=(bsz, s // ROW_TILE),
        in_specs=[_row_block(D_MODEL), _resident((1, D_MODEL)), _resident((D_MODEL, in_width)),
                  _row_block(LANES), _row_block(LANES), _row_block(LANES),
                  _resident((1, LANES)), _resident((1, LANES)),
                  _resident((1, 2 * D_MODEL))],
        out_specs=[_group_block(d, GROUP_WIDTH, lead=(3,)) for _, d in DILATED_PATTERNS]
        + [_row_block(CONV_CH), _row_block(2 * D_MODEL)],
        out_shape=qkv_shapes + [jax.ShapeDtypeStruct((bsz, s, CONV_CH), F32),
                                jax.ShapeDtypeStruct((bsz, s, 2 * D_MODEL), F32)],
        scratch_shapes=[pltpu.VMEM((ROW_TILE, D_MODEL), BF16),
                        pltpu.VMEM((GROUP_CHUNKS, ROW_TILE, LANES), F32),
                        pltpu.VMEM((2, ROW_TILE, GROUP_WIDTH), F32)],
        compiler_params=pltpu.CompilerParams(dimension_semantics=("arbitrary", "arbitrary"),
                                             vmem_limit_bytes=VMEM_LIMIT),
        name="proj",
    )(x3, n1, w_in, cos, sa, sb, qn, kn, bg)


def _attn_kernel(q_ref, k_ref, v_ref, o_ref, lse_ref, kpad_ref, vt_ref, bias_ref, p_ref, stat_ref):
    n_seq, seq_len, _ = q_ref.shape
    whole = seq_len == KEY_BLOCK
    kw = KEY_BLOCK if whole else 3 * SUB_Q
    n_key_blocks = seq_len // KEY_BLOCK
    n_blocks = n_seq * n_key_blocks
    vt_pad = 0 if whole else 1
    last_sub = seq_len // SUB_Q - 1
    subs = KEY_BLOCK // SUB_Q
    lane = lax.broadcasted_iota(jnp.int32, (1, LANES), 1)
    first_head = lane < HEAD_DIM
    block_diag = (lax.broadcasted_iota(jnp.int32, (LANES, LANES), 0) < HEAD_DIM) == first_head

    @pl.when(pl.program_id(0) == 0)
    def _init():
        row = lax.broadcasted_iota(jnp.int32, (kw, LANES), 0)
        lq = lax.broadcasted_iota(jnp.int32, (kw, LANES), 1) % SUB_Q
        as_bias = lambda valid: jnp.where(valid, 0.0, NEG_INF).astype(F32)
        if whole:
            for case in range(subs):
                bias_ref[case] = as_bias(jnp.abs(lq + case * SUB_Q - row) <= HALF_SPAN)
        else:
            band = jnp.abs(lq + SUB_Q - row) <= HALF_SPAN
            bias_ref[0] = as_bias(band & (row >= SUB_Q))
            bias_ref[1] = as_bias(band)
            bias_ref[2] = as_bias(band & (row < 2 * SUB_Q))
            zeros = jnp.zeros((SUB_Q, GROUP_WIDTH), BF16)
            zero_block = jnp.zeros((GROUP_WIDTH, KEY_BLOCK), BF16)
            for r in range(n_seq):
                kpad_ref[r, 0:SUB_Q, :] = zeros
                kpad_ref[r, SUB_Q + seq_len:2 * SUB_Q + seq_len, :] = zeros
                vt_ref[r, 0] = zero_block
                vt_ref[r, n_key_blocks + 1] = zero_block

    def split(j):
        return j // n_key_blocks, j % n_key_blocks

    def stage(j, c):
        r, b = split(j)
        rows = pl.ds(pl.multiple_of(b * KEY_BLOCK, KEY_BLOCK), KEY_BLOCK)
        if not whole:
            kpad_ref[r, pl.ds(pl.multiple_of(b * KEY_BLOCK + SUB_Q, SUB_Q), KEY_BLOCK), :] = k_ref[r, rows, :]
        for hp in range(GROUP_CHUNKS):
            cols = slice(hp * LANES, (hp + 1) * LANES)
            vt_ref[r, b + vt_pad, cols, :] = v_ref[r, rows, cols].astype(F32).T.astype(BF16)
        return c

    lax.fori_loop(0, n_blocks, stage, 0)

    def scores(j):
        r, i = split(j)
        for sub in range(subs):
            t = subs * i + sub
            q0 = pl.multiple_of(t * SUB_Q, SUB_Q)
            if whole:
                bias = bias_ref[sub]
            else:
                bias = bias_ref[jnp.where(t == 0, 0, jnp.where(t == last_sub, 2, 1))]
            for hp in range(GROUP_CHUNKS):
                unit = sub * GROUP_CHUNKS + hp
                cols = slice(hp * LANES, (hp + 1) * LANES)
                qb = q_ref[r, pl.ds(q0, SUB_Q), cols]
                zq = jnp.zeros_like(qb)
                qv = jnp.concatenate([jnp.where(first_head, qb, zq),
                                      jnp.where(first_head, zq, qb)], axis=0)
                kwin = k_ref[r, :, cols] if whole else kpad_ref[r, pl.ds(q0, kw), cols]
                s = lax.dot_general(kwin, qv, (((1,), (1,)), ((), ())),
                                    preferred_element_type=F32) + bias
                m = jnp.max(s, axis=0, keepdims=True)
                p = jnp.exp2(s - m)
                den = jnp.sum(p, axis=0, keepdims=True)
                p_ref[unit] = p.astype(BF16)
                stat_ref[unit, 0:1, :] = 1.0 / den
                stat_ref[unit, 1:2, :] = (m + jnp.log2(den)) * LN2

    def outputs(j):
        r, i = split(j)
        for sub in range(subs):
            q0 = pl.multiple_of((subs * i + sub) * SUB_Q, SUB_Q)
            lse_tile = jnp.zeros((SUB_Q, LANES), F32)
            for hp in range(GROUP_CHUNKS):
                unit = sub * GROUP_CHUNKS + hp
                cols = slice(hp * LANES, (hp + 1) * LANES)
                pb = p_ref[unit]
                if whole:
                    ot = _dot(vt_ref[r, 0, cols, :], pb)
                else:
                    first = i + sub
                    vt2 = jnp.concatenate([vt_ref[r, first, cols, :], vt_ref[r, first + 1, cols, :]], axis=1)
                    zp = jnp.zeros((SUB_Q, LANES), BF16)
                    ot = _dot(vt2, jnp.concatenate([zp, pb] if sub == 0 else [pb, zp], axis=0))
                tile = jnp.where(block_diag, ot * stat_ref[unit, 0:1, :], stat_ref[unit, 1:2, :]).T
                top, bot = tile[0:SUB_Q], tile[SUB_Q:2 * SUB_Q]
                o_ref[r, pl.ds(q0, SUB_Q), cols] = jnp.where(first_head, top, bot).astype(BF16)
                lse_tile = jnp.where((lane % HEAD_DIM) // LSE_LANES == hp,
                                     jnp.where(first_head, bot, top), lse_tile)
            lse_ref[r, pl.ds(q0, SUB_Q), :] = lse_tile

    scores(0)

    def pipelined(j, c):
        outputs(j - 1)
        scores(j)
        return c

    lax.fori_loop(1, n_blocks, pipelined, 0)
    outputs(n_blocks - 1)


def _attn_call(qkv):
    _, bsz, dilation, seq_len, _ = qkv.shape
    whole = seq_len == KEY_BLOCK
    kw = KEY_BLOCK if whole else 3 * SUB_Q
    units = (KEY_BLOCK // SUB_Q) * GROUP_CHUNKS
    in_spec = lambda kind: pl.BlockSpec((None, None, dilation, seq_len, GROUP_WIDTH),
                                        lambda b: (kind, b, 0, 0, 0))
    out_spec = lambda w: pl.BlockSpec((None, dilation, seq_len, w), lambda b: (b, 0, 0, 0))
    return pl.pallas_call(
        _attn_kernel,
        grid=(bsz,),
        in_specs=[in_spec(0), in_spec(1), in_spec(2)],
        out_specs=[out_spec(GROUP_WIDTH), out_spec(LANES)],
        out_shape=[jax.ShapeDtypeStruct((bsz, dilation, seq_len, GROUP_WIDTH), BF16),
                   jax.ShapeDtypeStruct((bsz, dilation, seq_len, LANES), F32)],
        scratch_shapes=[
            pltpu.VMEM((dilation, seq_len + 2 * SUB_Q, GROUP_WIDTH), BF16),
            pltpu.VMEM((dilation, seq_len // KEY_BLOCK + (0 if whole else 2), GROUP_WIDTH, KEY_BLOCK), BF16),
            pltpu.VMEM((2 if whole else 3, kw, LANES), F32),
            pltpu.VMEM((units, kw, LANES), BF16),
            pltpu.VMEM((units, SUBLANES, LANES), F32)],
        compiler_params=pltpu.CompilerParams(dimension_semantics=("arbitrary",),
                                             vmem_limit_bytes=VMEM_LIMIT),
        name=f"attn_d{dilation}",
    )(qkv, qkv, qkv)


CONV_PAD = 16
CONV_ROWS = 128
CONV_CHUNKS = CONV_CH // LANES


def _conv_kernel(u_ref, w_ref, b_ref, lnw_ref, lnb_ref, o_ref, pad_ref, conv_ref):
    s = u_ref.shape[1]
    zeros = jnp.zeros((CONV_PAD, LANES), F32)
    for c in range(CONV_CHUNKS):
        pad_ref[c, 0:CONV_PAD, :] = zeros
        pad_ref[c, CONV_PAD + s:2 * CONV_PAD + s, :] = zeros
        pad_ref[c, CONV_PAD:CONV_PAD + s, :] = u_ref[0, :, c * LANES:(c + 1) * LANES]
    first_tap = CONV_PAD - (CONV_WIDTH - 1) // 2
    half_rows = CONV_ROWS // 2

    def step(i, carry):
        r0 = i * CONV_ROWS
        for c in range(CONV_CHUNKS):
            cols = slice(c * LANES, (c + 1) * LANES)
            for parity in range(2):
                acc = jnp.zeros((half_rows, LANES), F32) + b_ref[:, cols]
                for t in range(CONV_WIDTH):
                    rows = pl.ds(r0 + parity + first_tap + t, half_rows, stride=2)
                    acc = acc + pad_ref[c, rows, :] * w_ref[t:t + 1, cols]
                conv_ref[c, pl.ds(parity, half_rows, stride=2), :] = acc
        ys = [conv_ref[c] for c in range(CONV_CHUNKS)]
        mu = sum(jnp.sum(y, axis=-1, keepdims=True) for y in ys) * (1.0 / CONV_CH)
        cen = [y - mu for y in ys]
        var = sum(jnp.sum(t * t, axis=-1, keepdims=True) for t in cen) * (1.0 / CONV_CH)
        inv = lax.rsqrt(var + EPS)
        for c in range(CONV_CHUNKS):
            cols = slice(c * LANES, (c + 1) * LANES)
            y = cen[c] * inv * lnw_ref[:, cols] + lnb_ref[:, cols]
            o_ref[0, pl.ds(pl.multiple_of(r0, CONV_ROWS), CONV_ROWS), cols] = (
                y * jax.nn.sigmoid(y)).astype(BF16)
        return carry

    lax.fori_loop(0, s // CONV_ROWS, step, 0)


def _conv_call(u, w, b, lnw, lnb):
    bsz, s, _ = u.shape
    return pl.pallas_call(
        _conv_kernel,
        grid=(bsz,),
        in_specs=[pl.BlockSpec((1, s, CONV_CH), lambda i: (i, 0, 0)),
                  _resident((CONV_WIDTH, CONV_CH)), _resident((1, CONV_CH)),
                  _resident((1, CONV_CH)), _resident((1, CONV_CH))],
        out_specs=pl.BlockSpec((1, s, CONV_CH), lambda i: (i, 0, 0)),
        out_shape=jax.ShapeDtypeStruct((bsz, s, CONV_CH), BF16),
        scratch_shapes=[pltpu.VMEM((CONV_CHUNKS, s + 2 * CONV_PAD, LANES), F32),
                        pltpu.VMEM((CONV_CHUNKS, CONV_ROWS, LANES), F32)],
        compiler_params=pltpu.CompilerParams(dimension_semantics=("arbitrary",),
                                             vmem_limit_bytes=VMEM_LIMIT),
        name="conv_ln",
    )(u, w, b, lnw, lnb)


FF_CHUNK = 256


def _mix_ffn_kernel(x_ref, o0_ref, o1_ref, o2_ref, l0_ref, l1_ref, l2_ref, u_ref, g_ref,
                    wo_ref, wpw_ref, wout_ref, n2_ref, w1_ref, w2_ref,
                    out_ref, attn_ref, act_ref, perm_ref):
    def token_order(ref, group, slab, cols):
        dilation = DILATED_PATTERNS[group][1]
        if dilation == 1:
            return ref[0, :, cols].astype(F32)
        n = ROW_TILE // dilation
        for r in range(dilation):
            perm_ref[slab, pl.ds(r, n, stride=dilation), :] = ref[r, :, cols].astype(F32)
        return perm_ref[slab]

    o_refs = (o0_ref, o1_ref, o2_ref)
    all_lanes = slice(0, LANES)
    n_slabs = GROUP_CHUNKS + 1
    lse = [token_order(ref, g, g * n_slabs + GROUP_CHUNKS, all_lanes)
           for g, ref in enumerate((l0_ref, l1_ref, l2_ref))]
    m = jnp.maximum(jnp.maximum(lse[0], lse[1]), lse[2])
    e = [jnp.exp(l - m) for l in lse]
    inv = 1.0 / (e[0] + e[1] + e[2])
    weights = [(e_g * inv).astype(BF16) for e_g in e]
    src_lane = lax.broadcasted_iota(jnp.int32, (LANES, LANES), 0)
    first_head = lax.broadcasted_iota(jnp.int32, (LANES, LANES), 1) < HEAD_DIM
    for c in range(GROUP_CHUNKS):
        cols = slice(c * LANES, (c + 1) * LANES)
        lane_b = LSE_LANES * c
        expand = jnp.where(src_lane == jnp.where(first_head, lane_b + HEAD_DIM, lane_b),
                           1.0, 0.0).astype(BF16)
        acc = jnp.zeros((ROW_TILE, LANES), F32)
        for g in range(N_GROUPS):
            acc = acc + _dot(weights[g], expand) * token_order(o_refs[g], g, g * n_slabs + c, cols)
        attn_ref[:, cols] = acc.astype(BF16)

    y_a = _dot(attn_ref[...], wo_ref[...])
    y_b = _dot(u_ref[...], wpw_ref[...])
    z = g_ref[:, 0:D_MODEL] * y_a + g_ref[:, D_MODEL:2 * D_MODEL] * y_b
    x1 = x_ref[...] + _dot(z.astype(BF16), wout_ref[...])

    ms = jnp.mean(x1 * x1, axis=-1, keepdims=True)
    h2 = (x1 * lax.rsqrt(ms + EPS) * n2_ref[...]).astype(BF16)
    for n in range(0, D_FF, FF_CHUNK):
        gt = _dot(h2, w1_ref[:, n:n + FF_CHUNK])
        up = _dot(h2, w1_ref[:, D_FF + n:D_FF + n + FF_CHUNK])
        act_ref[:, n:n + FF_CHUNK] = (gt * jax.nn.sigmoid(gt) * up).astype(BF16)
    out_ref[...] = x1 + _dot(act_ref[...], w2_ref[...])


def _mix_ffn_call(x3, o, lse, u_ln, gates, wo, wpw, wout, n2, w1, w2):
    bsz, s, _ = x3.shape
    dilations = [d for _, d in DILATED_PATTERNS]
    return pl.pallas_call(
        _mix_ffn_kernel,
        grid=(bsz, s // ROW_TILE),
        in_specs=[_row_block(D_MODEL)]
        + [_group_block(d, GROUP_WIDTH) for d in dilations]
        + [_group_block(d, LANES) for d in dilations]
        + [_row_block(CONV_CH), _row_block(2 * D_MODEL),
           _resident(wo.shape), _resident(wpw.shape),
           _resident(wout.shape), _resident(n2.shape), _resident(w1.shape), _resident(w2.shape)],
        out_specs=_row_block(D_MODEL),
        out_shape=jax.ShapeDtypeStruct((bsz, s, D_MODEL), F32),
        scratch_shapes=[pltpu.VMEM((ROW_TILE, GROUP_WIDTH), BF16),
                        pltpu.VMEM((ROW_TILE, D_FF), BF16),
                        pltpu.VMEM((N_GROUPS * (GROUP_CHUNKS + 1), ROW_TILE, LANES), F32)],
        compiler_params=pltpu.CompilerParams(dimension_semantics=("arbitrary", "arbitrary"),
                                             vmem_limit_bytes=VMEM_LIMIT),
        name="mix_ffn",
    )(x3, *o, *lse, u_ln, gates, wo, wpw, wout, n2, w1, w2)


def _rope_tables(positions):
    half = ROT_DIM // 2
    inv_freq = ROPE_THETA ** (-jnp.arange(0, ROT_DIM, 2, dtype=F32) / ROT_DIM)
    ang = positions.astype(F32)[..., None] * inv_freq
    cos, sin = jnp.cos(ang), jnp.sin(ang)
    shape = ang.shape[:-1]
    ones = jnp.ones(shape + (HEAD_DIM - ROT_DIM,), F32)
    z = lambda n: jnp.zeros(shape + (n,), F32)
    c = jnp.concatenate([cos, cos, ones], axis=-1)
    sa = jnp.concatenate([-sin, z(HEAD_DIM - half)], axis=-1)
    sb = jnp.concatenate([z(half), sin, z(HEAD_DIM - ROT_DIM)], axis=-1)
    rep = lambda t: jnp.tile(t, (1,) * len(shape) + (LANES // HEAD_DIM,))
    return rep(c), rep(sa), rep(sb)


def kernel(x, positions, norm1_w, w_in, b_gate, q_norm_w, k_norm_w, w_o_attn, conv_w, conv_b,
           conv_ln_w, conv_ln_b, w_pw_conv, w_out, norm2_w, w_ffn_in, w_ffn_out):
    d = x.shape[-1]
    depth = norm1_w.shape[0]
    cos, sa, sb = _rope_tables(positions)
    pair = lambda w: jnp.tile(w.astype(F32), LANES // HEAD_DIM).reshape(1, LANES)

    for l in range(depth):
        qkv0, qkv1, qkv2, u, gates = _proj_call(
            x, norm1_w[l].reshape(1, d), w_in[l].astype(BF16), cos, sa, sb,
            pair(q_norm_w[l]), pair(k_norm_w[l]), b_gate[l].reshape(1, 2 * d))
        outs, lses = zip(*[_attn_call(qkv) for qkv in (qkv0, qkv1, qkv2)])
        u_ln = _conv_call(u, conv_w[l], conv_b[l].reshape(1, CONV_CH),
                          conv_ln_w[l].reshape(1, CONV_CH), conv_ln_b[l].reshape(1, CONV_CH))
        x = _mix_ffn_call(
            x, outs, lses, u_ln, gates,
            w_o_attn[l].astype(BF16), w_pw_conv[l].astype(BF16), w_out[l].astype(BF16),
            norm2_w[l].reshape(1, d), w_ffn_in[l].astype(BF16), w_ffn_out[l].astype(BF16))
    return x
```

```python
import functools

import jax
import jax.numpy as jnp
from jax import lax
from jax.experimental import pallas as pl
from jax.experimental.pallas import tpu as pltpu

D_MODEL = 1024
HEAD_DIM = 64
N_SLOT_HEADS = 8
DILATED_PATTERNS = ((128, 1), (512, 4), (2048, 16))
N_GROUPS = len(DILATED_PATTERNS)
GROUP_WIDTH = N_SLOT_HEADS * HEAD_DIM
QKV_WIDTH = N_GROUPS * GROUP_WIDTH
ROPE_THETA = 500000.0
ROT_DIM = HEAD_DIM // 4
CONV_CH = D_MODEL // 2
CONV_WIDTH = 31
D_FF = 2816
EPS = 1e-6
NEG_INF = -1e30

LANES = 128
SUBLANES = 8
SUB_Q = 64
KEY_BLOCK = 128
HALF_SPAN = 64
LSE_LANES = 16
LOG2E = 1.4426950408889634
LN2 = 0.6931471805599453
ROW_TILE = 512
EPI_ROWS = 128
GROUP_CHUNKS = GROUP_WIDTH // LANES
VMEM_LIMIT = 56 * 1024 * 1024

F32 = jnp.float32
BF16 = jnp.bfloat16


def _dot(a, b):
    return jnp.dot(a, b, preferred_element_type=F32)


def _split_dot(a_f32, b_bf16):
    hi = a_f32.astype(BF16)
    lo = (a_f32 - hi.astype(F32)).astype(BF16)
    return _dot(hi, b_bf16) + _dot(lo, b_bf16)


def _resident(shape):
    nd = len(shape)
    return pl.BlockSpec(shape, lambda *_: (0,) * nd, pipeline_mode=pl.Buffered(1))


def _row_block(width):
    return pl.BlockSpec((None, ROW_TILE, width), lambda b, i: (b, i, 0))


def _group_block(dilation, width, lead=()):
    n_lead = len(lead)
    return pl.BlockSpec(lead + (None, dilation, ROW_TILE // dilation, width),
                        lambda b, i: (0,) * n_lead + (b, 0, i, 0))


def _proj_kernel(x_ref, n1_ref, w_ref, rot_ref, qn_ref, kn_ref,
                 bg_ref, qkv0_ref, qkv1_ref, qkv2_ref, u_ref, g_ref, h_ref, perm_ref, acc_ref,
                 rope_ref):
    xt = x_ref[...]
    ms = jnp.mean(xt * xt, axis=-1, keepdims=True)
    h_ref[...] = (xt * lax.rsqrt(ms + EPS) * n1_ref[...]).astype(BF16)

    row_blocks = [slice(r, r + EPI_ROWS) for r in range(0, ROW_TILE, EPI_ROWS)]
    chunks = [slice(c * LANES, (c + 1) * LANES) for c in range(GROUP_CHUNKS)]

    half_rot = ROT_DIM // 2
    head_lane = lax.broadcasted_iota(jnp.int32, (1, LANES), 1) % HEAD_DIM
    for rows in row_blocks:
        tab = rot_ref[rows, :]
        rope_ref[0, rows, :] = jnp.where(head_lane < half_rot, tab,
                                         jnp.where(head_lane < ROT_DIM, pltpu.roll(tab, half_rot, 1), 1.0))
        rope_ref[1, rows, :] = jnp.where(head_lane < half_rot, -pltpu.roll(tab, LANES - half_rot, 1), 0.0)
        rope_ref[2, rows, :] = jnp.where((head_lane >= half_rot) & (head_lane < ROT_DIM), tab, 0.0)

    n_proj = [0]

    def proj(c0):
        slot = n_proj[0] % acc_ref.shape[0]
        n_proj[0] += 1
        acc_ref[slot] = _dot(h_ref[...], w_ref[:, c0:c0 + GROUP_WIDTH])
        return acc_ref.at[slot]

    same_head = (lax.broadcasted_iota(jnp.int32, (LANES, LANES), 0) // HEAD_DIM
                 == lax.broadcasted_iota(jnp.int32, (LANES, LANES), 1) // HEAD_DIM)
    seg = jnp.where(same_head, 1.0, 0.0).astype(BF16)

    def head_norm_rope(t, nw, scale, rows):
        ssq = _dot((t * t).astype(BF16), seg)
        y = t * lax.rsqrt(ssq * (1.0 / HEAD_DIM) + EPS) * nw
        r = (y * rope_ref[0, rows, :] + pltpu.roll(y, LANES - half_rot, 1) * rope_ref[1, rows, :]
             + pltpu.roll(y, half_rot, 1) * rope_ref[2, rows, :])
        return r * scale

    def emit(kind, group, acc, fn):
        out_ref = (qkv0_ref, qkv1_ref, qkv2_ref)[group]
        dilation = DILATED_PATTERNS[group][1]
        n = ROW_TILE // dilation
        for c, cols in enumerate(chunks):
            for rows in row_blocks:
                val = fn(acc[rows, cols], rows)
                if dilation == 1:
                    out_ref[kind, 0, rows, cols] = val.astype(BF16)
                else:
                    perm_ref[c, rows, :] = val
            for r in range(dilation if dilation > 1 else 0):
                out_ref[kind, r, :, cols] = perm_ref[c, pl.ds(r, n, stride=dilation), :].astype(BF16)

    def qk_block(kind, j):
        nw_ref, scale = ((qn_ref, HEAD_DIM ** -0.5 * LOG2E), (kn_ref, 1.0))[kind]
        nw = nw_ref[...]
        acc = proj(kind * QKV_WIDTH + j * GROUP_WIDTH)
        emit(kind, j, acc, lambda t, rows: head_norm_rope(t, nw, scale, rows))

    def v_block(j):
        emit(2, j, proj(2 * QKV_WIDTH + j * GROUP_WIDTH), lambda t, rows: t)

    conv0 = 3 * QKV_WIDTH
    gate0 = conv0 + 2 * CONV_CH

    def gate_block(j):
        lo = j * GROUP_WIDTH
        acc = proj(gate0 + lo)
        for rows in row_blocks:
            g_ref[rows, lo:lo + GROUP_WIDTH] = jax.nn.sigmoid(
                acc[rows, :] + bg_ref[:, lo:lo + GROUP_WIDTH])

    for j in range(N_GROUPS):
        qk_block(0, j)
        v_block(j)
        qk_block(1, j)
        gate_block(j)
    a = proj(conv0)
    b = proj(conv0 + CONV_CH)
    for rows in row_blocks:
        u_ref[rows, :] = a[rows, :] * jax.nn.sigmoid(b[rows, :])
    gate_block(N_GROUPS)


def _proj_call(x3, n1, w_in, rot, qn, kn, bg):
    bsz, s, _ = x3.shape
    in_width = w_in.shape[1]
    qkv_shapes = [jax.ShapeDtypeStruct((3, bsz, d, s // d, GROUP_WIDTH), BF16)
                  for _, d in DILATED_PATTERNS]
    return pl.pallas_call(
        _proj_kernel,
        grid=(bsz, s // ROW_TILE),
        in_specs=[_row_block(D_MODEL), _resident((1, D_MODEL)), _resident((D_MODEL, in_width)),
                  _row_block(LANES),
                  _resident((1, LANES)), _resident((1, LANES)),
                  _resident((1, 2 * D_MODEL))],
        out_specs=[_group_block(d, GROUP_WIDTH, lead=(3,)) for _, d in DILATED_PATTERNS]
        + [_row_block(CONV_CH), _row_block(2 * D_MODEL)],
        out_shape=qkv_shapes + [jax.ShapeDtypeStruct((bsz, s, CONV_CH), F32),
                                jax.ShapeDtypeStruct((bsz, s, 2 * D_MODEL), F32)],
        scratch_shapes=[pltpu.VMEM((ROW_TILE, D_MODEL), BF16),
                        pltpu.VMEM((GROUP_CHUNKS, ROW_TILE, LANES), F32),
                        pltpu.VMEM((3, ROW_TILE, GROUP_WIDTH), F32),
                        pltpu.VMEM((3, ROW_TILE, LANES), F32)],
        compiler_params=pltpu.CompilerParams(dimension_semantics=("arbitrary", "arbitrary"),
                                             vmem_limit_bytes=VMEM_LIMIT),
        name="proj",
    )(x3, n1, w_in, rot, qn, kn, bg)


def _attn_kernel(q_ref, k_ref, v_ref, o_ref, lse_ref, kpad_ref, vt_ref, bias_ref, p_ref, stat_ref):
    n_seq, seq_len, _ = q_ref.shape
    whole = seq_len == KEY_BLOCK
    kw = KEY_BLOCK if whole else 3 * SUB_Q
    n_key_blocks = seq_len // KEY_BLOCK
    n_blocks = n_seq * n_key_blocks
    vt_pad = 0 if whole else 1
    last_sub = seq_len // SUB_Q - 1
    subs = KEY_BLOCK // SUB_Q
    lane = lax.broadcasted_iota(jnp.int32, (1, LANES), 1)
    first_head = lane < HEAD_DIM
    block_diag = (lax.broadcasted_iota(jnp.int32, (LANES, LANES), 0) < HEAD_DIM) == first_head

    @pl.when(pl.program_id(0) == 0)
    def _init():
        row = lax.broadcasted_iota(jnp.int32, (kw, LANES), 0)
        lq = lax.broadcasted_iota(jnp.int32, (kw, LANES), 1) % SUB_Q
        as_bias = lambda valid: jnp.where(valid, 0.0, NEG_INF).astype(F32)
        if whole:
            for case in range(subs):
                bias_ref[case] = as_bias(jnp.abs(lq + case * SUB_Q - row) <= HALF_SPAN)
        else:
            band = jnp.abs(lq + SUB_Q - row) <= HALF_SPAN
            bias_ref[0] = as_bias(band & (row >= SUB_Q))
            bias_ref[1] = as_bias(band)
            bias_ref[2] = as_bias(band & (row < 2 * SUB_Q))
            zeros = jnp.zeros((SUB_Q, GROUP_WIDTH), BF16)
            zero_block = jnp.zeros((GROUP_WIDTH, KEY_BLOCK), BF16)
            for r in range(n_seq):
                kpad_ref[r, 0:SUB_Q, :] = zeros
                kpad_ref[r, SUB_Q + seq_len:2 * SUB_Q + seq_len, :] = zeros
                vt_ref[r, 0] = zero_block
                vt_ref[r, n_key_blocks + 1] = zero_block

    def split(j):
        return j // n_key_blocks, j % n_key_blocks

    if not whole:
        for r in range(n_seq):
            kpad_ref[r, SUB_Q:SUB_Q + seq_len, :] = k_ref[r]

    def stage_vt(j):
        r, b = split(j)
        rows = pl.ds(pl.multiple_of(b * KEY_BLOCK, KEY_BLOCK), KEY_BLOCK)
        for hp in range(GROUP_CHUNKS):
            cols = slice(hp * LANES, (hp + 1) * LANES)
            vt_ref[r, b + vt_pad, cols, :] = v_ref[r, rows, cols].astype(F32).T.astype(BF16)

    def scores(j):
        r, i = split(j)
        for sub in range(subs):
            t = subs * i + sub
            q0 = pl.multiple_of(t * SUB_Q, SUB_Q)
            if whole:
                bias = bias_ref[sub]
            else:
                bias = bias_ref[jnp.where(t == 0, 0, jnp.where(t == last_sub, 2, 1))]
            for hp in range(GROUP_CHUNKS):
                unit = sub * GROUP_CHUNKS + hp
                cols = slice(hp * LANES, (hp + 1) * LANES)
                qb = q_ref[r, pl.ds(q0, SUB_Q), cols]
                zq = jnp.zeros_like(qb)
                qv = jnp.concatenate([jnp.where(first_head, qb, zq),
                                      jnp.where(first_head, zq, qb)], axis=0)
                kwin = k_ref[r, :, cols] if whole else kpad_ref[r, pl.ds(q0, kw), cols]
                s = lax.dot_general(kwin, qv, (((1,), (1,)), ((), ())),
                                    preferred_element_type=F32) + bias
                m = jnp.max(s, axis=0, keepdims=True)
                p = jnp.exp2(s - m)
                den = jnp.sum(p, axis=0, keepdims=True)
                p_ref[unit] = p.astype(BF16)
                stat_ref[unit, 0:1, :] = 1.0 / den
                stat_ref[unit, 1:2, :] = (m + jnp.log2(den)) * LN2

    def outputs(j):
        r, i = split(j)
        for sub in range(subs):
            q0 = pl.multiple_of((subs * i + sub) * SUB_Q, SUB_Q)
            lse_tile = jnp.zeros((SUB_Q, LANES), F32)
            for hp in range(GROUP_CHUNKS):
                unit = sub * GROUP_CHUNKS + hp
                cols = slice(hp * LANES, (hp + 1) * LANES)
                pb = p_ref[unit]
                if whole:
                    ot = _dot(vt_ref[r, 0, cols, :], pb)
                else:
                    first = i + sub
                    vt2 = jnp.concatenate([vt_ref[r, first, cols, :], vt_ref[r, first + 1, cols, :]], axis=1)
                    zp = jnp.zeros((SUB_Q, LANES), BF16)
                    ot = _dot(vt2, jnp.concatenate([zp, pb] if sub == 0 else [pb, zp], axis=0))
                tile = jnp.where(block_diag, ot * stat_ref[unit, 0:1, :], stat_ref[unit, 1:2, :]).T
                top, bot = tile[0:SUB_Q], tile[SUB_Q:2 * SUB_Q]
                o_ref[r, pl.ds(q0, SUB_Q), cols] = jnp.where(first_head, top, bot).astype(BF16)
                lse_tile = jnp.where((lane % HEAD_DIM) // LSE_LANES == hp,
                                     jnp.where(first_head, bot, top), lse_tile)
            lse_ref[r, pl.ds(q0, SUB_Q), :] = lse_tile

    stage_vt(0)
    stage_vt(1)
    scores(0)

    def pipelined(j, c):
        outputs(j - 1)
        scores(j)
        stage_vt(jnp.minimum(j + 1, n_blocks - 1))
        return c

    lax.fori_loop(1, n_blocks, pipelined, 0)
    outputs(n_blocks - 1)


def _attn_call(qkv):
    _, bsz, dilation, seq_len, _ = qkv.shape
    whole = seq_len == KEY_BLOCK
    kw = KEY_BLOCK if whole else 3 * SUB_Q
    units = (KEY_BLOCK // SUB_Q) * GROUP_CHUNKS
    in_spec = lambda kind: pl.BlockSpec((None, None, dilation, seq_len, GROUP_WIDTH),
                                        lambda b: (kind, b, 0, 0, 0))
    out_spec = lambda w: pl.BlockSpec((None, dilation, seq_len, w), lambda b: (b, 0, 0, 0))
    return pl.pallas_call(
        _attn_kernel,
        grid=(bsz,),
        in_specs=[in_spec(0), in_spec(1), in_spec(2)],
        out_specs=[out_spec(GROUP_WIDTH), out_spec(LANES)],
        out_shape=[jax.ShapeDtypeStruct((bsz, dilation, seq_len, GROUP_WIDTH), BF16),
                   jax.ShapeDtypeStruct((bsz, dilation, seq_len, LANES), F32)],
        scratch_shapes=[
            pltpu.VMEM((dilation, seq_len + 2 * SUB_Q, GROUP_WIDTH), BF16),
            pltpu.VMEM((dilation, seq_len // KEY_BLOCK + (0 if whole else 2), GROUP_WIDTH, KEY_BLOCK), BF16),
            pltpu.VMEM((2 if whole else 3, kw, LANES), F32),
            pltpu.VMEM((units, kw, LANES), BF16),
            pltpu.VMEM((units, SUBLANES, LANES), F32)],
        compiler_params=pltpu.CompilerParams(dimension_semantics=("arbitrary",),
                                             vmem_limit_bytes=VMEM_LIMIT),
        name=f"attn_d{dilation}",
    )(qkv, qkv, qkv)


CONV_PAD = 16
CONV_ROWS = 128
CONV_CHUNKS = CONV_CH // LANES


def _conv_kernel(u_ref, w_ref, b_ref, lnw_ref, lnb_ref, o_ref, pad_ref, conv_ref):
    s = u_ref.shape[1]
    zeros = jnp.zeros((CONV_PAD, LANES), F32)
    for c in range(CONV_CHUNKS):
        pad_ref[c, 0:CONV_PAD, :] = zeros
        pad_ref[c, CONV_PAD + s:2 * CONV_PAD + s, :] = zeros
        pad_ref[c, CONV_PAD:CONV_PAD + s, :] = u_ref[0, :, c * LANES:(c + 1) * LANES]
    first_tap = CONV_PAD - (CONV_WIDTH - 1) // 2
    half_rows = CONV_ROWS // 2

    def step(i, carry):
        r0 = i * CONV_ROWS
        for c in range(CONV_CHUNKS):
            cols = slice(c * LANES, (c + 1) * LANES)
            for parity in range(2):
                acc = jnp.zeros((half_rows, LANES), F32) + b_ref[:, cols]
                for t in range(CONV_WIDTH):
                    rows = pl.ds(r0 + parity + first_tap + t, half_rows, stride=2)
                    acc = acc + pad_ref[c, rows, :] * w_ref[t:t + 1, cols]
                conv_ref[c, pl.ds(parity, half_rows, stride=2), :] = acc
        ys = [conv_ref[c] for c in range(CONV_CHUNKS)]
        mu = sum(jnp.sum(y, axis=-1, keepdims=True) for y in ys) * (1.0 / CONV_CH)
        cen = [y - mu for y in ys]
        var = sum(jnp.sum(t * t, axis=-1, keepdims=True) for t in cen) * (1.0 / CONV_CH)
        inv = lax.rsqrt(var + EPS)
        for c in range(CONV_CHUNKS):
            cols = slice(c * LANES, (c + 1) * LANES)
            y = cen[c] * inv * lnw_ref[:, cols] + lnb_ref[:, cols]
            o_ref[0, pl.ds(pl.multiple_of(r0, CONV_ROWS), CONV_ROWS), cols] = (
                y * jax.nn.sigmoid(y)).astype(BF16)
        return carry

    lax.fori_loop(0, s // CONV_ROWS, step, 0)


def _conv_call(u, w, b, lnw, lnb):
    bsz, s, _ = u.shape
    return pl.pallas_call(
        _conv_kernel,
        grid=(bsz,),
        in_specs=[pl.BlockSpec((1, s, CONV_CH), lambda i: (i, 0, 0)),
                  _resident((CONV_WIDTH, CONV_CH)), _resident((1, CONV_CH)),
                  _resident((1, CONV_CH)), _resident((1, CONV_CH))],
        out_specs=pl.BlockSpec((1, s, CONV_CH), lambda i: (i, 0, 0)),
        out_shape=jax.ShapeDtypeStruct((bsz, s, CONV_CH), BF16),
        scratch_shapes=[pltpu.VMEM((CONV_CHUNKS, s + 2 * CONV_PAD, LANES), F32),
                        pltpu.VMEM((CONV_CHUNKS, CONV_ROWS, LANES), F32)],
        compiler_params=pltpu.CompilerParams(dimension_semantics=("arbitrary",),
                                             vmem_limit_bytes=VMEM_LIMIT),
        name="conv_ln",
    )(u, w, b, lnw, lnb)


FF_CHUNK = 256


def _mix_ffn_kernel(x_ref, o0_ref, o1_ref, o2_ref, l0_ref, l1_ref, l2_ref, u_ref, g_ref,
                    wo_ref, wpw_ref, wout_ref, n2_ref, w1_ref, w2_ref,
                    out_ref, attn_ref, act_ref, perm_ref):
    def token_order(ref, group, slab, cols):
        dilation = DILATED_PATTERNS[group][1]
        if dilation == 1:
            return ref[0, :, cols].astype(F32)
        n = ROW_TILE // dilation
        for r in range(dilation):
            perm_ref[slab, pl.ds(r, n, stride=dilation), :] = ref[r, :, cols].astype(F32)
        return perm_ref[slab]

    o_refs = (o0_ref, o1_ref, o2_ref)
    all_lanes = slice(0, LANES)
    n_slabs = GROUP_CHUNKS + 1
    lse = [token_order(ref, g, g * n_slabs + GROUP_CHUNKS, all_lanes)
           for g, ref in enumerate((l0_ref, l1_ref, l2_ref))]
    m = jnp.maximum(jnp.maximum(lse[0], lse[1]), lse[2])
    e = [jnp.exp(l - m) for l in lse]
    inv = 1.0 / (e[0] + e[1] + e[2])
    weights = [(e_g * inv).astype(BF16) for e_g in e]
    src_lane = lax.broadcasted_iota(jnp.int32, (LANES, LANES), 0)
    first_head = lax.broadcasted_iota(jnp.int32, (LANES, LANES), 1) < HEAD_DIM
    for c in range(GROUP_CHUNKS):
        cols = slice(c * LANES, (c + 1) * LANES)
        lane_b = LSE_LANES * c
        expand = jnp.where(src_lane == jnp.where(first_head, lane_b + HEAD_DIM, lane_b),
                           1.0, 0.0).astype(BF16)
        acc = jnp.zeros((ROW_TILE, LANES), F32)
        for g in range(N_GROUPS):
            acc = acc + _dot(weights[g], expand) * token_order(o_refs[g], g, g * n_slabs + c, cols)
        attn_ref[:, cols] = acc.astype(BF16)

    y_a = _dot(attn_ref[...], wo_ref[...])
    y_b = _dot(u_ref[...], wpw_ref[...])
    z = g_ref[:, 0:D_MODEL] * y_a + g_ref[:, D_MODEL:2 * D_MODEL] * y_b
    x1 = x_ref[...] + _dot(z.astype(BF16), wout_ref[...])

    ms = jnp.mean(x1 * x1, axis=-1, keepdims=True)
    h2 = (x1 * lax.rsqrt(ms + EPS) * n2_ref[...]).astype(BF16)
    for n in range(0, D_FF, FF_CHUNK):
        gt = _dot(h2, w1_ref[:, n:n + FF_CHUNK])
        up = _dot(h2, w1_ref[:, D_FF + n:D_FF + n + FF_CHUNK])
        act_ref[:, n:n + FF_CHUNK] = (gt * jax.nn.sigmoid(gt) * up).astype(BF16)
    out_ref[...] = x1 + _dot(act_ref[...], w2_ref[...])


def _mix_ffn_call(x3, o, lse, u_ln, gates, wo, wpw, wout, n2, w1, w2):
    bsz, s, _ = x3.shape
    dilations = [d for _, d in DILATED_PATTERNS]
    return pl.pallas_call(
        _mix_ffn_kernel,
        grid=(bsz, s // ROW_TILE),
        in_specs=[_row_block(D_MODEL)]
        + [_group_block(d, GROUP_WIDTH) for d in dilations]
        + [_group_block(d, LANES) for d in dilations]
        + [_row_block(CONV_CH), _row_block(2 * D_MODEL),
           _resident(wo.shape), _resident(wpw.shape),
           _resident(wout.shape), _resident(n2.shape), _resident(w1.shape), _resident(w2.shape)],
        out_specs=_row_block(D_MODEL),
        out_shape=jax.ShapeDtypeStruct((bsz, s, D_MODEL), F32),
        scratch_shapes=[pltpu.VMEM((ROW_TILE, GROUP_WIDTH), BF16),
                        pltpu.VMEM((ROW_TILE, D_FF), BF16),
                        pltpu.VMEM((N_GROUPS * (GROUP_CHUNKS + 1), ROW_TILE, LANES), F32)],
        compiler_params=pltpu.CompilerParams(dimension_semantics=("arbitrary", "arbitrary"),
                                             vmem_limit_bytes=VMEM_LIMIT),
        name="mix_ffn",
    )(x3, *o, *lse, u_ln, gates, wo, wpw, wout, n2, w1, w2)


def _rope_table(positions):
    half = ROT_DIM // 2
    inv_freq = ROPE_THETA ** (-jnp.arange(0, ROT_DIM, 2, dtype=F32) / ROT_DIM)
    head_lane = jnp.arange(LANES) % HEAD_DIM
    ang = positions.astype(F32)[..., None] * inv_freq[head_lane % half]
    return jnp.where(head_lane < half, jnp.cos(ang),
                     jnp.where(head_lane < ROT_DIM, jnp.sin(ang), 0.0))


def kernel(x, positions, norm1_w, w_in, b_gate, q_norm_w, k_norm_w, w_o_attn, conv_w, conv_b,
           conv_ln_w, conv_ln_b, w_pw_conv, w_out, norm2_w, w_ffn_in, w_ffn_out):
    d = x.shape[-1]
    depth = norm1_w.shape[0]
    rot = _rope_table(positions)
    pair = lambda w: jnp.tile(w.astype(F32), LANES // HEAD_DIM).reshape(1, LANES)

    for l in range(depth):
        qkv0, qkv1, qkv2, u, gates = _proj_call(
            x, norm1_w[l].reshape(1, d), w_in[l].astype(BF16), rot,
            pair(q_norm_w[l]), pair(k_norm_w[l]), b_gate[l].reshape(1, 2 * d))
        outs, lses = zip(*[_attn_call(qkv) for qkv in (qkv0, qkv1, qkv2)])
        u_ln = _conv_call(u, conv_w[l], conv_b[l].reshape(1, CONV_CH),
                          conv_ln_w[l].reshape(1, CONV_CH), conv_ln_b[l].reshape(1, CONV_CH))
        x = _mix_ffn_call(
            x, outs, lses, u_ln, gates,
            w_o_attn[l].astype(BF16), w_pw_conv[l].astype(BF16), w_out[l].astype(BF16),
            norm2_w[l].reshape(1, d), w_ffn_in[l].astype(BF16), w_ffn_out[l].astype(BF16))
    return x
```

```python
import functools

import jax
import jax.numpy as jnp
from jax import lax
from jax.experimental import pallas as pl
from jax.experimental.pallas import tpu as pltpu

D_MODEL = 1024
HEAD_DIM = 64
N_SLOT_HEADS = 8
DILATED_PATTERNS = ((128, 1), (512, 4), (2048, 16))
N_GROUPS = len(DILATED_PATTERNS)
GROUP_WIDTH = N_SLOT_HEADS * HEAD_DIM
QKV_WIDTH = N_GROUPS * GROUP_WIDTH
ROPE_THETA = 500000.0
ROT_DIM = HEAD_DIM // 4
CONV_CH = D_MODEL // 2
CONV_WIDTH = 31
D_FF = 2816
EPS = 1e-6
NEG_INF = -1e30

LANES = 128
SUBLANES = 8
SUB_Q = 64
KEY_BLOCK = 128
HALF_SPAN = 64
LSE_LANES = 16
LOG2E = 1.4426950408889634
LN2 = 0.6931471805599453
ROW_TILE = 512
EPI_ROWS = 128
GROUP_CHUNKS = GROUP_WIDTH // LANES
VMEM_LIMIT = 56 * 1024 * 1024

F32 = jnp.float32
BF16 = jnp.bfloat16


def _dot(a, b):
    return jnp.dot(a, b, preferred_element_type=F32)


def _split_dot(a_f32, b_bf16):
    hi = a_f32.astype(BF16)
    lo = (a_f32 - hi.astype(F32)).astype(BF16)
    return _dot(hi, b_bf16) + _dot(lo, b_bf16)


def _resident(shape):
    nd = len(shape)
    return pl.BlockSpec(shape, lambda *_: (0,) * nd, pipeline_mode=pl.Buffered(1))


def _row_block(width):
    return pl.BlockSpec((None, ROW_TILE, width), lambda b, i: (b, i, 0))


def _group_block(dilation, width, lead=()):
    n_lead = len(lead)
    return pl.BlockSpec(lead + (None, dilation, ROW_TILE // dilation, width),
                        lambda b, i: (0,) * n_lead + (b, 0, i, 0))


def _proj_kernel(x_ref, n1_ref, w_ref, rot_ref, qn_ref, kn_ref,
                 bg_ref, qkv0_ref, qkv1_ref, qkv2_ref, u_ref, g_ref, h_ref, perm_ref, acc_ref,
                 rope_ref):
    xt = x_ref[...]
    ms = jnp.mean(xt * xt, axis=-1, keepdims=True)
    h_ref[...] = (xt * lax.rsqrt(ms + EPS) * n1_ref[...]).astype(BF16)

    row_blocks = [slice(r, r + EPI_ROWS) for r in range(0, ROW_TILE, EPI_ROWS)]
    chunks = [slice(c * LANES, (c + 1) * LANES) for c in range(GROUP_CHUNKS)]

    half_rot = ROT_DIM // 2
    head_lane = lax.broadcasted_iota(jnp.int32, (1, LANES), 1) % HEAD_DIM
    for rows in row_blocks:
        tab = rot_ref[rows, :]
        rope_ref[0, rows, :] = jnp.where(head_lane < half_rot, tab,
                                         jnp.where(head_lane < ROT_DIM, pltpu.roll(tab, half_rot, 1), 1.0))
        rope_ref[1, rows, :] = jnp.where(head_lane < half_rot, -pltpu.roll(tab, LANES - half_rot, 1), 0.0)
        rope_ref[2, rows, :] = jnp.where((head_lane >= half_rot) & (head_lane < ROT_DIM), tab, 0.0)

    n_proj = [0]

    def proj(c0):
        slot = n_proj[0] % acc_ref.shape[0]
        n_proj[0] += 1
        acc_ref[slot] = _dot(h_ref[...], w_ref[:, c0:c0 + GROUP_WIDTH])
        return acc_ref.at[slot]

    same_head = (lax.broadcasted_iota(jnp.int32, (LANES, LANES), 0) // HEAD_DIM
                 == lax.broadcasted_iota(jnp.int32, (LANES, LANES), 1) // HEAD_DIM)
    seg = jnp.where(same_head, 1.0, 0.0).astype(BF16)

    def head_norm_rope(t, nw, scale, rows):
        ssq = _dot((t * t).astype(BF16), seg)
        y = t * lax.rsqrt(ssq * (1.0 / HEAD_DIM) + EPS) * nw
        r = (y * rope_ref[0, rows, :] + pltpu.roll(y, LANES - half_rot, 1) * rope_ref[1, rows, :]
             + pltpu.roll(y, half_rot, 1) * rope_ref[2, rows, :])
        return r * scale

    def emit(kind, group, acc, fn):
        out_ref = (qkv0_ref, qkv1_ref, qkv2_ref)[group]
        dilation = DILATED_PATTERNS[group][1]
        n = ROW_TILE // dilation
        for c, cols in enumerate(chunks):
            for rows in row_blocks:
                val = fn(acc[rows, cols], rows)
                if dilation == 1:
                    out_ref[kind, 0, rows, cols] = val.astype(BF16)
                else:
                    perm_ref[c, rows, :] = val
            for r in range(dilation if dilation > 1 else 0):
                out_ref[kind, r, :, cols] = perm_ref[c, pl.ds(r, n, stride=dilation), :].astype(BF16)

    def qk_block(kind, j):
        nw_ref, scale = ((qn_ref, HEAD_DIM ** -0.5 * LOG2E), (kn_ref, 1.0))[kind]
        nw = nw_ref[...]
        acc = proj(kind * QKV_WIDTH + j * GROUP_WIDTH)
        emit(kind, j, acc, lambda t, rows: head_norm_rope(t, nw, scale, rows))

    def v_block(j):
        emit(2, j, proj(2 * QKV_WIDTH + j * GROUP_WIDTH), lambda t, rows: t)

    conv0 = 3 * QKV_WIDTH
    gate0 = conv0 + 2 * CONV_CH

    def gate_block(j):
        lo = j * GROUP_WIDTH
        acc = proj(gate0 + lo)
        for rows in row_blocks:
            g_ref[rows, lo:lo + GROUP_WIDTH] = jax.nn.sigmoid(
                acc[rows, :] + bg_ref[:, lo:lo + GROUP_WIDTH])

    for j in range(N_GROUPS):
        qk_block(0, j)
        v_block(j)
        qk_block(1, j)
        gate_block(j)
    a = proj(conv0)
    b = proj(conv0 + CONV_CH)
    for rows in row_blocks:
        u_ref[rows, :] = a[rows, :] * jax.nn.sigmoid(b[rows, :])
    gate_block(N_GROUPS)


def _proj_call(x3, n1, w_in, rot, qn, kn, bg):
    bsz, s, _ = x3.shape
    in_width = w_in.shape[1]
    qkv_shapes = [jax.ShapeDtypeStruct((3, bsz, d, s // d, GROUP_WIDTH), BF16)
                  for _, d in DILATED_PATTERNS]
    return pl.pallas_call(
        _proj_kernel,
        grid=(bsz, s // ROW_TILE),
        in_specs=[_row_block(D_MODEL), _resident((1, D_MODEL)), _resident((D_MODEL, in_width)),
                  _row_block(LANES),
                  _resident((1, LANES)), _resident((1, LANES)),
                  _resident((1, 2 * D_MODEL))],
        out_specs=[_group_block(d, GROUP_WIDTH, lead=(3,)) for _, d in DILATED_PATTERNS]
        + [_row_block(CONV_CH), _row_block(2 * D_MODEL)],
        out_shape=qkv_shapes + [jax.ShapeDtypeStruct((bsz, s, CONV_CH), F32),
                                jax.ShapeDtypeStruct((bsz, s, 2 * D_MODEL), F32)],
        scratch_shapes=[pltpu.VMEM((ROW_TILE, D_MODEL), BF16),
                        pltpu.VMEM((GROUP_CHUNKS, ROW_TILE, LANES), F32),
                        pltpu.VMEM((3, ROW_TILE, GROUP_WIDTH), F32),
                        pltpu.VMEM((3, ROW_TILE, LANES), F32)],
        compiler_params=pltpu.CompilerParams(dimension_semantics=("arbitrary", "arbitrary"),
                                             vmem_limit_bytes=VMEM_LIMIT),
        name="proj",
    )(x3, n1, w_in, rot, qn, kn, bg)


def _attn_kernel(q_ref, k_ref, v_ref, o_ref, lse_ref, kpad_ref, vt_ref, bias_ref, s_ref, p_ref,
                 ot_ref, stat_ref):
    n_seq, seq_len, _ = q_ref.shape
    whole = seq_len == KEY_BLOCK
    kw = KEY_BLOCK if whole else 3 * SUB_Q
    n_key_blocks = seq_len // KEY_BLOCK
    n_blocks = n_seq * n_key_blocks
    vt_pad = 0 if whole else 1
    last_sub = seq_len // SUB_Q - 1
    subs = KEY_BLOCK // SUB_Q
    lane = lax.broadcasted_iota(jnp.int32, (1, LANES), 1)
    first_head = lane < HEAD_DIM
    block_diag = (lax.broadcasted_iota(jnp.int32, (LANES, LANES), 0) < HEAD_DIM) == first_head

    @pl.when(pl.program_id(0) == 0)
    def _init():
        row = lax.broadcasted_iota(jnp.int32, (kw, LANES), 0)
        lq = lax.broadcasted_iota(jnp.int32, (kw, LANES), 1) % SUB_Q
        as_bias = lambda valid: jnp.where(valid, 0.0, NEG_INF).astype(F32)
        if whole:
            for case in range(subs):
                bias_ref[case] = as_bias(jnp.abs(lq + case * SUB_Q - row) <= HALF_SPAN)
        else:
            band = jnp.abs(lq + SUB_Q - row) <= HALF_SPAN
            bias_ref[0] = as_bias(band & (row >= SUB_Q))
            bias_ref[1] = as_bias(band)
            bias_ref[2] = as_bias(band & (row < 2 * SUB_Q))
            zeros = jnp.zeros((SUB_Q, GROUP_WIDTH), BF16)
            zero_block = jnp.zeros((GROUP_WIDTH, KEY_BLOCK), BF16)
            for r in range(n_seq):
                kpad_ref[r, 0:SUB_Q, :] = zeros
                kpad_ref[r, SUB_Q + seq_len:2 * SUB_Q + seq_len, :] = zeros
                vt_ref[r, 0] = zero_block
                vt_ref[r, n_key_blocks + 1] = zero_block

    def split(j):
        return j // n_key_blocks, j % n_key_blocks

    if not whole:
        for r in range(n_seq):
            kpad_ref[r, SUB_Q:SUB_Q + seq_len, :] = k_ref[r]

    def stage_vt(j):
        r, b = split(j)
        rows = pl.ds(pl.multiple_of(b * KEY_BLOCK, KEY_BLOCK), KEY_BLOCK)
        for hp in range(GROUP_CHUNKS):
            cols = slice(hp * LANES, (hp + 1) * LANES)
            vt_ref[r, b + vt_pad, cols, :] = v_ref[r, rows, cols].astype(F32).T.astype(BF16)

    def stage1_qk(j):
        r, i = split(j)
        for sub in range(subs):
            q0 = pl.multiple_of((subs * i + sub) * SUB_Q, SUB_Q)
            for hp in range(GROUP_CHUNKS):
                cols = slice(hp * LANES, (hp + 1) * LANES)
                qb = q_ref[r, pl.ds(q0, SUB_Q), cols]
                zq = jnp.zeros_like(qb)
                qv = jnp.concatenate([jnp.where(first_head, qb, zq),
                                      jnp.where(first_head, zq, qb)], axis=0)
                kwin = k_ref[r, :, cols] if whole else kpad_ref[r, pl.ds(q0, kw), cols]
                s_ref[sub * GROUP_CHUNKS + hp] = lax.dot_general(
                    kwin, qv, (((1,), (1,)), ((), ())), preferred_element_type=F32)

    def stage2_softmax(j):
        _, i = split(j)
        for sub in range(subs):
            t = subs * i + sub
            if whole:
                bias = bias_ref[sub]
            else:
                bias = bias_ref[jnp.where(t == 0, 0, jnp.where(t == last_sub, 2, 1))]
            for hp in range(GROUP_CHUNKS):
                unit = sub * GROUP_CHUNKS + hp
                s = s_ref[unit] + bias
                m = jnp.max(s, axis=0, keepdims=True)
                p = jnp.exp2(s - m)
                den = jnp.sum(p, axis=0, keepdims=True)
                p_ref[unit] = p.astype(BF16)
                stat_ref[0, unit, 0:1, :] = 1.0 / den
                stat_ref[0, unit, 1:2, :] = (m + jnp.log2(den)) * LN2

    def stage3_pv(j):
        r, i = split(j)
        for sub in range(subs):
            for hp in range(GROUP_CHUNKS):
                unit = sub * GROUP_CHUNKS + hp
                cols = slice(hp * LANES, (hp + 1) * LANES)
                pb = p_ref[unit]
                if whole:
                    ot_ref[unit] = _dot(vt_ref[r, 0, cols, :], pb)
                else:
                    first = i + sub
                    vt2 = jnp.concatenate([vt_ref[r, first, cols, :], vt_ref[r, first + 1, cols, :]], axis=1)
                    zp = jnp.zeros((SUB_Q, LANES), BF16)
                    ot_ref[unit] = _dot(vt2, jnp.concatenate([zp, pb] if sub == 0 else [pb, zp], axis=0))
                stat_ref[1, unit] = stat_ref[0, unit]

    def stage4_finish(j):
        r, i = split(j)
        for sub in range(subs):
            q0 = pl.multiple_of((subs * i + sub) * SUB_Q, SUB_Q)
            lse_tile = jnp.zeros((SUB_Q, LANES), F32)
            for hp in range(GROUP_CHUNKS):
                unit = sub * GROUP_CHUNKS + hp
                cols = slice(hp * LANES, (hp + 1) * LANES)
                tile = jnp.where(block_diag, ot_ref[unit] * stat_ref[1, unit, 0:1, :],
                                 stat_ref[1, unit, 1:2, :]).T
                top, bot = tile[0:SUB_Q], tile[SUB_Q:2 * SUB_Q]
                o_ref[r, pl.ds(q0, SUB_Q), cols] = jnp.where(first_head, top, bot).astype(BF16)
                lse_tile = jnp.where((lane % HEAD_DIM) // LSE_LANES == hp,
                                     jnp.where(first_head, bot, top), lse_tile)
            lse_ref[r, pl.ds(q0, SUB_Q), :] = lse_tile

    def trip(j, stages=(4, 3, 2, 1)):
        if 4 in stages:
            stage4_finish(j - 2)
        if 3 in stages:
            stage3_pv(j - 1)
        if 2 in stages:
            stage2_softmax(j)
        if 1 in stages:
            stage1_qk(j + 1)
            stage_vt(j + 1)

    trip(-1, stages=(1,))
    trip(0, stages=(2, 1))
    trip(1, stages=(3, 2, 1))

    def steady(j, c):
        trip(j)
        return c

    lax.fori_loop(2, n_blocks - 1, steady, 0)
    trip(n_blocks - 1, stages=(4, 3, 2))
    trip(n_blocks, stages=(4, 3))
    trip(n_blocks + 1, stages=(4,))


def _attn_call(qkv):
    _, bsz, dilation, seq_len, _ = qkv.shape
    whole = seq_len == KEY_BLOCK
    kw = KEY_BLOCK if whole else 3 * SUB_Q
    units = (KEY_BLOCK // SUB_Q) * GROUP_CHUNKS
    in_spec = lambda kind: pl.BlockSpec((None, None, dilation, seq_len, GROUP_WIDTH),
                                        lambda b: (kind, b, 0, 0, 0))
    out_spec = lambda w: pl.BlockSpec((None, dilation, seq_len, w), lambda b: (b, 0, 0, 0))
    return pl.pallas_call(
        _attn_kernel,
        grid=(bsz,),
        in_specs=[in_spec(0), in_spec(1), in_spec(2)],
        out_specs=[out_spec(GROUP_WIDTH), out_spec(LANES)],
        out_shape=[jax.ShapeDtypeStruct((bsz, dilation, seq_len, GROUP_WIDTH), BF16),
                   jax.ShapeDtypeStruct((bsz, dilation, seq_len, LANES), F32)],
        scratch_shapes=[
            pltpu.VMEM((dilation, seq_len + 2 * SUB_Q, GROUP_WIDTH), BF16),
            pltpu.VMEM((dilation, seq_len // KEY_BLOCK + (0 if whole else 2), GROUP_WIDTH, KEY_BLOCK), BF16),
            pltpu.VMEM((2 if whole else 3, kw, LANES), F32),
            pltpu.VMEM((units, kw, LANES), F32),
            pltpu.VMEM((units, kw, LANES), BF16),
            pltpu.VMEM((units, LANES, LANES), F32),
            pltpu.VMEM((2, units, SUBLANES, LANES), F32)],
        compiler_params=pltpu.CompilerParams(dimension_semantics=("arbitrary",),
                                             vmem_limit_bytes=VMEM_LIMIT),
        name=f"attn_d{dilation}",
    )(qkv, qkv, qkv)


CONV_PAD = 16
CONV_ROWS = 128
CONV_CHUNKS = CONV_CH // LANES


def _conv_kernel(u_ref, w_ref, b_ref, lnw_ref, lnb_ref, o_ref, pad_ref, conv_ref):
    s = u_ref.shape[1]
    zeros = jnp.zeros((CONV_PAD, LANES), F32)
    for c in range(CONV_CHUNKS):
        pad_ref[c, 0:CONV_PAD, :] = zeros
        pad_ref[c, CONV_PAD + s:2 * CONV_PAD + s, :] = zeros
        pad_ref[c, CONV_PAD:CONV_PAD + s, :] = u_ref[0, :, c * LANES:(c + 1) * LANES]
    first_tap = CONV_PAD - (CONV_WIDTH - 1) // 2
    half_rows = CONV_ROWS // 2

    def step(i, carry):
        r0 = i * CONV_ROWS
        for c in range(CONV_CHUNKS):
            cols = slice(c * LANES, (c + 1) * LANES)
            for parity in range(2):
                acc = jnp.zeros((half_rows, LANES), F32) + b_ref[:, cols]
                for t in range(CONV_WIDTH):
                    rows = pl.ds(r0 + parity + first_tap + t, half_rows, stride=2)
                    acc = acc + pad_ref[c, rows, :] * w_ref[t:t + 1, cols]
                conv_ref[c, pl.ds(parity, half_rows, stride=2), :] = acc
        ys = [conv_ref[c] for c in range(CONV_CHUNKS)]
        mu = sum(jnp.sum(y, axis=-1, keepdims=True) for y in ys) * (1.0 / CONV_CH)
        cen = [y - mu for y in ys]
        var = sum(jnp.sum(t * t, axis=-1, keepdims=True) for t in cen) * (1.0 / CONV_CH)
        inv = lax.rsqrt(var + EPS)
        for c in range(CONV_CHUNKS):
            cols = slice(c * LANES, (c + 1) * LANES)
            y = cen[c] * inv * lnw_ref[:, cols] + lnb_ref[:, cols]
            o_ref[0, pl.ds(pl.multiple_of(r0, CONV_ROWS), CONV_ROWS), cols] = (
                y * jax.nn.sigmoid(y)).astype(BF16)
        return carry

    lax.fori_loop(0, s // CONV_ROWS, step, 0)


def _conv_call(u, w, b, lnw, lnb):
    bsz, s, _ = u.shape
    return pl.pallas_call(
        _conv_kernel,
        grid=(bsz,),
        in_specs=[pl.BlockSpec((1, s, CONV_CH), lambda i: (i, 0, 0)),
                  _resident((CONV_WIDTH, CONV_CH)), _resident((1, CONV_CH)),
                  _resident((1, CONV_CH)), _resident((1, CONV_CH))],
        out_specs=pl.BlockSpec((1, s, CONV_CH), lambda i: (i, 0, 0)),
        out_shape=jax.ShapeDtypeStruct((bsz, s, CONV_CH), BF16),
        scratch_shapes=[pltpu.VMEM((CONV_CHUNKS, s + 2 * CONV_PAD, LANES), F32),
                        pltpu.VMEM((CONV_CHUNKS, CONV_ROWS, LANES), F32)],
        compiler_params=pltpu.CompilerParams(dimension_semantics=("arbitrary",),
                                             vmem_limit_bytes=VMEM_LIMIT),
        name="conv_ln",
    )(u, w, b, lnw, lnb)


FF_CHUNK = 256


def _mix_ffn_kernel(x_ref, o0_ref, o1_ref, o2_ref, l0_ref, l1_ref, l2_ref, u_ref, g_ref,
                    wo_ref, wpw_ref, wout_ref, n2_ref, w1_ref, w2_ref,
                    out_ref, attn_ref, act_ref, perm_ref):
    def token_order(ref, group, slab, cols):
        dilation = DILATED_PATTERNS[group][1]
        if dilation == 1:
            return ref[0, :, cols].astype(F32)
        n = ROW_TILE // dilation
        for r in range(dilation):
            perm_ref[slab, pl.ds(r, n, stride=dilation), :] = ref[r, :, cols].astype(F32)
        return perm_ref[slab]

    o_refs = (o0_ref, o1_ref, o2_ref)
    all_lanes = slice(0, LANES)
    n_slabs = GROUP_CHUNKS + 1
    lse = [token_order(ref, g, g * n_slabs + GROUP_CHUNKS, all_lanes)
           for g, ref in enumerate((l0_ref, l1_ref, l2_ref))]
    m = jnp.maximum(jnp.maximum(lse[0], lse[1]), lse[2])
    e = [jnp.exp(l - m) for l in lse]
    inv = 1.0 / (e[0] + e[1] + e[2])
    weights = [(e_g * inv).astype(BF16) for e_g in e]
    src_lane = lax.broadcasted_iota(jnp.int32, (LANES, LANES), 0)
    first_head = lax.broadcasted_iota(jnp.int32, (LANES, LANES), 1) < HEAD_DIM
    for c in range(GROUP_CHUNKS):
        cols = slice(c * LANES, (c + 1) * LANES)
        lane_b = LSE_LANES * c
        expand = jnp.where(src_lane == jnp.where(first_head, lane_b + HEAD_DIM, lane_b),
                           1.0, 0.0).astype(BF16)
        acc = jnp.zeros((ROW_TILE, LANES), F32)
        for g in range(N_GROUPS):
            acc = acc + _dot(weights[g], expand) * token_order(o_refs[g], g, g * n_slabs + c, cols)
        attn_ref[:, cols] = acc.astype(BF16)

    y_a = _dot(attn_ref[...], wo_ref[...])
    y_b = _dot(u_ref[...], wpw_ref[...])
    z = g_ref[:, 0:D_MODEL] * y_a + g_ref[:, D_MODEL:2 * D_MODEL] * y_b
    x1 = x_ref[...] + _dot(z.astype(BF16), wout_ref[...])

    ms = jnp.mean(x1 * x1, axis=-1, keepdims=True)
    h2 = (x1 * lax.rsqrt(ms + EPS) * n2_ref[...]).astype(BF16)
    for n in range(0, D_FF, FF_CHUNK):
        gt = _dot(h2, w1_ref[:, n:n + FF_CHUNK])
        up = _dot(h2, w1_ref[:, D_FF + n:D_FF + n + FF_CHUNK])
        act_ref[:, n:n + FF_CHUNK] = (gt * jax.nn.sigmoid(gt) * up).astype(BF16)
    out_ref[...] = x1 + _dot(act_ref[...], w2_ref[...])


def _mix_ffn_call(x3, o, lse, u_ln, gates, wo, wpw, wout, n2, w1, w2):
    bsz, s, _ = x3.shape
    dilations = [d for _, d in DILATED_PATTERNS]
    return pl.pallas_call(
        _mix_ffn_kernel,
        grid=(bsz, s // ROW_TILE),
        in_specs=[_row_block(D_MODEL)]
        + [_group_block(d, GROUP_WIDTH) for d in dilations]
        + [_group_block(d, LANES) for d in dilations]
        + [_row_block(CONV_CH), _row_block(2 * D_MODEL),
           _resident(wo.shape), _resident(wpw.shape),
           _resident(wout.shape), _resident(n2.shape), _resident(w1.shape), _resident(w2.shape)],
        out_specs=_row_block(D_MODEL),
        out_shape=jax.ShapeDtypeStruct((bsz, s, D_MODEL), F32),
        scratch_shapes=[pltpu.VMEM((ROW_TILE, GROUP_WIDTH), BF16),
                        pltpu.VMEM((ROW_TILE, D_FF), BF16),
                        pltpu.VMEM((N_GROUPS * (GROUP_CHUNKS + 1), ROW_TILE, LANES), F32)],
        compiler_params=pltpu.CompilerParams(dimension_semantics=("arbitrary", "arbitrary"),
                                             vmem_limit_bytes=VMEM_LIMIT),
        name="mix_ffn",
    )(x3, *o, *lse, u_ln, gates, wo, wpw, wout, n2, w1, w2)


def _rope_table(positions):
    inv_freq = ROPE_THETA ** (-jnp.arange(0, ROT_DIM, 2, dtype=F32) / ROT_DIM)
    ang = positions.astype(F32)[..., None] * inv_freq
    rest = jnp.zeros(ang.shape[:-1] + (HEAD_DIM - ROT_DIM,), F32)
    return jnp.concatenate([jnp.cos(ang), jnp.sin(ang), rest] * (LANES // HEAD_DIM), axis=-1)


def kernel(x, positions, norm1_w, w_in, b_gate, q_norm_w, k_norm_w, w_o_attn, conv_w, conv_b,
           conv_ln_w, conv_ln_b, w_pw_conv, w_out, norm2_w, w_ffn_in, w_ffn_out):
    d = x.shape[-1]
    depth = norm1_w.shape[0]
    rot = _rope_table(positions)
    pair = lambda w: jnp.tile(w.astype(F32), LANES // HEAD_DIM).reshape(1, LANES)

    for l in range(depth):
        qkv0, qkv1, qkv2, u, gates = _proj_call(
            x, norm1_w[l].reshape(1, d), w_in[l].astype(BF16), rot,
            pair(q_norm_w[l]), pair(k_norm_w[l]), b_gate[l].reshape(1, 2 * d))
        outs, lses = zip(*[_attn_call(qkv) for qkv in (qkv0, qkv1, qkv2)])
        u_ln = _conv_call(u, conv_w[l], conv_b[l].reshape(1, CONV_CH),
                          conv_ln_w[l].reshape(1, CONV_CH), conv_ln_b[l].reshape(1, CONV_CH))
        x = _mix_ffn_call(
            x, outs, lses, u_ln, gates,
            w_o_attn[l].astype(BF16), w_pw_conv[l].astype(BF16), w_out[l].astype(BF16),
            norm2_w[l].reshape(1, d), w_ffn_in[l].astype(BF16), w_ffn_out[l].astype(BF16))
    return x
```

```python
import functools

import jax
import jax.numpy as jnp
from jax import lax
from jax.experimental import pallas as pl
from jax.experimental.pallas import tpu as pltpu

D_MODEL = 1024
HEAD_DIM = 64
N_SLOT_HEADS = 8
DILATED_PATTERNS = ((128, 1), (512, 4), (2048, 16))
N_GROUPS = len(DILATED_PATTERNS)
GROUP_WIDTH = N_SLOT_HEADS * HEAD_DIM
QKV_WIDTH = N_GROUPS * GROUP_WIDTH
ROPE_THETA = 500000.0
ROT_DIM = HEAD_DIM // 4
CONV_CH = D_MODEL // 2
CONV_WIDTH = 31
D_FF = 2816
EPS = 1e-6
NEG_INF = -1e30

LANES = 128
SUBLANES = 8
SUB_Q = 64
KEY_BLOCK = 128
HALF_SPAN = 64
LSE_LANES = 16
LOG2E = 1.4426950408889634
LN2 = 0.6931471805599453
ROW_TILE = 512
EPI_ROWS = 128
GROUP_CHUNKS = GROUP_WIDTH // LANES
VMEM_LIMIT = 56 * 1024 * 1024

F32 = jnp.float32
BF16 = jnp.bfloat16


def _dot(a, b):
    return jnp.dot(a, b, preferred_element_type=F32)


def _split_dot(a_f32, b_bf16):
    hi = a_f32.astype(BF16)
    lo = (a_f32 - hi.astype(F32)).astype(BF16)
    return _dot(hi, b_bf16) + _dot(lo, b_bf16)


def _resident(shape):
    nd = len(shape)
    return pl.BlockSpec(shape, lambda *_: (0,) * nd, pipeline_mode=pl.Buffered(1))


def _row_block(width):
    return pl.BlockSpec((None, ROW_TILE, width), lambda b, i: (b, i, 0))


def _group_block(dilation, width, lead=()):
    n_lead = len(lead)
    return pl.BlockSpec(lead + (None, dilation, ROW_TILE // dilation, width),
                        lambda b, i: (0,) * n_lead + (b, 0, i, 0))


def _proj_kernel(x_ref, n1_ref, w_ref, rot_ref, qn_ref, kn_ref,
                 bg_ref, qkv0_ref, qkv1_ref, qkv2_ref, u_ref, g_ref, h_ref, perm_ref, acc_ref,
                 rope_ref):
    xt = x_ref[...]
    ms = jnp.mean(xt * xt, axis=-1, keepdims=True)
    h_ref[...] = (xt * lax.rsqrt(ms + EPS) * n1_ref[...]).astype(BF16)

    row_blocks = [slice(r, r + EPI_ROWS) for r in range(0, ROW_TILE, EPI_ROWS)]
    chunks = [slice(c * LANES, (c + 1) * LANES) for c in range(GROUP_CHUNKS)]

    half_rot = ROT_DIM // 2
    head_lane = lax.broadcasted_iota(jnp.int32, (1, LANES), 1) % HEAD_DIM
    for rows in row_blocks:
        tab = rot_ref[:, rows].T
        rope_ref[0, rows, :] = jnp.where(head_lane < half_rot, tab,
                                         jnp.where(head_lane < ROT_DIM, pltpu.roll(tab, half_rot, 1), 1.0))
        rope_ref[1, rows, :] = jnp.where(head_lane < half_rot, -pltpu.roll(tab, LANES - half_rot, 1), 0.0)
        rope_ref[2, rows, :] = jnp.where((head_lane >= half_rot) & (head_lane < ROT_DIM), tab, 0.0)

    n_proj = [0]

    def proj(c0):
        slot = n_proj[0] % acc_ref.shape[0]
        n_proj[0] += 1
        acc_ref[slot] = _dot(h_ref[...], w_ref[:, c0:c0 + GROUP_WIDTH])
        return acc_ref.at[slot]

    same_head = (lax.broadcasted_iota(jnp.int32, (LANES, LANES), 0) // HEAD_DIM
                 == lax.broadcasted_iota(jnp.int32, (LANES, LANES), 1) // HEAD_DIM)
    seg = jnp.where(same_head, 1.0, 0.0).astype(BF16)

    def head_norm_rope(t, nw, scale, rows):
        ssq = _dot((t * t).astype(BF16), seg)
        y = t * lax.rsqrt(ssq * (1.0 / HEAD_DIM) + EPS) * nw
        r = (y * rope_ref[0, rows, :] + pltpu.roll(y, LANES - half_rot, 1) * rope_ref[1, rows, :]
             + pltpu.roll(y, half_rot, 1) * rope_ref[2, rows, :])
        return r * scale

    def emit(kind, group, acc, fn):
        out_ref = (qkv0_ref, qkv1_ref, qkv2_ref)[group]
        dilation = DILATED_PATTERNS[group][1]
        n = ROW_TILE // dilation
        for c, cols in enumerate(chunks):
            for rows in row_blocks:
                val = fn(acc[rows, cols], rows)
                if dilation == 1:
                    out_ref[kind, 0, rows, cols] = val.astype(BF16)
                else:
                    perm_ref[c, rows, :] = val
            for r in range(dilation if dilation > 1 else 0):
                out_ref[kind, r, :, cols] = perm_ref[c, pl.ds(r, n, stride=dilation), :].astype(BF16)

    def qk_block(kind, j):
        nw_ref, scale = ((qn_ref, HEAD_DIM ** -0.5 * LOG2E), (kn_ref, 1.0))[kind]
        nw = nw_ref[...]
        acc = proj(kind * QKV_WIDTH + j * GROUP_WIDTH)
        emit(kind, j, acc, lambda t, rows: head_norm_rope(t, nw, scale, rows))

    def v_block(j):
        emit(2, j, proj(2 * QKV_WIDTH + j * GROUP_WIDTH), lambda t, rows: t)

    conv0 = 3 * QKV_WIDTH
    gate0 = conv0 + 2 * CONV_CH

    def gate_block(j):
        lo = j * GROUP_WIDTH
        acc = proj(gate0 + lo)
        for rows in row_blocks:
            g_ref[rows, lo:lo + GROUP_WIDTH] = jax.nn.sigmoid(
                acc[rows, :] + bg_ref[:, lo:lo + GROUP_WIDTH])

    for j in range(N_GROUPS):
        qk_block(0, j)
        v_block(j)
        qk_block(1, j)
        gate_block(j)
    a = proj(conv0)
    b = proj(conv0 + CONV_CH)
    for rows in row_blocks:
        u_ref[rows, :] = a[rows, :] * jax.nn.sigmoid(b[rows, :])
    gate_block(N_GROUPS)


def _proj_call(x3, n1, w_in, rot, qn, kn, bg):
    bsz, s, _ = x3.shape
    in_width = w_in.shape[1]
    qkv_shapes = [jax.ShapeDtypeStruct((3, bsz, d, s // d, GROUP_WIDTH), BF16)
                  for _, d in DILATED_PATTERNS]
    return pl.pallas_call(
        _proj_kernel,
        grid=(bsz, s // ROW_TILE),
        in_specs=[_row_block(D_MODEL), _resident((1, D_MODEL)), _resident((D_MODEL, in_width)),
                  pl.BlockSpec((None, LANES, ROW_TILE), lambda b, i: (b, 0, i)),
                  _resident((1, LANES)), _resident((1, LANES)),
                  _resident((1, 2 * D_MODEL))],
        out_specs=[_group_block(d, GROUP_WIDTH, lead=(3,)) for _, d in DILATED_PATTERNS]
        + [_row_block(CONV_CH), _row_block(2 * D_MODEL)],
        out_shape=qkv_shapes + [jax.ShapeDtypeStruct((bsz, s, CONV_CH), F32),
                                jax.ShapeDtypeStruct((bsz, s, 2 * D_MODEL), F32)],
        scratch_shapes=[pltpu.VMEM((ROW_TILE, D_MODEL), BF16),
                        pltpu.VMEM((GROUP_CHUNKS, ROW_TILE, LANES), F32),
                        pltpu.VMEM((3, ROW_TILE, GROUP_WIDTH), F32),
                        pltpu.VMEM((3, ROW_TILE, LANES), F32)],
        compiler_params=pltpu.CompilerParams(dimension_semantics=("arbitrary", "arbitrary"),
                                             vmem_limit_bytes=VMEM_LIMIT),
        name="proj",
    )(x3, n1, w_in, rot, qn, kn, bg)


def _attn_kernel(q_ref, k_ref, v_ref, o_ref, lse_ref, kpad_ref, vt_ref, bias_ref, s_ref, p_ref,
                 ot_ref, stat_ref):
    n_seq, seq_len, _ = q_ref.shape
    whole = seq_len == KEY_BLOCK
    kw = KEY_BLOCK if whole else 3 * SUB_Q
    n_key_blocks = seq_len // KEY_BLOCK
    n_blocks = n_seq * n_key_blocks
    vt_pad = 0 if whole else 1
    last_sub = seq_len // SUB_Q - 1
    subs = KEY_BLOCK // SUB_Q
    lane = lax.broadcasted_iota(jnp.int32, (1, LANES), 1)
    first_head = lane < HEAD_DIM
    block_diag = (lax.broadcasted_iota(jnp.int32, (LANES, LANES), 0) < HEAD_DIM) == first_head

    @pl.when(pl.program_id(0) == 0)
    def _init():
        row = lax.broadcasted_iota(jnp.int32, (kw, LANES), 0)
        lq = lax.broadcasted_iota(jnp.int32, (kw, LANES), 1) % SUB_Q
        as_bias = lambda valid: jnp.where(valid, 0.0, NEG_INF).astype(F32)
        if whole:
            for case in range(subs):
                bias_ref[case] = as_bias(jnp.abs(lq + case * SUB_Q - row) <= HALF_SPAN)
        else:
            band = jnp.abs(lq + SUB_Q - row) <= HALF_SPAN
            bias_ref[0] = as_bias(band & (row >= SUB_Q))
            bias_ref[1] = as_bias(band)
            bias_ref[2] = as_bias(band & (row < 2 * SUB_Q))
            zeros = jnp.zeros((SUB_Q, GROUP_WIDTH), BF16)
            zero_block = jnp.zeros((GROUP_WIDTH, KEY_BLOCK), BF16)
            for r in range(n_seq):
                kpad_ref[r, 0:SUB_Q, :] = zeros
                kpad_ref[r, SUB_Q + seq_len:2 * SUB_Q + seq_len, :] = zeros
                vt_ref[r, 0] = zero_block
                vt_ref[r, n_key_blocks + 1] = zero_block

    def split(j):
        return j // n_key_blocks, j % n_key_blocks

    if not whole:
        for r in range(n_seq):
            kpad_ref[r, SUB_Q:SUB_Q + seq_len, :] = k_ref[r]

    def stage_vt(j):
        r, b = split(j)
        rows = pl.ds(pl.multiple_of(b * KEY_BLOCK, KEY_BLOCK), KEY_BLOCK)
        for hp in range(GROUP_CHUNKS):
            cols = slice(hp * LANES, (hp + 1) * LANES)
            vt_ref[r, b + vt_pad, cols, :] = v_ref[r, rows, cols].astype(F32).T.astype(BF16)

    def stage1_qk(j):
        r, i = split(j)
        for sub in range(subs):
            q0 = pl.multiple_of((subs * i + sub) * SUB_Q, SUB_Q)
            for hp in range(GROUP_CHUNKS):
                cols = slice(hp * LANES, (hp + 1) * LANES)
                qb = q_ref[r, pl.ds(q0, SUB_Q), cols]
                zq = jnp.zeros_like(qb)
                qv = jnp.concatenate([jnp.where(first_head, qb, zq),
                                      jnp.where(first_head, zq, qb)], axis=0)
                kwin = k_ref[r, :, cols] if whole else kpad_ref[r, pl.ds(q0, kw), cols]
                s_ref[sub * GROUP_CHUNKS + hp] = lax.dot_general(
                    kwin, qv, (((1,), (1,)), ((), ())), preferred_element_type=F32)

    def stage2_softmax(j):
        _, i = split(j)
        for sub in range(subs):
            t = subs * i + sub
            if whole:
                bias = bias_ref[sub]
            else:
                bias = bias_ref[jnp.where(t == 0, 0, jnp.where(t == last_sub, 2, 1))]
            for hp in range(GROUP_CHUNKS):
                unit = sub * GROUP_CHUNKS + hp
                s = s_ref[unit] + bias
                m = jnp.max(s, axis=0, keepdims=True)
                p = jnp.exp2(s - m)
                den = jnp.sum(p, axis=0, keepdims=True)
                p_ref[unit] = p.astype(BF16)
                stat_ref[0, unit, 0:1, :] = 1.0 / den
                stat_ref[0, unit, 1:2, :] = (m + jnp.log2(den)) * LN2

    def stage3_pv(j):
        r, i = split(j)
        for sub in range(subs):
            for hp in range(GROUP_CHUNKS):
                unit = sub * GROUP_CHUNKS + hp
                cols = slice(hp * LANES, (hp + 1) * LANES)
                pb = p_ref[unit]
                if whole:
                    ot_ref[unit] = _dot(vt_ref[r, 0, cols, :], pb)
                else:
                    first = i + sub
                    vt2 = jnp.concatenate([vt_ref[r, first, cols, :], vt_ref[r, first + 1, cols, :]], axis=1)
                    zp = jnp.zeros((SUB_Q, LANES), BF16)
                    ot_ref[unit] = _dot(vt2, jnp.concatenate([zp, pb] if sub == 0 else [pb, zp], axis=0))
                stat_ref[1, unit] = stat_ref[0, unit]

    def stage4_finish(j):
        r, i = split(j)
        for sub in range(subs):
            q0 = pl.multiple_of((subs * i + sub) * SUB_Q, SUB_Q)
            lse_tile = jnp.zeros((SUB_Q, LANES), F32)
            for hp in range(GROUP_CHUNKS):
                unit = sub * GROUP_CHUNKS + hp
                cols = slice(hp * LANES, (hp + 1) * LANES)
                tile = jnp.where(block_diag, ot_ref[unit] * stat_ref[1, unit, 0:1, :],
                                 stat_ref[1, unit, 1:2, :]).T
                top, bot = tile[0:SUB_Q], tile[SUB_Q:2 * SUB_Q]
                o_ref[r, pl.ds(q0, SUB_Q), cols] = jnp.where(first_head, top, bot).astype(BF16)
                lse_tile = jnp.where((lane % HEAD_DIM) // LSE_LANES == hp,
                                     jnp.where(first_head, bot, top), lse_tile)
            lse_ref[r, pl.ds(q0, SUB_Q), :] = lse_tile

    def trip(j, stages=(4, 3, 2, 1)):
        if 4 in stages:
            stage4_finish(j - 2)
        if 3 in stages:
            stage3_pv(j - 1)
        if 2 in stages:
            stage2_softmax(j)
        if 1 in stages:
            stage1_qk(j + 1)
            stage_vt(j + 1)

    trip(-1, stages=(1,))
    trip(0, stages=(2, 1))
    trip(1, stages=(3, 2, 1))

    def steady(j, c):
        trip(j)
        return c

    lax.fori_loop(2, n_blocks - 1, steady, 0)
    trip(n_blocks - 1, stages=(4, 3, 2))
    trip(n_blocks, stages=(4, 3))
    trip(n_blocks + 1, stages=(4,))


def _attn_call(qkv):
    _, bsz, dilation, seq_len, _ = qkv.shape
    whole = seq_len == KEY_BLOCK
    kw = KEY_BLOCK if whole else 3 * SUB_Q
    units = (KEY_BLOCK // SUB_Q) * GROUP_CHUNKS
    in_spec = lambda kind: pl.BlockSpec((None, None, dilation, seq_len, GROUP_WIDTH),
                                        lambda b: (kind, b, 0, 0, 0))
    out_spec = lambda w: pl.BlockSpec((None, dilation, seq_len, w), lambda b: (b, 0, 0, 0))
    return pl.pallas_call(
        _attn_kernel,
        grid=(bsz,),
        in_specs=[in_spec(0), in_spec(1), in_spec(2)],
        out_specs=[out_spec(GROUP_WIDTH), out_spec(LANES)],
        out_shape=[jax.ShapeDtypeStruct((bsz, dilation, seq_len, GROUP_WIDTH), BF16),
                   jax.ShapeDtypeStruct((bsz, dilation, seq_len, LANES), F32)],
        scratch_shapes=[
            pltpu.VMEM((dilation, seq_len + 2 * SUB_Q, GROUP_WIDTH), BF16),
            pltpu.VMEM((dilation, seq_len // KEY_BLOCK + (0 if whole else 2), GROUP_WIDTH, KEY_BLOCK), BF16),
            pltpu.VMEM((2 if whole else 3, kw, LANES), F32),
            pltpu.VMEM((units, kw, LANES), F32),
            pltpu.VMEM((units, kw, LANES), BF16),
            pltpu.VMEM((units, LANES, LANES), F32),
            pltpu.VMEM((2, units, SUBLANES, LANES), F32)],
        compiler_params=pltpu.CompilerParams(dimension_semantics=("arbitrary",),
                                             vmem_limit_bytes=VMEM_LIMIT),
        name=f"attn_d{dilation}",
    )(qkv, qkv, qkv)


CONV_PAD = 16
CONV_ROWS = 128
CONV_CHUNKS = CONV_CH // LANES


def _conv_kernel(u_ref, w_ref, b_ref, lnw_ref, lnb_ref, o_ref, pad_ref, conv_ref):
    s = u_ref.shape[1]
    zeros = jnp.zeros((CONV_PAD, LANES), F32)
    for c in range(CONV_CHUNKS):
        pad_ref[c, 0:CONV_PAD, :] = zeros
        pad_ref[c, CONV_PAD + s:2 * CONV_PAD + s, :] = zeros
        pad_ref[c, CONV_PAD:CONV_PAD + s, :] = u_ref[0, :, c * LANES:(c + 1) * LANES]
    first_tap = CONV_PAD - (CONV_WIDTH - 1) // 2
    half_rows = CONV_ROWS // 2

    def step(i, carry):
        r0 = i * CONV_ROWS
        for c in range(CONV_CHUNKS):
            cols = slice(c * LANES, (c + 1) * LANES)
            for parity in range(2):
                acc = jnp.zeros((half_rows, LANES), F32) + b_ref[:, cols]
                for t in range(CONV_WIDTH):
                    rows = pl.ds(r0 + parity + first_tap + t, half_rows, stride=2)
                    acc = acc + pad_ref[c, rows, :] * w_ref[t:t + 1, cols]
                conv_ref[c, pl.ds(parity, half_rows, stride=2), :] = acc
        ys = [conv_ref[c] for c in range(CONV_CHUNKS)]
        mu = sum(jnp.sum(y, axis=-1, keepdims=True) for y in ys) * (1.0 / CONV_CH)
        cen = [y - mu for y in ys]
        var = sum(jnp.sum(t * t, axis=-1, keepdims=True) for t in cen) * (1.0 / CONV_CH)
        inv = lax.rsqrt(var + EPS)
        for c in range(CONV_CHUNKS):
            cols = slice(c * LANES, (c + 1) * LANES)
            y = cen[c] * inv * lnw_ref[:, cols] + lnb_ref[:, cols]
            o_ref[0, pl.ds(pl.multiple_of(r0, CONV_ROWS), CONV_ROWS), cols] = (
                y * jax.nn.sigmoid(y)).astype(BF16)
        return carry

    lax.fori_loop(0, s // CONV_ROWS, step, 0)


def _conv_call(u, w, b, lnw, lnb):
    bsz, s, _ = u.shape
    return pl.pallas_call(
        _conv_kernel,
        grid=(bsz,),
        in_specs=[pl.BlockSpec((1, s, CONV_CH), lambda i: (i, 0, 0)),
                  _resident((CONV_WIDTH, CONV_CH)), _resident((1, CONV_CH)),
                  _resident((1, CONV_CH)), _resident((1, CONV_CH))],
        out_specs=pl.BlockSpec((1, s, CONV_CH), lambda i: (i, 0, 0)),
        out_shape=jax.ShapeDtypeStruct((bsz, s, CONV_CH), BF16),
        scratch_shapes=[pltpu.VMEM((CONV_CHUNKS, s + 2 * CONV_PAD, LANES), F32),
                        pltpu.VMEM((CONV_CHUNKS, CONV_ROWS, LANES), F32)],
        compiler_params=pltpu.CompilerParams(dimension_semantics=("arbitrary",),
                                             vmem_limit_bytes=VMEM_LIMIT),
        name="conv_ln",
    )(u, w, b, lnw, lnb)


FF_CHUNK = 256


def _mix_ffn_kernel(x_ref, o0_ref, o1_ref, o2_ref, l0_ref, l1_ref, l2_ref, u_ref, g_ref,
                    wo_ref, wpw_ref, wout_ref, n2_ref, w1_ref, w2_ref,
                    out_ref, attn_ref, act_ref, perm_ref):
    def token_order(ref, group, slab, cols):
        dilation = DILATED_PATTERNS[group][1]
        if dilation == 1:
            return ref[0, :, cols].astype(F32)
        n = ROW_TILE // dilation
        for r in range(dilation):
            perm_ref[slab, pl.ds(r, n, stride=dilation), :] = ref[r, :, cols].astype(F32)
        return perm_ref[slab]

    o_refs = (o0_ref, o1_ref, o2_ref)
    all_lanes = slice(0, LANES)
    n_slabs = GROUP_CHUNKS + 1
    lse = [token_order(ref, g, g * n_slabs + GROUP_CHUNKS, all_lanes)
           for g, ref in enumerate((l0_ref, l1_ref, l2_ref))]
    m = jnp.maximum(jnp.maximum(lse[0], lse[1]), lse[2])
    e = [jnp.exp(l - m) for l in lse]
    inv = 1.0 / (e[0] + e[1] + e[2])
    weights = [(e_g * inv).astype(BF16) for e_g in e]
    src_lane = lax.broadcasted_iota(jnp.int32, (LANES, LANES), 0)
    first_head = lax.broadcasted_iota(jnp.int32, (LANES, LANES), 1) < HEAD_DIM
    for c in range(GROUP_CHUNKS):
        cols = slice(c * LANES, (c + 1) * LANES)
        lane_b = LSE_LANES * c
        expand = jnp.where(src_lane == jnp.where(first_head, lane_b + HEAD_DIM, lane_b),
                           1.0, 0.0).astype(BF16)
        acc = jnp.zeros((ROW_TILE, LANES), F32)
        for g in range(N_GROUPS):
            acc = acc + _dot(weights[g], expand) * token_order(o_refs[g], g, g * n_slabs + c, cols)
        attn_ref[:, cols] = acc.astype(BF16)

    y_a = _dot(attn_ref[...], wo_ref[...])
    y_b = _dot(u_ref[...], wpw_ref[...])
    z = g_ref[:, 0:D_MODEL] * y_a + g_ref[:, D_MODEL:2 * D_MODEL] * y_b
    x1 = x_ref[...] + _dot(z.astype(BF16), wout_ref[...])

    ms = jnp.mean(x1 * x1, axis=-1, keepdims=True)
    h2 = (x1 * lax.rsqrt(ms + EPS) * n2_ref[...]).astype(BF16)
    for n in range(0, D_FF, FF_CHUNK):
        gt = _dot(h2, w1_ref[:, n:n + FF_CHUNK])
        up = _dot(h2, w1_ref[:, D_FF + n:D_FF + n + FF_CHUNK])
        act_ref[:, n:n + FF_CHUNK] = (gt * jax.nn.sigmoid(gt) * up).astype(BF16)
    out_ref[...] = x1 + _dot(act_ref[...], w2_ref[...])


def _mix_ffn_call(x3, o, lse, u_ln, gates, wo, wpw, wout, n2, w1, w2):
    bsz, s, _ = x3.shape
    dilations = [d for _, d in DILATED_PATTERNS]
    return pl.pallas_call(
        _mix_ffn_kernel,
        grid=(bsz, s // ROW_TILE),
        in_specs=[_row_block(D_MODEL)]
        + [_group_block(d, GROUP_WIDTH) for d in dilations]
        + [_group_block(d, LANES) for d in dilations]
        + [_row_block(CONV_CH), _row_block(2 * D_MODEL),
           _resident(wo.shape), _resident(wpw.shape),
           _resident(wout.shape), _resident(n2.shape), _resident(w1.shape), _resident(w2.shape)],
        out_specs=_row_block(D_MODEL),
        out_shape=jax.ShapeDtypeStruct((bsz, s, D_MODEL), F32),
        scratch_shapes=[pltpu.VMEM((ROW_TILE, GROUP_WIDTH), BF16),
                        pltpu.VMEM((ROW_TILE, D_FF), BF16),
                        pltpu.VMEM((N_GROUPS * (GROUP_CHUNKS + 1), ROW_TILE, LANES), F32)],
        compiler_params=pltpu.CompilerParams(dimension_semantics=("arbitrary", "arbitrary"),
                                             vmem_limit_bytes=VMEM_LIMIT),
        name="mix_ffn",
    )(x3, *o, *lse, u_ln, gates, wo, wpw, wout, n2, w1, w2)


def _rope_table(positions):
    inv_freq = ROPE_THETA ** (-jnp.arange(0, ROT_DIM, 2, dtype=F32) / ROT_DIM)
    ang = positions.astype(F32)[:, None, :] * inv_freq[None, :, None]
    rest = jnp.zeros((ang.shape[0], HEAD_DIM - ROT_DIM, ang.shape[2]), F32)
    return jnp.concatenate([jnp.cos(ang), jnp.sin(ang), rest] * (LANES // HEAD_DIM), axis=1)


def kernel(x, positions, norm1_w, w_in, b_gate, q_norm_w, k_norm_w, w_o_attn, conv_w, conv_b,
           conv_ln_w, conv_ln_b, w_pw_conv, w_out, norm2_w, w_ffn_in, w_ffn_out):
    d = x.shape[-1]
    depth = norm1_w.shape[0]
    rot = _rope_table(positions)
    pair = lambda w: jnp.tile(w.astype(F32), LANES // HEAD_DIM).reshape(1, LANES)

    for l in range(depth):
        qkv0, qkv1, qkv2, u, gates = _proj_call(
            x, norm1_w[l].reshape(1, d), w_in[l].astype(BF16), rot,
            pair(q_norm_w[l]), pair(k_norm_w[l]), b_gate[l].reshape(1, 2 * d))
        outs, lses = zip(*[_attn_call(qkv) for qkv in (qkv0, qkv1, qkv2)])
        u_ln = _conv_call(u, conv_w[l], conv_b[l].reshape(1, CONV_CH),
                          conv_ln_w[l].reshape(1, CONV_CH), conv_ln_b[l].reshape(1, CONV_CH))
        x = _mix_ffn_call(
            x, outs, lses, u_ln, gates,
            w_o_attn[l].astype(BF16), w_pw_conv[l].astype(BF16), w_out[l].astype(BF16),
            norm2_w[l].reshape(1, d), w_ffn_in[l].astype(BF16), w_ffn_out[l].astype(BF16))
    return x
```

```python
import functools

import jax
import jax.numpy as jnp
from jax import lax
from jax.experimental import pallas as pl
from jax.experimental.pallas import tpu as pltpu

D_MODEL = 1024
HEAD_DIM = 64
N_SLOT_HEADS = 8
DILATED_PATTERNS = ((128, 1), (512, 4), (2048, 16))
N_GROUPS = len(DILATED_PATTERNS)
GROUP_WIDTH = N_SLOT_HEADS * HEAD_DIM
QKV_WIDTH = N_GROUPS * GROUP_WIDTH
ROPE_THETA = 500000.0
ROT_DIM = HEAD_DIM // 4
CONV_CH = D_MODEL // 2
CONV_WIDTH = 31
D_FF = 2816
EPS = 1e-6
NEG_INF = -1e30

LANES = 128
SUBLANES = 8
SUB_Q = 64
KEY_BLOCK = 128
HALF_SPAN = 64
LSE_LANES = 16
LOG2E = 1.4426950408889634
LN2 = 0.6931471805599453
ROW_TILE = 512
EPI_ROWS = 128
GROUP_CHUNKS = GROUP_WIDTH // LANES
VMEM_LIMIT = 56 * 1024 * 1024
MIX_VMEM_LIMIT = 60 * 1024 * 1024

F32 = jnp.float32
BF16 = jnp.bfloat16


def _dot(a, b):
    return jnp.dot(a, b, preferred_element_type=F32)


def _split_dot(a_f32, b_bf16):
    hi = a_f32.astype(BF16)
    lo = (a_f32 - hi.astype(F32)).astype(BF16)
    return _dot(hi, b_bf16) + _dot(lo, b_bf16)


def _resident(shape):
    nd = len(shape)
    return pl.BlockSpec(shape, lambda *_: (0,) * nd, pipeline_mode=pl.Buffered(1))


def _row_block(width):
    return pl.BlockSpec((None, ROW_TILE, width), lambda b, i: (b, i, 0))


def _group_block(dilation, width, lead=()):
    n_lead = len(lead)
    return pl.BlockSpec(lead + (None, dilation, ROW_TILE // dilation, width),
                        lambda b, i: (0,) * n_lead + (b, 0, i, 0))


def _proj_kernel(x_ref, n1_ref, w_ref, rot_ref, qn_ref, kn_ref,
                 bg_ref, qkv0_ref, qkv1_ref, qkv2_ref, u_ref, g_ref, h_ref, perm_ref, acc_ref,
                 rope_ref):
    xt = x_ref[...]
    ms = jnp.mean(xt * xt, axis=-1, keepdims=True)
    h_ref[...] = (xt * lax.rsqrt(ms + EPS) * n1_ref[...]).astype(BF16)

    row_blocks = [slice(r, r + EPI_ROWS) for r in range(0, ROW_TILE, EPI_ROWS)]
    chunks = [slice(c * LANES, (c + 1) * LANES) for c in range(GROUP_CHUNKS)]

    half_rot = ROT_DIM // 2
    head_lane = lax.broadcasted_iota(jnp.int32, (1, LANES), 1) % HEAD_DIM
    for rows in row_blocks:
        tab = rot_ref[:, rows].T
        rope_ref[0, rows, :] = jnp.where(head_lane < half_rot, tab,
                                         jnp.where(head_lane < ROT_DIM, pltpu.roll(tab, half_rot, 1), 1.0))
        rope_ref[1, rows, :] = jnp.where(head_lane < half_rot, -pltpu.roll(tab, LANES - half_rot, 1), 0.0)
        rope_ref[2, rows, :] = jnp.where((head_lane >= half_rot) & (head_lane < ROT_DIM), tab, 0.0)

    n_proj = [0]

    def proj(c0):
        slot = n_proj[0] % acc_ref.shape[0]
        n_proj[0] += 1
        acc_ref[slot] = _dot(h_ref[...], w_ref[:, c0:c0 + GROUP_WIDTH])
        return acc_ref.at[slot]

    same_head = (lax.broadcasted_iota(jnp.int32, (LANES, LANES), 0) // HEAD_DIM
                 == lax.broadcasted_iota(jnp.int32, (LANES, LANES), 1) // HEAD_DIM)
    seg = jnp.where(same_head, 1.0, 0.0).astype(BF16)

    def head_norm_rope(t, nw, scale, rows):
        ssq = _dot((t * t).astype(BF16), seg)
        y = t * lax.rsqrt(ssq * (1.0 / HEAD_DIM) + EPS) * nw
        r = (y * rope_ref[0, rows, :] + pltpu.roll(y, LANES - half_rot, 1) * rope_ref[1, rows, :]
             + pltpu.roll(y, half_rot, 1) * rope_ref[2, rows, :])
        return r * scale

    def emit(kind, group, acc, fn):
        out_ref = (qkv0_ref, qkv1_ref, qkv2_ref)[group]
        dilation = DILATED_PATTERNS[group][1]
        n = ROW_TILE // dilation
        for c, cols in enumerate(chunks):
            for rows in row_blocks:
                val = fn(acc[rows, cols], rows)
                if dilation == 1:
                    out_ref[kind, 0, rows, cols] = val.astype(BF16)
                else:
                    perm_ref[c, rows, :] = val
            for r in range(dilation if dilation > 1 else 0):
                out_ref[kind, r, :, cols] = perm_ref[c, pl.ds(r, n, stride=dilation), :].astype(BF16)

    def qk_block(kind, j):
        nw_ref, scale = ((qn_ref, HEAD_DIM ** -0.5 * LOG2E), (kn_ref, 1.0))[kind]
        nw = nw_ref[...]
        acc = proj(kind * QKV_WIDTH + j * GROUP_WIDTH)
        emit(kind, j, acc, lambda t, rows: head_norm_rope(t, nw, scale, rows))

    def v_block(j):
        emit(2, j, proj(2 * QKV_WIDTH + j * GROUP_WIDTH), lambda t, rows: t)

    conv0 = 3 * QKV_WIDTH
    gate0 = conv0 + 2 * CONV_CH

    def gate_block(j):
        lo = j * GROUP_WIDTH
        acc = proj(gate0 + lo)
        for rows in row_blocks:
            g_ref[rows, lo:lo + GROUP_WIDTH] = jax.nn.sigmoid(
                acc[rows, :] + bg_ref[:, lo:lo + GROUP_WIDTH])

    for j in range(N_GROUPS):
        qk_block(0, j)
        v_block(j)
        qk_block(1, j)
        gate_block(j)
    a = proj(conv0)
    b = proj(conv0 + CONV_CH)
    for rows in row_blocks:
        u_ref[rows, :] = a[rows, :] * jax.nn.sigmoid(b[rows, :])
    gate_block(N_GROUPS)


def _proj_call(x3, n1, w_in, rot, qn, kn, bg):
    bsz, s, _ = x3.shape
    in_width = w_in.shape[1]
    qkv_shapes = [jax.ShapeDtypeStruct((3, bsz, d, s // d, GROUP_WIDTH), BF16)
                  for _, d in DILATED_PATTERNS]
    return pl.pallas_call(
        _proj_kernel,
        grid=(bsz, s // ROW_TILE),
        in_specs=[_row_block(D_MODEL), _resident((1, D_MODEL)), _resident((D_MODEL, in_width)),
                  pl.BlockSpec((None, LANES, ROW_TILE), lambda b, i: (b, 0, i)),
                  _resident((1, LANES)), _resident((1, LANES)),
                  _resident((1, 2 * D_MODEL))],
        out_specs=[_group_block(d, GROUP_WIDTH, lead=(3,)) for _, d in DILATED_PATTERNS]
        + [_row_block(CONV_CH), _row_block(2 * D_MODEL)],
        out_shape=qkv_shapes + [jax.ShapeDtypeStruct((bsz, s, CONV_CH), F32),
                                jax.ShapeDtypeStruct((bsz, s, 2 * D_MODEL), F32)],
        scratch_shapes=[pltpu.VMEM((ROW_TILE, D_MODEL), BF16),
                        pltpu.VMEM((GROUP_CHUNKS, ROW_TILE, LANES), F32),
                        pltpu.VMEM((3, ROW_TILE, GROUP_WIDTH), F32),
                        pltpu.VMEM((3, ROW_TILE, LANES), F32)],
        compiler_params=pltpu.CompilerParams(dimension_semantics=("arbitrary", "arbitrary"),
                                             vmem_limit_bytes=VMEM_LIMIT),
        name="proj",
    )(x3, n1, w_in, rot, qn, kn, bg)


def _attn_kernel(q_ref, k_ref, v_ref, o_ref, lse_ref, kpad_ref, vt_ref, bias_ref, s_ref, p_ref,
                 ot_ref, stat_ref):
    n_seq, seq_len, _ = q_ref.shape
    whole = seq_len == KEY_BLOCK
    kw = KEY_BLOCK if whole else 3 * SUB_Q
    n_key_blocks = seq_len // KEY_BLOCK
    n_blocks = n_seq * n_key_blocks
    vt_pad = 0 if whole else 1
    last_sub = seq_len // SUB_Q - 1
    subs = KEY_BLOCK // SUB_Q
    lane = lax.broadcasted_iota(jnp.int32, (1, LANES), 1)
    first_head = lane < HEAD_DIM
    block_diag = (lax.broadcasted_iota(jnp.int32, (LANES, LANES), 0) < HEAD_DIM) == first_head

    @pl.when(pl.program_id(0) == 0)
    def _init():
        row = lax.broadcasted_iota(jnp.int32, (kw, LANES), 0)
        lq = lax.broadcasted_iota(jnp.int32, (kw, LANES), 1) % SUB_Q
        as_bias = lambda valid: jnp.where(valid, 0.0, NEG_INF).astype(F32)
        if whole:
            for case in range(subs):
                bias_ref[case] = as_bias(jnp.abs(lq + case * SUB_Q - row) <= HALF_SPAN)
        else:
            band = jnp.abs(lq + SUB_Q - row) <= HALF_SPAN
            bias_ref[0] = as_bias(band & (row >= SUB_Q))
            bias_ref[1] = as_bias(band)
            bias_ref[2] = as_bias(band & (row < 2 * SUB_Q))
            zeros = jnp.zeros((SUB_Q, GROUP_WIDTH), BF16)
            zero_block = jnp.zeros((GROUP_WIDTH, KEY_BLOCK), BF16)
            for r in range(n_seq):
                kpad_ref[r, 0:SUB_Q, :] = zeros
                kpad_ref[r, SUB_Q + seq_len:2 * SUB_Q + seq_len, :] = zeros
                vt_ref[r, 0] = zero_block
                vt_ref[r, n_key_blocks + 1] = zero_block

    def split(j):
        return j // n_key_blocks, j % n_key_blocks

    if not whole:
        for r in range(n_seq):
            kpad_ref[r, SUB_Q:SUB_Q + seq_len, :] = k_ref[r]

    def stage_vt(j):
        r, b = split(j)
        rows = pl.ds(pl.multiple_of(b * KEY_BLOCK, KEY_BLOCK), KEY_BLOCK)
        for hp in range(GROUP_CHUNKS):
            cols = slice(hp * LANES, (hp + 1) * LANES)
            vt_ref[r, b + vt_pad, cols, :] = v_ref[r, rows, cols].astype(F32).T.astype(BF16)

    def stage1_qk(j):
        r, i = split(j)
        for sub in range(subs):
            q0 = pl.multiple_of((subs * i + sub) * SUB_Q, SUB_Q)
            for hp in range(GROUP_CHUNKS):
                cols = slice(hp * LANES, (hp + 1) * LANES)
                qb = q_ref[r, pl.ds(q0, SUB_Q), cols]
                zq = jnp.zeros_like(qb)
                qv = jnp.concatenate([jnp.where(first_head, qb, zq),
                                      jnp.where(first_head, zq, qb)], axis=0)
                kwin = k_ref[r, :, cols] if whole else kpad_ref[r, pl.ds(q0, kw), cols]
                s_ref[sub * GROUP_CHUNKS + hp] = lax.dot_general(
                    kwin, qv, (((1,), (1,)), ((), ())), preferred_element_type=F32)

    def stage2_softmax(j):
        _, i = split(j)
        for sub in range(subs):
            t = subs * i + sub
            if whole:
                bias = bias_ref[sub]
            else:
                bias = bias_ref[jnp.where(t == 0, 0, jnp.where(t == last_sub, 2, 1))]
            for hp in range(GROUP_CHUNKS):
                unit = sub * GROUP_CHUNKS + hp
                s = s_ref[unit] + bias
                m = jnp.max(s, axis=0, keepdims=True)
                p = jnp.exp2(s - m)
                den = jnp.sum(p, axis=0, keepdims=True)
                p_ref[unit] = p.astype(BF16)
                stat_ref[0, unit, 0:1, :] = 1.0 / den
                stat_ref[0, unit, 1:2, :] = (m + jnp.log2(den)) * LN2

    def stage3_pv(j):
        r, i = split(j)
        for sub in range(subs):
            for hp in range(GROUP_CHUNKS):
                unit = sub * GROUP_CHUNKS + hp
                cols = slice(hp * LANES, (hp + 1) * LANES)
                pb = p_ref[unit]
                if whole:
                    ot_ref[unit] = _dot(vt_ref[r, 0, cols, :], pb)
                else:
                    first = i + sub
                    vt2 = jnp.concatenate([vt_ref[r, first, cols, :], vt_ref[r, first + 1, cols, :]], axis=1)
                    zp = jnp.zeros((SUB_Q, LANES), BF16)
                    ot_ref[unit] = _dot(vt2, jnp.concatenate([zp, pb] if sub == 0 else [pb, zp], axis=0))
                stat_ref[1, unit] = stat_ref[0, unit]

    def stage4_finish(j):
        r, i = split(j)
        for sub in range(subs):
            q0 = pl.multiple_of((subs * i + sub) * SUB_Q, SUB_Q)
            lse_tile = jnp.zeros((SUB_Q, LANES), F32)
            for hp in range(GROUP_CHUNKS):
                unit = sub * GROUP_CHUNKS + hp
                cols = slice(hp * LANES, (hp + 1) * LANES)
                tile = jnp.where(block_diag, ot_ref[unit] * stat_ref[1, unit, 0:1, :],
                                 stat_ref[1, unit, 1:2, :]).T
                top, bot = tile[0:SUB_Q], tile[SUB_Q:2 * SUB_Q]
                o_ref[r, pl.ds(q0, SUB_Q), cols] = jnp.where(first_head, top, bot).astype(BF16)
                lse_tile = jnp.where((lane % HEAD_DIM) // LSE_LANES == hp,
                                     jnp.where(first_head, bot, top), lse_tile)
            lse_ref[r, pl.ds(q0, SUB_Q), :] = lse_tile

    def trip(j, stages=(4, 3, 2, 1)):
        if 4 in stages:
            stage4_finish(j - 2)
        if 3 in stages:
            stage3_pv(j - 1)
        if 2 in stages:
            stage2_softmax(j)
        if 1 in stages:
            stage1_qk(j + 1)
            stage_vt(j + 1)

    trip(-1, stages=(1,))
    trip(0, stages=(2, 1))
    trip(1, stages=(3, 2, 1))

    def steady(j, c):
        trip(j)
        return c

    lax.fori_loop(2, n_blocks - 1, steady, 0)
    trip(n_blocks - 1, stages=(4, 3, 2))
    trip(n_blocks, stages=(4, 3))
    trip(n_blocks + 1, stages=(4,))


def _attn_call(qkv):
    _, bsz, dilation, seq_len, _ = qkv.shape
    whole = seq_len == KEY_BLOCK
    kw = KEY_BLOCK if whole else 3 * SUB_Q
    units = (KEY_BLOCK // SUB_Q) * GROUP_CHUNKS
    in_spec = lambda kind: pl.BlockSpec((None, None, dilation, seq_len, GROUP_WIDTH),
                                        lambda b: (kind, b, 0, 0, 0))
    out_spec = lambda w: pl.BlockSpec((None, dilation, seq_len, w), lambda b: (b, 0, 0, 0))
    return pl.pallas_call(
        _attn_kernel,
        grid=(bsz,),
        in_specs=[in_spec(0), in_spec(1), in_spec(2)],
        out_specs=[out_spec(GROUP_WIDTH), out_spec(LANES)],
        out_shape=[jax.ShapeDtypeStruct((bsz, dilation, seq_len, GROUP_WIDTH), BF16),
                   jax.ShapeDtypeStruct((bsz, dilation, seq_len, LANES), F32)],
        scratch_shapes=[
            pltpu.VMEM((dilation, seq_len + 2 * SUB_Q, GROUP_WIDTH), BF16),
            pltpu.VMEM((dilation, seq_len // KEY_BLOCK + (0 if whole else 2), GROUP_WIDTH, KEY_BLOCK), BF16),
            pltpu.VMEM((2 if whole else 3, kw, LANES), F32),
            pltpu.VMEM((units, kw, LANES), F32),
            pltpu.VMEM((units, kw, LANES), BF16),
            pltpu.VMEM((units, LANES, LANES), F32),
            pltpu.VMEM((2, units, SUBLANES, LANES), F32)],
        compiler_params=pltpu.CompilerParams(dimension_semantics=("arbitrary",),
                                             vmem_limit_bytes=VMEM_LIMIT),
        name=f"attn_d{dilation}",
    )(qkv, qkv, qkv)


CONV_PAD = 16
CONV_ROWS = 128
CONV_CHUNKS = CONV_CH // LANES


def _conv_stage(u_ref, prev_ref, next_ref, has_prev, has_next, pad_ref):
    for c in range(CONV_CHUNKS):
        cols = slice(c * LANES, (c + 1) * LANES)
        pad_ref[c, 0:CONV_PAD, :] = jnp.where(has_prev, prev_ref[:, cols], 0.0)
        pad_ref[c, CONV_PAD:CONV_PAD + ROW_TILE, :] = u_ref[:, cols]
        pad_ref[c, CONV_PAD + ROW_TILE:2 * CONV_PAD + ROW_TILE, :] = jnp.where(
            has_next, next_ref[:, cols], 0.0)


def _conv_ln_rows(r0, w_ref, b_ref, lnw_ref, lnb_ref, out_ref, pad_ref, conv_ref):
    chunks = [slice(c * LANES, (c + 1) * LANES) for c in range(CONV_CHUNKS)]
    first_tap = CONV_PAD - (CONV_WIDTH - 1) // 2
    half_rows = CONV_ROWS // 2
    for c, cols in enumerate(chunks):
        for parity in range(2):
            acc = jnp.zeros((half_rows, LANES), F32) + b_ref[:, cols]
            for t in range(CONV_WIDTH):
                rows = pl.ds(r0 + parity + first_tap + t, half_rows, stride=2)
                acc = acc + pad_ref[c, rows, :] * w_ref[t:t + 1, cols]
            conv_ref[c, pl.ds(r0 + parity, half_rows, stride=2), :] = acc
    ys = [conv_ref[c, r0:r0 + CONV_ROWS, :] for c in range(CONV_CHUNKS)]
    mu = sum(jnp.sum(y, axis=-1, keepdims=True) for y in ys) * (1.0 / CONV_CH)
    cen = [y - mu for y in ys]
    var = sum(jnp.sum(t * t, axis=-1, keepdims=True) for t in cen) * (1.0 / CONV_CH)
    inv = lax.rsqrt(var + EPS)
    for c, cols in enumerate(chunks):
        y = cen[c] * inv * lnw_ref[:, cols] + lnb_ref[:, cols]
        out_ref[r0:r0 + CONV_ROWS, cols] = (y * jax.nn.sigmoid(y)).astype(BF16)
    return y[0:1, :]


def _zero_after(value):
    bits = value.astype(jnp.int32)
    return lax.shift_right_logical(lax.shift_right_logical(bits, 16), 16).astype(F32)


FF_CHUNK = 256


def _mix_ffn_kernel(x_ref, o0_ref, o1_ref, o2_ref, l0_ref, l1_ref, l2_ref, g_ref,
                    uc_ref, up_ref, un_ref, cw_ref, cb_ref, lnw_ref, lnb_ref,
                    wo_ref, wpw_ref, wout_ref, n2_ref, w1_ref, w2_ref,
                    out_ref, attn_ref, act_ref, perm_ref, pad_ref, conv_ref, uln_ref,
                    *, tiles_per_seq):
    t = pl.program_id(0)
    conv_pos = jnp.minimum(t, pl.num_programs(0) - 2) % tiles_per_seq
    write_slot = t % 2
    read_slot = 1 - write_slot

    @pl.when(t == 0)
    def _init():
        uln_ref[1] = jnp.zeros(uln_ref.shape[1:], BF16)

    _conv_stage(uc_ref, up_ref, un_ref, conv_pos > 0, conv_pos < tiles_per_seq - 1, pad_ref)

    def token_order(ref, group, slab, cols):
        dilation = DILATED_PATTERNS[group][1]
        if dilation == 1:
            return ref[0, :, cols].astype(F32)
        n = ROW_TILE // dilation
        for r in range(dilation):
            perm_ref[slab, pl.ds(r, n, stride=dilation), :] = ref[r, :, cols].astype(F32)
        return perm_ref[slab]

    o_refs = (o0_ref, o1_ref, o2_ref)
    all_lanes = slice(0, LANES)
    n_slabs = GROUP_CHUNKS + 1
    lse = [token_order(ref, g, g * n_slabs + GROUP_CHUNKS, all_lanes)
           for g, ref in enumerate((l0_ref, l1_ref, l2_ref))]
    m = jnp.maximum(jnp.maximum(lse[0], lse[1]), lse[2])
    e = [jnp.exp(l - m) for l in lse]
    inv = 1.0 / (e[0] + e[1] + e[2])
    weights = [(e_g * inv).astype(BF16) for e_g in e]
    src_lane = lax.broadcasted_iota(jnp.int32, (LANES, LANES), 0)
    first_head = lax.broadcasted_iota(jnp.int32, (LANES, LANES), 1) < HEAD_DIM
    for c in range(GROUP_CHUNKS):
        cols = slice(c * LANES, (c + 1) * LANES)
        lane_b = LSE_LANES * c
        expand = jnp.where(src_lane == jnp.where(first_head, lane_b + HEAD_DIM, lane_b),
                           1.0, 0.0).astype(BF16)
        acc = jnp.zeros((ROW_TILE, LANES), F32)
        for g in range(N_GROUPS):
            acc = acc + _dot(weights[g], expand) * token_order(o_refs[g], g, g * n_slabs + c, cols)
        attn_ref[:, cols] = acc.astype(BF16)

    y_a = _dot(attn_ref[...], wo_ref[...])
    y_b = _dot(uln_ref[read_slot], wpw_ref[...])
    z = g_ref[:, 0:D_MODEL] * y_a + g_ref[:, D_MODEL:2 * D_MODEL] * y_b
    x1 = x_ref[...] + _dot(z.astype(BF16), wout_ref[...])

    ms = jnp.mean(x1 * x1, axis=-1, keepdims=True)
    h2 = (x1 * lax.rsqrt(ms + EPS) * n2_ref[...]).astype(BF16)
    ff_chunks = range(0, D_FF, FF_CHUNK)
    conv_blocks = list(range(0, ROW_TILE, CONV_ROWS))
    conv_every = len(ff_chunks) // len(conv_blocks)
    pending = None
    for idx, n in enumerate(ff_chunks):
        gt = _dot(h2, w1_ref[:, n:n + FF_CHUNK])
        up = _dot(h2, w1_ref[:, D_FF + n:D_FF + n + FF_CHUNK])
        act = gt * jax.nn.sigmoid(gt) * up
        if pending is not None and idx % conv_every == conv_every - 1:
            act = act + jnp.concatenate([_zero_after(pending)] * (FF_CHUNK // LANES), axis=1)
            pending = None
        act_ref[:, n:n + FF_CHUNK] = act.astype(BF16)
        if idx % conv_every == 0 and idx // conv_every < len(conv_blocks):
            pending = _conv_ln_rows(conv_blocks[idx // conv_every], cw_ref, cb_ref, lnw_ref,
                                    lnb_ref, uln_ref.at[write_slot], pad_ref, conv_ref)
    out_ref[...] = x1 + _dot(act_ref[...], w2_ref[...])


def _mix_ffn_call(x3, o, lse, gates, u, conv_w, conv_b, ln_w, ln_b, wo, wpw, wout, n2, w1, w2):
    bsz, s, _ = x3.shape
    dilations = [d for _, d in DILATED_PATTERNS]
    tiles_per_seq = s // ROW_TILE
    n_tiles = bsz * tiles_per_seq
    halo_blocks = ROW_TILE // CONV_PAD

    def finish_tile(t):
        tile = jnp.maximum(t - 1, 0)
        return tile // tiles_per_seq, tile % tiles_per_seq

    def conv_tile(t):
        tile = jnp.minimum(t, n_tiles - 1)
        return tile // tiles_per_seq, tile % tiles_per_seq

    def row(width):
        return pl.BlockSpec((None, ROW_TILE, width), lambda t: (*finish_tile(t), 0))

    def group(dilation, width):
        return pl.BlockSpec((None, dilation, ROW_TILE // dilation, width),
                            lambda t: (finish_tile(t)[0], 0, finish_tile(t)[1], 0))

    def halo(offset):
        def index(t):
            b, i = conv_tile(t)
            block = i * halo_blocks - 1 if offset < 0 else (i + 1) * halo_blocks
            return b, jnp.clip(block, 0, s // CONV_PAD - 1), 0
        return pl.BlockSpec((None, CONV_PAD, CONV_CH), index)

    return pl.pallas_call(
        functools.partial(_mix_ffn_kernel, tiles_per_seq=tiles_per_seq),
        grid=(n_tiles + 1,),
        in_specs=[row(D_MODEL)]
        + [group(d, GROUP_WIDTH) for d in dilations]
        + [group(d, LANES) for d in dilations]
        + [row(2 * D_MODEL),
           pl.BlockSpec((None, ROW_TILE, CONV_CH), lambda t: (*conv_tile(t), 0)), halo(-1), halo(1),
           _resident(conv_w.shape), _resident(conv_b.shape), _resident(ln_w.shape),
           _resident(ln_b.shape),
           _resident(wo.shape), _resident(wpw.shape),
           _resident(wout.shape), _resident(n2.shape), _resident(w1.shape), _resident(w2.shape)],
        out_specs=row(D_MODEL),
        out_shape=jax.ShapeDtypeStruct((bsz, s, D_MODEL), F32),
        scratch_shapes=[pltpu.VMEM((ROW_TILE, GROUP_WIDTH), BF16),
                        pltpu.VMEM((ROW_TILE, D_FF), BF16),
                        pltpu.VMEM((N_GROUPS * (GROUP_CHUNKS + 1), ROW_TILE, LANES), F32),
                        pltpu.VMEM((CONV_CHUNKS, ROW_TILE + 2 * CONV_PAD, LANES), F32),
                        pltpu.VMEM((CONV_CHUNKS, ROW_TILE, LANES), F32),
                        pltpu.VMEM((2, ROW_TILE, CONV_CH), BF16)],
        compiler_params=pltpu.CompilerParams(dimension_semantics=("arbitrary",),
                                             vmem_limit_bytes=MIX_VMEM_LIMIT),
        name="mix_ffn",
    )(x3, *o, *lse, gates, u, u, u, conv_w, conv_b, ln_w, ln_b, wo, wpw, wout, n2, w1, w2)


def _rope_table(positions):
    inv_freq = ROPE_THETA ** (-jnp.arange(0, ROT_DIM, 2, dtype=F32) / ROT_DIM)
    ang = positions.astype(F32)[:, None, :] * inv_freq[None, :, None]
    rest = jnp.zeros((ang.shape[0], HEAD_DIM - ROT_DIM, ang.shape[2]), F32)
    return jnp.concatenate([jnp.cos(ang), jnp.sin(ang), rest] * (LANES // HEAD_DIM), axis=1)


def kernel(x, positions, norm1_w, w_in, b_gate, q_norm_w, k_norm_w, w_o_attn, conv_w, conv_b,
           conv_ln_w, conv_ln_b, w_pw_conv, w_out, norm2_w, w_ffn_in, w_ffn_out):
    d = x.shape[-1]
    depth = norm1_w.shape[0]
    rot = _rope_table(positions)
    pair = lambda w: jnp.tile(w.astype(F32), LANES // HEAD_DIM).reshape(1, LANES)

    for l in range(depth):
        qkv0, qkv1, qkv2, u, gates = _proj_call(
            x, norm1_w[l].reshape(1, d), w_in[l].astype(BF16), rot,
            pair(q_norm_w[l]), pair(k_norm_w[l]), b_gate[l].reshape(1, 2 * d))
        outs, lses = zip(*[_attn_call(qkv) for qkv in (qkv0, qkv1, qkv2)])
        x = _mix_ffn_call(
            x, outs, lses, gates, u, conv_w[l], conv_b[l].reshape(1, CONV_CH),
            conv_ln_w[l].reshape(1, CONV_CH), conv_ln_b[l].reshape(1, CONV_CH),
            w_o_attn[l].astype(BF16), w_pw_conv[l].astype(BF16), w_out[l].astype(BF16),
            norm2_w[l].reshape(1, d), w_ffn_in[l].astype(BF16), w_ffn_out[l].astype(BF16))
    return x
```

```python
import functools

import jax
import jax.numpy as jnp
from jax import lax
from jax.experimental import pallas as pl
from jax.experimental.pallas import tpu as pltpu

D_MODEL = 1024
HEAD_DIM = 64
N_SLOT_HEADS = 8
DILATED_PATTERNS = ((128, 1), (512, 4), (2048, 16))
N_GROUPS = len(DILATED_PATTERNS)
GROUP_WIDTH = N_SLOT_HEADS * HEAD_DIM
QKV_WIDTH = N_GROUPS * GROUP_WIDTH
ROPE_THETA = 500000.0
ROT_DIM = HEAD_DIM // 4
CONV_CH = D_MODEL // 2
CONV_WIDTH = 31
D_FF = 2816
EPS = 1e-6
NEG_INF = -1e30

LANES = 128
SUBLANES = 8
SUB_Q = 64
KEY_BLOCK = 128
HALF_SPAN = 64
LSE_LANES = 16
LOG2E = 1.4426950408889634
LN2 = 0.6931471805599453
ROW_TILE = 512
EPI_ROWS = 128
GROUP_CHUNKS = GROUP_WIDTH // LANES
VMEM_LIMIT = 56 * 1024 * 1024
MIX_VMEM_LIMIT = 60 * 1024 * 1024

F32 = jnp.float32
BF16 = jnp.bfloat16


def _dot(a, b):
    return jnp.dot(a, b, preferred_element_type=F32)


def _split_dot(a_f32, b_bf16):
    hi = a_f32.astype(BF16)
    lo = (a_f32 - hi.astype(F32)).astype(BF16)
    return _dot(hi, b_bf16) + _dot(lo, b_bf16)


def _resident(shape):
    nd = len(shape)
    return pl.BlockSpec(shape, lambda *_: (0,) * nd, pipeline_mode=pl.Buffered(1))


def _row_block(width):
    return pl.BlockSpec((None, ROW_TILE, width), lambda b, i: (b, i, 0))


def _group_block(dilation, width, lead=()):
    n_lead = len(lead)
    return pl.BlockSpec(lead + (None, dilation, ROW_TILE // dilation, width),
                        lambda b, i: (0,) * n_lead + (b, 0, i, 0))


def _proj_kernel(x_ref, n1_ref, w_ref, rot_ref, qn_ref, kn_ref,
                 bg_ref, qkv0_ref, qkv1_ref, qkv2_ref, u_ref, g_ref, h_ref, perm_ref, acc_ref,
                 rope_ref):
    xt = x_ref[...]
    ms = jnp.mean(xt * xt, axis=-1, keepdims=True)
    h_ref[...] = (xt * lax.rsqrt(ms + EPS) * n1_ref[...]).astype(BF16)

    row_blocks = [slice(r, r + EPI_ROWS) for r in range(0, ROW_TILE, EPI_ROWS)]
    chunks = [slice(c * LANES, (c + 1) * LANES) for c in range(GROUP_CHUNKS)]

    half_rot = ROT_DIM // 2
    head_lane = lax.broadcasted_iota(jnp.int32, (1, LANES), 1) % HEAD_DIM
    for rows in row_blocks:
        tab = rot_ref[:, rows].T
        rope_ref[0, rows, :] = jnp.where(head_lane < half_rot, tab,
                                         jnp.where(head_lane < ROT_DIM, pltpu.roll(tab, half_rot, 1), 1.0))
        rope_ref[1, rows, :] = jnp.where(head_lane < half_rot, -pltpu.roll(tab, LANES - half_rot, 1), 0.0)
        rope_ref[2, rows, :] = jnp.where((head_lane >= half_rot) & (head_lane < ROT_DIM), tab, 0.0)

    n_proj = [0]

    def proj(c0):
        slot = n_proj[0] % acc_ref.shape[0]
        n_proj[0] += 1
        acc_ref[slot] = _dot(h_ref[...], w_ref[:, c0:c0 + GROUP_WIDTH])
        return acc_ref.at[slot]

    same_head = (lax.broadcasted_iota(jnp.int32, (LANES, LANES), 0) // HEAD_DIM
                 == lax.broadcasted_iota(jnp.int32, (LANES, LANES), 1) // HEAD_DIM)
    seg = jnp.where(same_head, 1.0, 0.0).astype(BF16)

    def head_norm_rope(t, nw, scale, rows):
        ssq = _dot((t * t).astype(BF16), seg)
        y = t * lax.rsqrt(ssq * (1.0 / HEAD_DIM) + EPS) * nw
        r = (y * rope_ref[0, rows, :] + pltpu.roll(y, LANES - half_rot, 1) * rope_ref[1, rows, :]
             + pltpu.roll(y, half_rot, 1) * rope_ref[2, rows, :])
        return r * scale

    def emit(kind, group, acc, fn):
        out_ref = (qkv0_ref, qkv1_ref, qkv2_ref)[group]
        dilation = DILATED_PATTERNS[group][1]
        n = ROW_TILE // dilation
        for c, cols in enumerate(chunks):
            for rows in row_blocks:
                val = fn(acc[rows, cols], rows)
                if dilation == 1:
                    out_ref[kind, 0, rows, cols] = val.astype(BF16)
                else:
                    perm_ref[c, rows, :] = val
            for r in range(dilation if dilation > 1 else 0):
                out_ref[kind, r, :, cols] = perm_ref[c, pl.ds(r, n, stride=dilation), :].astype(BF16)

    def qk_block(kind, j):
        nw_ref, scale = ((qn_ref, HEAD_DIM ** -0.5 * LOG2E), (kn_ref, 1.0))[kind]
        nw = nw_ref[...]
        acc = proj(kind * QKV_WIDTH + j * GROUP_WIDTH)
        emit(kind, j, acc, lambda t, rows: head_norm_rope(t, nw, scale, rows))

    def v_block(j):
        emit(2, j, proj(2 * QKV_WIDTH + j * GROUP_WIDTH), lambda t, rows: t)

    conv0 = 3 * QKV_WIDTH
    gate0 = conv0 + 2 * CONV_CH

    def gate_block(j):
        lo = j * GROUP_WIDTH
        acc = proj(gate0 + lo)
        for rows in row_blocks:
            g_ref[rows, lo:lo + GROUP_WIDTH] = jax.nn.sigmoid(
                acc[rows, :] + bg_ref[:, lo:lo + GROUP_WIDTH]).astype(BF16)

    for j in range(N_GROUPS):
        qk_block(0, j)
        v_block(j)
        qk_block(1, j)
        gate_block(j)
    a = proj(conv0)
    b = proj(conv0 + CONV_CH)
    for rows in row_blocks:
        u_ref[rows, :] = a[rows, :] * jax.nn.sigmoid(b[rows, :])
    gate_block(N_GROUPS)


def _proj_call(x3, n1, w_in, rot, qn, kn, bg):
    bsz, s, _ = x3.shape
    in_width = w_in.shape[1]
    qkv_shapes = [jax.ShapeDtypeStruct((3, bsz, d, s // d, GROUP_WIDTH), BF16)
                  for _, d in DILATED_PATTERNS]
    return pl.pallas_call(
        _proj_kernel,
        grid=(bsz, s // ROW_TILE),
        in_specs=[_row_block(D_MODEL), _resident((1, D_MODEL)), _resident((D_MODEL, in_width)),
                  pl.BlockSpec((None, LANES, ROW_TILE), lambda b, i: (b, 0, i)),
                  _resident((1, LANES)), _resident((1, LANES)),
                  _resident((1, 2 * D_MODEL))],
        out_specs=[_group_block(d, GROUP_WIDTH, lead=(3,)) for _, d in DILATED_PATTERNS]
        + [_row_block(CONV_CH), _row_block(2 * D_MODEL)],
        out_shape=qkv_shapes + [jax.ShapeDtypeStruct((bsz, s, CONV_CH), F32),
                                jax.ShapeDtypeStruct((bsz, s, 2 * D_MODEL), BF16)],
        scratch_shapes=[pltpu.VMEM((ROW_TILE, D_MODEL), BF16),
                        pltpu.VMEM((GROUP_CHUNKS, ROW_TILE, LANES), F32),
                        pltpu.VMEM((3, ROW_TILE, GROUP_WIDTH), F32),
                        pltpu.VMEM((3, ROW_TILE, LANES), F32)],
        compiler_params=pltpu.CompilerParams(dimension_semantics=("arbitrary", "arbitrary"),
                                             vmem_limit_bytes=VMEM_LIMIT),
        name="proj",
    )(x3, n1, w_in, rot, qn, kn, bg)


def _attn_kernel(q_ref, k_ref, v_ref, o_ref, lse_ref, kpad_ref, vt_ref, bias_ref, s_ref, p_ref,
                 ot_ref, stat_ref):
    n_seq, seq_len, _ = q_ref.shape
    whole = seq_len == KEY_BLOCK
    kw = KEY_BLOCK if whole else 3 * SUB_Q
    n_key_blocks = seq_len // KEY_BLOCK
    n_blocks = n_seq * n_key_blocks
    vt_pad = 0 if whole else 1
    last_sub = seq_len // SUB_Q - 1
    subs = KEY_BLOCK // SUB_Q
    lane = lax.broadcasted_iota(jnp.int32, (1, LANES), 1)
    first_head = lane < HEAD_DIM
    block_diag = (lax.broadcasted_iota(jnp.int32, (LANES, LANES), 0) < HEAD_DIM) == first_head

    @pl.when(pl.program_id(0) == 0)
    def _init():
        row = lax.broadcasted_iota(jnp.int32, (kw, LANES), 0)
        lq = lax.broadcasted_iota(jnp.int32, (kw, LANES), 1) % SUB_Q
        as_bias = lambda valid: jnp.where(valid, 0.0, NEG_INF).astype(F32)
        if whole:
            for case in range(subs):
                bias_ref[case] = as_bias(jnp.abs(lq + case * SUB_Q - row) <= HALF_SPAN)
        else:
            band = jnp.abs(lq + SUB_Q - row) <= HALF_SPAN
            bias_ref[0] = as_bias(band & (row >= SUB_Q))
            bias_ref[1] = as_bias(band)
            bias_ref[2] = as_bias(band & (row < 2 * SUB_Q))
            zeros = jnp.zeros((SUB_Q, GROUP_WIDTH), BF16)
            zero_block = jnp.zeros((GROUP_WIDTH, KEY_BLOCK), BF16)
            for r in range(n_seq):
                kpad_ref[r, 0:SUB_Q, :] = zeros
                kpad_ref[r, SUB_Q + seq_len:2 * SUB_Q + seq_len, :] = zeros
                vt_ref[r, 0] = zero_block
                vt_ref[r, n_key_blocks + 1] = zero_block

    def split(j):
        return j // n_key_blocks, j % n_key_blocks

    if not whole:
        for r in range(n_seq):
            kpad_ref[r, SUB_Q:SUB_Q + seq_len, :] = k_ref[r]

    def stage_vt(j):
        r, b = split(j)
        rows = pl.ds(pl.multiple_of(b * KEY_BLOCK, KEY_BLOCK), KEY_BLOCK)
        for hp in range(GROUP_CHUNKS):
            cols = slice(hp * LANES, (hp + 1) * LANES)
            vt_ref[r, b + vt_pad, cols, :] = v_ref[r, rows, cols].astype(F32).T.astype(BF16)

    def stage1_qk(j):
        r, i = split(j)
        for sub in range(subs):
            q0 = pl.multiple_of((subs * i + sub) * SUB_Q, SUB_Q)
            for hp in range(GROUP_CHUNKS):
                cols = slice(hp * LANES, (hp + 1) * LANES)
                qb = q_ref[r, pl.ds(q0, SUB_Q), cols]
                zq = jnp.zeros_like(qb)
                qv = jnp.concatenate([jnp.where(first_head, qb, zq),
                                      jnp.where(first_head, zq, qb)], axis=0)
                kwin = k_ref[r, :, cols] if whole else kpad_ref[r, pl.ds(q0, kw), cols]
                s_ref[sub * GROUP_CHUNKS + hp] = lax.dot_general(
                    kwin, qv, (((1,), (1,)), ((), ())), preferred_element_type=F32)

    def stage2_softmax(j):
        _, i = split(j)
        for sub in range(subs):
            t = subs * i + sub
            if whole:
                bias = bias_ref[sub]
            else:
                bias = bias_ref[jnp.where(t == 0, 0, jnp.where(t == last_sub, 2, 1))]
            for hp in range(GROUP_CHUNKS):
                unit = sub * GROUP_CHUNKS + hp
                s = s_ref[unit] + bias
                m = jnp.max(s, axis=0, keepdims=True)
                p = jnp.exp2(s - m)
                den = jnp.sum(p, axis=0, keepdims=True)
                p_ref[unit] = p.astype(BF16)
                stat_ref[0, unit, 0:1, :] = 1.0 / den
                stat_ref[0, unit, 1:2, :] = (m + jnp.log2(den)) * LN2

    def stage3_pv(j):
        r, i = split(j)
        for sub in range(subs):
            for hp in range(GROUP_CHUNKS):
                unit = sub * GROUP_CHUNKS + hp
                cols = slice(hp * LANES, (hp + 1) * LANES)
                pb = p_ref[unit]
                if whole:
                    ot_ref[unit] = _dot(vt_ref[r, 0, cols, :], pb)
                else:
                    first = i + sub
                    vt2 = jnp.concatenate([vt_ref[r, first, cols, :], vt_ref[r, first + 1, cols, :]], axis=1)
                    zp = jnp.zeros((SUB_Q, LANES), BF16)
                    ot_ref[unit] = _dot(vt2, jnp.concatenate([zp, pb] if sub == 0 else [pb, zp], axis=0))
                stat_ref[1, unit] = stat_ref[0, unit]

    def stage4_finish(j):
        r, i = split(j)
        for sub in range(subs):
            q0 = pl.multiple_of((subs * i + sub) * SUB_Q, SUB_Q)
            lse_tile = jnp.zeros((SUB_Q, LANES), F32)
            for hp in range(GROUP_CHUNKS):
                unit = sub * GROUP_CHUNKS + hp
                cols = slice(hp * LANES, (hp + 1) * LANES)
                tile = jnp.where(block_diag, ot_ref[unit] * stat_ref[1, unit, 0:1, :],
                                 stat_ref[1, unit, 1:2, :]).T
                top, bot = tile[0:SUB_Q], tile[SUB_Q:2 * SUB_Q]
                o_ref[r, pl.ds(q0, SUB_Q), cols] = jnp.where(first_head, top, bot).astype(BF16)
                lse_tile = jnp.where((lane % HEAD_DIM) // LSE_LANES == hp,
                                     jnp.where(first_head, bot, top), lse_tile)
            lse_ref[r, pl.ds(q0, SUB_Q), :] = lse_tile

    def trip(j, stages=(4, 3, 2, 1)):
        if 4 in stages:
            stage4_finish(j - 2)
        if 3 in stages:
            stage3_pv(j - 1)
        if 2 in stages:
            stage2_softmax(j)
        if 1 in stages:
            stage1_qk(j + 1)
            stage_vt(j + 1)

    trip(-1, stages=(1,))
    trip(0, stages=(2, 1))
    trip(1, stages=(3, 2, 1))

    def steady(j, c):
        trip(j)
        return c

    lax.fori_loop(2, n_blocks - 1, steady, 0)
    trip(n_blocks - 1, stages=(4, 3, 2))
    trip(n_blocks, stages=(4, 3))
    trip(n_blocks + 1, stages=(4,))


def _attn_call(qkv):
    _, bsz, dilation, seq_len, _ = qkv.shape
    whole = seq_len == KEY_BLOCK
    kw = KEY_BLOCK if whole else 3 * SUB_Q
    units = (KEY_BLOCK // SUB_Q) * GROUP_CHUNKS
    in_spec = lambda kind: pl.BlockSpec((None, None, dilation, seq_len, GROUP_WIDTH),
                                        lambda b: (kind, b, 0, 0, 0))
    out_spec = lambda w: pl.BlockSpec((None, dilation, seq_len, w), lambda b: (b, 0, 0, 0))
    return pl.pallas_call(
        _attn_kernel,
        grid=(bsz,),
        in_specs=[in_spec(0), in_spec(1), in_spec(2)],
        out_specs=[out_spec(GROUP_WIDTH), out_spec(LANES)],
        out_shape=[jax.ShapeDtypeStruct((bsz, dilation, seq_len, GROUP_WIDTH), BF16),
                   jax.ShapeDtypeStruct((bsz, dilation, seq_len, LANES), F32)],
        scratch_shapes=[
            pltpu.VMEM((dilation, seq_len + 2 * SUB_Q, GROUP_WIDTH), BF16),
            pltpu.VMEM((dilation, seq_len // KEY_BLOCK + (0 if whole else 2), GROUP_WIDTH, KEY_BLOCK), BF16),
            pltpu.VMEM((2 if whole else 3, kw, LANES), F32),
            pltpu.VMEM((units, kw, LANES), F32),
            pltpu.VMEM((units, kw, LANES), BF16),
            pltpu.VMEM((units, LANES, LANES), F32),
            pltpu.VMEM((2, units, SUBLANES, LANES), F32)],
        compiler_params=pltpu.CompilerParams(dimension_semantics=("arbitrary",),
                                             vmem_limit_bytes=VMEM_LIMIT),
        name=f"attn_d{dilation}",
    )(qkv, qkv, qkv)


CONV_PAD = 16
CONV_ROWS = 128
CONV_CHUNKS = CONV_CH // LANES


def _conv_stage(u_ref, prev_ref, next_ref, has_prev, has_next, pad_ref):
    for c in range(CONV_CHUNKS):
        cols = slice(c * LANES, (c + 1) * LANES)
        pad_ref[c, 0:CONV_PAD, :] = jnp.where(has_prev, prev_ref[:, cols], 0.0)
        pad_ref[c, CONV_PAD:CONV_PAD + ROW_TILE, :] = u_ref[:, cols]
        pad_ref[c, CONV_PAD + ROW_TILE:2 * CONV_PAD + ROW_TILE, :] = jnp.where(
            has_next, next_ref[:, cols], 0.0)


def _conv_ln_rows(r0, w_ref, b_ref, lnw_ref, lnb_ref, out_ref, pad_ref, conv_ref):
    chunks = [slice(c * LANES, (c + 1) * LANES) for c in range(CONV_CHUNKS)]
    first_tap = CONV_PAD - (CONV_WIDTH - 1) // 2
    half_rows = CONV_ROWS // 2
    for c, cols in enumerate(chunks):
        for parity in range(2):
            acc = jnp.zeros((half_rows, LANES), F32) + b_ref[:, cols]
            for t in range(CONV_WIDTH):
                rows = pl.ds(r0 + parity + first_tap + t, half_rows, stride=2)
                acc = acc + pad_ref[c, rows, :] * w_ref[t:t + 1, cols]
            conv_ref[c, pl.ds(r0 + parity, half_rows, stride=2), :] = acc
    ys = [conv_ref[c, r0:r0 + CONV_ROWS, :] for c in range(CONV_CHUNKS)]
    mu = sum(jnp.sum(y, axis=-1, keepdims=True) for y in ys) * (1.0 / CONV_CH)
    cen = [y - mu for y in ys]
    var = sum(jnp.sum(t * t, axis=-1, keepdims=True) for t in cen) * (1.0 / CONV_CH)
    inv = lax.rsqrt(var + EPS)
    for c, cols in enumerate(chunks):
        y = cen[c] * inv * lnw_ref[:, cols] + lnb_ref[:, cols]
        out_ref[r0:r0 + CONV_ROWS, cols] = (y * jax.nn.sigmoid(y)).astype(BF16)
    return y[0:1, :]


def _zero_after(value):
    bits = value.astype(jnp.int32)
    return lax.shift_right_logical(lax.shift_right_logical(bits, 16), 16).astype(F32)


FF_CHUNK = 256


def _mix_ffn_kernel(x_ref, o0_ref, o1_ref, o2_ref, l0_ref, l1_ref, l2_ref, g_ref,
                    uc_ref, up_ref, un_ref, cw_ref, cb_ref, lnw_ref, lnb_ref,
                    wo_ref, wpw_ref, wout_ref, n2_ref, w1_ref, w2_ref,
                    out_ref, attn_ref, act_ref, perm_ref, pad_ref, conv_ref, uln_ref,
                    *, tiles_per_seq):
    t = pl.program_id(0)
    conv_pos = jnp.minimum(t, pl.num_programs(0) - 2) % tiles_per_seq
    write_slot = t % 2
    read_slot = 1 - write_slot

    @pl.when(t == 0)
    def _init():
        uln_ref[1] = jnp.zeros(uln_ref.shape[1:], BF16)

    _conv_stage(uc_ref, up_ref, un_ref, conv_pos > 0, conv_pos < tiles_per_seq - 1, pad_ref)

    def token_order(ref, group, slab, cols):
        dilation = DILATED_PATTERNS[group][1]
        if dilation == 1:
            return ref[0, :, cols].astype(F32)
        n = ROW_TILE // dilation
        for r in range(dilation):
            perm_ref[slab, pl.ds(r, n, stride=dilation), :] = ref[r, :, cols].astype(F32)
        return perm_ref[slab]

    o_refs = (o0_ref, o1_ref, o2_ref)
    all_lanes = slice(0, LANES)
    n_slabs = GROUP_CHUNKS + 1
    lse = [token_order(ref, g, g * n_slabs + GROUP_CHUNKS, all_lanes)
           for g, ref in enumerate((l0_ref, l1_ref, l2_ref))]
    m = jnp.maximum(jnp.maximum(lse[0], lse[1]), lse[2])
    e = [jnp.exp(l - m) for l in lse]
    inv = 1.0 / (e[0] + e[1] + e[2])
    weights = [(e_g * inv).astype(BF16) for e_g in e]
    src_lane = lax.broadcasted_iota(jnp.int32, (LANES, LANES), 0)
    first_head = lax.broadcasted_iota(jnp.int32, (LANES, LANES), 1) < HEAD_DIM
    for c in range(GROUP_CHUNKS):
        cols = slice(c * LANES, (c + 1) * LANES)
        lane_b = LSE_LANES * c
        expand = jnp.where(src_lane == jnp.where(first_head, lane_b + HEAD_DIM, lane_b),
                           1.0, 0.0).astype(BF16)
        acc = jnp.zeros((ROW_TILE, LANES), F32)
        for g in range(N_GROUPS):
            acc = acc + _dot(weights[g], expand) * token_order(o_refs[g], g, g * n_slabs + c, cols)
        attn_ref[:, cols] = acc.astype(BF16)

    y_a = _dot(attn_ref[...], wo_ref[...])
    y_b = _dot(uln_ref[read_slot], wpw_ref[...])
    z = g_ref[:, 0:D_MODEL] * y_a + g_ref[:, D_MODEL:2 * D_MODEL] * y_b
    x1 = x_ref[...] + _dot(z.astype(BF16), wout_ref[...])

    ms = jnp.mean(x1 * x1, axis=-1, keepdims=True)
    h2 = (x1 * lax.rsqrt(ms + EPS) * n2_ref[...]).astype(BF16)
    ff_chunks = range(0, D_FF, FF_CHUNK)
    conv_blocks = list(range(0, ROW_TILE, CONV_ROWS))
    conv_every = len(ff_chunks) // len(conv_blocks)
    pending = None
    for idx, n in enumerate(ff_chunks):
        gt = _dot(h2, w1_ref[:, n:n + FF_CHUNK])
        up = _dot(h2, w1_ref[:, D_FF + n:D_FF + n + FF_CHUNK])
        act = gt * jax.nn.sigmoid(gt) * up
        if pending is not None and idx % conv_every == conv_every - 1:
            act = act + jnp.concatenate([_zero_after(pending)] * (FF_CHUNK // LANES), axis=1)
            pending = None
        act_ref[:, n:n + FF_CHUNK] = act.astype(BF16)
        if idx % conv_every == 0 and idx // conv_every < len(conv_blocks):
            pending = _conv_ln_rows(conv_blocks[idx // conv_every], cw_ref, cb_ref, lnw_ref,
                                    lnb_ref, uln_ref.at[write_slot], pad_ref, conv_ref)
    out_ref[...] = x1 + _dot(act_ref[...], w2_ref[...])


def _mix_ffn_call(x3, o, lse, gates, u, conv_w, conv_b, ln_w, ln_b, wo, wpw, wout, n2, w1, w2):
    bsz, s, _ = x3.shape
    dilations = [d for _, d in DILATED_PATTERNS]
    tiles_per_seq = s // ROW_TILE
    n_tiles = bsz * tiles_per_seq
    halo_blocks = ROW_TILE // CONV_PAD

    def finish_tile(t):
        tile = jnp.maximum(t - 1, 0)
        return tile // tiles_per_seq, tile % tiles_per_seq

    def conv_tile(t):
        tile = jnp.minimum(t, n_tiles - 1)
        return tile // tiles_per_seq, tile % tiles_per_seq

    def row(width):
        return pl.BlockSpec((None, ROW_TILE, width), lambda t: (*finish_tile(t), 0))

    def group(dilation, width):
        return pl.BlockSpec((None, dilation, ROW_TILE // dilation, width),
                            lambda t: (finish_tile(t)[0], 0, finish_tile(t)[1], 0))

    def halo(offset):
        def index(t):
            b, i = conv_tile(t)
            block = i * halo_blocks - 1 if offset < 0 else (i + 1) * halo_blocks
            return b, jnp.clip(block, 0, s // CONV_PAD - 1), 0
        return pl.BlockSpec((None, CONV_PAD, CONV_CH), index)

    return pl.pallas_call(
        functools.partial(_mix_ffn_kernel, tiles_per_seq=tiles_per_seq),
        grid=(n_tiles + 1,),
        in_specs=[row(D_MODEL)]
        + [group(d, GROUP_WIDTH) for d in dilations]
        + [group(d, LANES) for d in dilations]
        + [row(2 * D_MODEL),
           pl.BlockSpec((None, ROW_TILE, CONV_CH), lambda t: (*conv_tile(t), 0)), halo(-1), halo(1),
           _resident(conv_w.shape), _resident(conv_b.shape), _resident(ln_w.shape),
           _resident(ln_b.shape),
           _resident(wo.shape), _resident(wpw.shape),
           _resident(wout.shape), _resident(n2.shape), _resident(w1.shape), _resident(w2.shape)],
        out_specs=row(D_MODEL),
        out_shape=jax.ShapeDtypeStruct((bsz, s, D_MODEL), F32),
        scratch_shapes=[pltpu.VMEM((ROW_TILE, GROUP_WIDTH), BF16),
                        pltpu.VMEM((ROW_TILE, D_FF), BF16),
                        pltpu.VMEM((N_GROUPS * (GROUP_CHUNKS + 1), ROW_TILE, LANES), F32),
                        pltpu.VMEM((CONV_CHUNKS, ROW_TILE + 2 * CONV_PAD, LANES), F32),
                        pltpu.VMEM((CONV_CHUNKS, ROW_TILE, LANES), F32),
                        pltpu.VMEM((2, ROW_TILE, CONV_CH), BF16)],
        compiler_params=pltpu.CompilerParams(dimension_semantics=("arbitrary",),
                                             vmem_limit_bytes=MIX_VMEM_LIMIT),
        name="mix_ffn",
    )(x3, *o, *lse, gates, u, u, u, conv_w, conv_b, ln_w, ln_b, wo, wpw, wout, n2, w1, w2)


def _rope_table(positions):
    inv_freq = ROPE_THETA ** (-jnp.arange(0, ROT_DIM, 2, dtype=F32) / ROT_DIM)
    ang = positions.astype(F32)[:, None, :] * inv_freq[None, :, None]
    rest = jnp.zeros((ang.shape[0], HEAD_DIM - ROT_DIM, ang.shape[2]), F32)
    return jnp.concatenate([jnp.cos(ang), jnp.sin(ang), rest] * (LANES // HEAD_DIM), axis=1)


def kernel(x, positions, norm1_w, w_in, b_gate, q_norm_w, k_norm_w, w_o_attn, conv_w, conv_b,
           conv_ln_w, conv_ln_b, w_pw_conv, w_out, norm2_w, w_ffn_in, w_ffn_out):
    d = x.shape[-1]
    depth = norm1_w.shape[0]
    rot = _rope_table(positions)
    pair = lambda w: jnp.tile(w.astype(F32), LANES // HEAD_DIM).reshape(1, LANES)

    for l in range(depth):
        qkv0, qkv1, qkv2, u, gates = _proj_call(
            x, norm1_w[l].reshape(1, d), w_in[l].astype(BF16), rot,
            pair(q_norm_w[l]), pair(k_norm_w[l]), b_gate[l].reshape(1, 2 * d))
        outs, lses = zip(*[_attn_call(qkv) for qkv in (qkv0, qkv1, qkv2)])
        x = _mix_ffn_call(
            x, outs, lses, gates, u, conv_w[l], conv_b[l].reshape(1, CONV_CH),
            conv_ln_w[l].reshape(1, CONV_CH), conv_ln_b[l].reshape(1, CONV_CH),
            w_o_attn[l].astype(BF16), w_pw_conv[l].astype(BF16), w_out[l].astype(BF16),
            norm2_w[l].reshape(1, d), w_ffn_in[l].astype(BF16), w_ffn_out[l].astype(BF16))
    return x
```

```python
import functools

import jax
import jax.numpy as jnp
from jax import lax
from jax.experimental import pallas as pl
from jax.experimental.pallas import tpu as pltpu

D_MODEL = 1024
HEAD_DIM = 64
N_SLOT_HEADS = 8
DILATED_PATTERNS = ((128, 1), (512, 4), (2048, 16))
N_GROUPS = len(DILATED_PATTERNS)
GROUP_WIDTH = N_SLOT_HEADS * HEAD_DIM
QKV_WIDTH = N_GROUPS * GROUP_WIDTH
ROPE_THETA = 500000.0
ROT_DIM = HEAD_DIM // 4
CONV_CH = D_MODEL // 2
CONV_WIDTH = 31
D_FF = 2816
EPS = 1e-6
NEG_INF = -1e30

LANES = 128
SUBLANES = 8
SUB_Q = 64
KEY_BLOCK = 128
HALF_SPAN = 64
LSE_LANES = 16
LOG2E = 1.4426950408889634
LN2 = 0.6931471805599453
ROW_TILE = 512
EPI_ROWS = 128
GROUP_CHUNKS = GROUP_WIDTH // LANES
VMEM_LIMIT = 56 * 1024 * 1024
MIX_VMEM_LIMIT = 60 * 1024 * 1024

F32 = jnp.float32
BF16 = jnp.bfloat16


def _dot(a, b):
    return jnp.dot(a, b, preferred_element_type=F32)


def _split_dot(a_f32, b_bf16):
    hi = a_f32.astype(BF16)
    lo = (a_f32 - hi.astype(F32)).astype(BF16)
    return _dot(hi, b_bf16) + _dot(lo, b_bf16)


def _resident(shape):
    nd = len(shape)
    return pl.BlockSpec(shape, lambda *_: (0,) * nd, pipeline_mode=pl.Buffered(1))


def _row_block(width):
    return pl.BlockSpec((None, ROW_TILE, width), lambda b, i: (b, i, 0))


def _group_block(dilation, width, lead=()):
    n_lead = len(lead)
    return pl.BlockSpec(lead + (None, dilation, ROW_TILE // dilation, width),
                        lambda b, i: (0,) * n_lead + (b, 0, i, 0))


def _proj_kernel(x_ref, n1_ref, w_ref, rot_ref, qn_ref, kn_ref,
                 bg_ref, qkv0_ref, qkv1_ref, qkv2_ref, u_ref, g_ref, h_ref, perm_ref, acc_ref,
                 rope_ref):
    xt = x_ref[...]
    ms = jnp.mean(xt * xt, axis=-1, keepdims=True)
    h_ref[...] = (xt * lax.rsqrt(ms + EPS) * n1_ref[...]).astype(BF16)

    row_blocks = [slice(r, r + EPI_ROWS) for r in range(0, ROW_TILE, EPI_ROWS)]
    chunks = [slice(c * LANES, (c + 1) * LANES) for c in range(GROUP_CHUNKS)]

    half_rot = ROT_DIM // 2
    head_lane = lax.broadcasted_iota(jnp.int32, (1, LANES), 1) % HEAD_DIM
    first_half = head_lane < half_rot
    for rows in row_blocks:
        tab = rot_ref[:, rows].T
        rope_ref[0, rows, :] = jnp.where(first_half, tab,
                                         jnp.where(head_lane < ROT_DIM, pltpu.roll(tab, half_rot, 1), 1.0))
        rope_ref[1, rows, :] = jnp.where(first_half, -pltpu.roll(tab, LANES - half_rot, 1),
                                         jnp.where(head_lane < ROT_DIM, tab, 0.0))

    n_proj = [0]

    def proj(c0):
        slot = n_proj[0] % acc_ref.shape[0]
        n_proj[0] += 1
        acc_ref[slot] = _dot(h_ref[...], w_ref[:, c0:c0 + GROUP_WIDTH])
        return acc_ref.at[slot]

    same_head = (lax.broadcasted_iota(jnp.int32, (LANES, LANES), 0) // HEAD_DIM
                 == lax.broadcasted_iota(jnp.int32, (LANES, LANES), 1) // HEAD_DIM)
    seg = jnp.where(same_head, 1.0 / HEAD_DIM, 0.0).astype(BF16)

    def head_norm_rope(t, nw, rows):
        ms = _dot((t * t).astype(BF16), seg)
        y = t * lax.rsqrt(ms + EPS) * nw
        partner = jnp.where(first_half, pltpu.roll(y, LANES - half_rot, 1), pltpu.roll(y, half_rot, 1))
        return y * rope_ref[0, rows, :] + partner * rope_ref[1, rows, :]

    def emit(kind, group, acc, fn):
        out_ref = (qkv0_ref, qkv1_ref, qkv2_ref)[group]
        dilation = DILATED_PATTERNS[group][1]
        n = ROW_TILE // dilation
        for c, cols in enumerate(chunks):
            for rows in row_blocks:
                val = fn(acc[rows, cols], rows)
                if dilation == 1:
                    out_ref[kind, 0, rows, cols] = val.astype(BF16)
                else:
                    perm_ref[c, rows, :] = val
            for r in range(dilation if dilation > 1 else 0):
                out_ref[kind, r, :, cols] = perm_ref[c, pl.ds(r, n, stride=dilation), :].astype(BF16)

    def qk_block(kind, j):
        nw_ref, scale = ((qn_ref, HEAD_DIM ** -0.5 * LOG2E), (kn_ref, 1.0))[kind]
        nw = nw_ref[...] * scale
        acc = proj(kind * QKV_WIDTH + j * GROUP_WIDTH)
        emit(kind, j, acc, lambda t, rows: head_norm_rope(t, nw, rows))

    def v_block(j):
        emit(2, j, proj(2 * QKV_WIDTH + j * GROUP_WIDTH), lambda t, rows: t)

    conv0 = 3 * QKV_WIDTH
    gate0 = conv0 + 2 * CONV_CH

    def gate_block(j):
        lo = j * GROUP_WIDTH
        acc = proj(gate0 + lo)
        for rows in row_blocks:
            g_ref[rows, lo:lo + GROUP_WIDTH] = jax.nn.sigmoid(
                acc[rows, :] + bg_ref[:, lo:lo + GROUP_WIDTH])

    for j in range(N_GROUPS):
        qk_block(0, j)
        v_block(j)
        qk_block(1, j)
        gate_block(j)
    a = proj(conv0)
    b = proj(conv0 + CONV_CH)
    for rows in row_blocks:
        u_ref[rows, :] = a[rows, :] * jax.nn.sigmoid(b[rows, :])
    gate_block(N_GROUPS)


def _proj_call(x3, n1, w_in, rot, qn, kn, bg):
    bsz, s, _ = x3.shape
    in_width = w_in.shape[1]
    qkv_shapes = [jax.ShapeDtypeStruct((3, bsz, d, s // d, GROUP_WIDTH), BF16)
                  for _, d in DILATED_PATTERNS]
    return pl.pallas_call(
        _proj_kernel,
        grid=(bsz, s // ROW_TILE),
        in_specs=[_row_block(D_MODEL), _resident((1, D_MODEL)), _resident((D_MODEL, in_width)),
                  pl.BlockSpec((None, LANES, ROW_TILE), lambda b, i: (b, 0, i)),
                  _resident((1, LANES)), _resident((1, LANES)),
                  _resident((1, 2 * D_MODEL))],
        out_specs=[_group_block(d, GROUP_WIDTH, lead=(3,)) for _, d in DILATED_PATTERNS]
        + [_row_block(CONV_CH), _row_block(2 * D_MODEL)],
        out_shape=qkv_shapes + [jax.ShapeDtypeStruct((bsz, s, CONV_CH), F32),
                                jax.ShapeDtypeStruct((bsz, s, 2 * D_MODEL), F32)],
        scratch_shapes=[pltpu.VMEM((ROW_TILE, D_MODEL), BF16),
                        pltpu.VMEM((GROUP_CHUNKS, ROW_TILE, LANES), F32),
                        pltpu.VMEM((3, ROW_TILE, GROUP_WIDTH), F32),
                        pltpu.VMEM((2, ROW_TILE, LANES), F32)],
        compiler_params=pltpu.CompilerParams(dimension_semantics=("arbitrary", "arbitrary"),
                                             vmem_limit_bytes=VMEM_LIMIT),
        name="proj",
    )(x3, n1, w_in, rot, qn, kn, bg)


def _attn_kernel(q_ref, k_ref, v_ref, o_ref, lse_ref, kpad_ref, vt_ref, bias_ref, s_ref, p_ref,
                 ot_ref, stat_ref):
    n_seq, seq_len, _ = q_ref.shape
    whole = seq_len == KEY_BLOCK
    kw = KEY_BLOCK if whole else 3 * SUB_Q
    n_key_blocks = seq_len // KEY_BLOCK
    n_blocks = n_seq * n_key_blocks
    vt_pad = 0 if whole else 1
    last_sub = seq_len // SUB_Q - 1
    subs = KEY_BLOCK // SUB_Q
    lane = lax.broadcasted_iota(jnp.int32, (1, LANES), 1)
    first_head = lane < HEAD_DIM
    block_diag = (lax.broadcasted_iota(jnp.int32, (LANES, LANES), 0) < HEAD_DIM) == first_head

    @pl.when(pl.program_id(0) == 0)
    def _init():
        row = lax.broadcasted_iota(jnp.int32, (kw, LANES), 0)
        lq = lax.broadcasted_iota(jnp.int32, (kw, LANES), 1) % SUB_Q
        as_bias = lambda valid: jnp.where(valid, 0.0, NEG_INF).astype(F32)
        if whole:
            for case in range(subs):
                bias_ref[case] = as_bias(jnp.abs(lq + case * SUB_Q - row) <= HALF_SPAN)
        else:
            band = jnp.abs(lq + SUB_Q - row) <= HALF_SPAN
            bias_ref[0] = as_bias(band & (row >= SUB_Q))
            bias_ref[1] = as_bias(band)
            bias_ref[2] = as_bias(band & (row < 2 * SUB_Q))
            zeros = jnp.zeros((SUB_Q, GROUP_WIDTH), BF16)
            zero_block = jnp.zeros((GROUP_WIDTH, KEY_BLOCK), BF16)
            for r in range(n_seq):
                kpad_ref[r, 0:SUB_Q, :] = zeros
                kpad_ref[r, SUB_Q + seq_len:2 * SUB_Q + seq_len, :] = zeros
                vt_ref[r, 0] = zero_block
                vt_ref[r, n_key_blocks + 1] = zero_block

    def split(j):
        return j // n_key_blocks, j % n_key_blocks

    if not whole:
        for r in range(n_seq):
            kpad_ref[r, SUB_Q:SUB_Q + seq_len, :] = k_ref[r]

    def stage_vt(j):
        r, b = split(j)
        rows = pl.ds(pl.multiple_of(b * KEY_BLOCK, KEY_BLOCK), KEY_BLOCK)
        for hp in range(GROUP_CHUNKS):
            cols = slice(hp * LANES, (hp + 1) * LANES)
            vt_ref[r, b + vt_pad, cols, :] = v_ref[r, rows, cols].astype(F32).T.astype(BF16)

    def stage1_qk(j):
        r, i = split(j)
        for sub in range(subs):
            q0 = pl.multiple_of((subs * i + sub) * SUB_Q, SUB_Q)
            for hp in range(GROUP_CHUNKS):
                cols = slice(hp * LANES, (hp + 1) * LANES)
                qb = q_ref[r, pl.ds(q0, SUB_Q), cols]
                zq = jnp.zeros_like(qb)
                qv = jnp.concatenate([jnp.where(first_head, qb, zq),
                                      jnp.where(first_head, zq, qb)], axis=0)
                kwin = k_ref[r, :, cols] if whole else kpad_ref[r, pl.ds(q0, kw), cols]
                s_ref[sub * GROUP_CHUNKS + hp] = lax.dot_general(
                    kwin, qv, (((1,), (1,)), ((), ())), preferred_element_type=F32)

    def stage2_softmax(j):
        _, i = split(j)
        for sub in range(subs):
            t = subs * i + sub
            if whole:
                bias = bias_ref[sub]
            else:
                bias = bias_ref[jnp.where(t == 0, 0, jnp.where(t == last_sub, 2, 1))]
            for hp in range(GROUP_CHUNKS):
                unit = sub * GROUP_CHUNKS + hp
                if whole:
                    s = s_ref[unit] + bias
                else:
                    s = jnp.concatenate([s_ref[unit, 0:SUB_Q, :] + bias[0:SUB_Q],
                                         s_ref[unit, SUB_Q:2 * SUB_Q, :],
                                         s_ref[unit, 2 * SUB_Q:kw, :] + bias[2 * SUB_Q:kw]], axis=0)
                m = jnp.max(s, axis=0, keepdims=True)
                p = jnp.exp2(s - m)
                den = jnp.sum(p, axis=0, keepdims=True)
                p_ref[unit] = p.astype(BF16)
                stat_ref[0, unit, 0:1, :] = 1.0 / den
                stat_ref[0, unit, 1:2, :] = (m + jnp.log2(den)) * LN2

    def stage3_pv(j):
        r, i = split(j)
        for sub in range(subs):
            for hp in range(GROUP_CHUNKS):
                unit = sub * GROUP_CHUNKS + hp
                cols = slice(hp * LANES, (hp + 1) * LANES)
                pb = p_ref[unit]
                if whole:
                    ot_ref[unit] = _dot(vt_ref[r, 0, cols, :], pb)
                else:
                    first = i + sub
                    vt2 = jnp.concatenate([vt_ref[r, first, cols, :], vt_ref[r, first + 1, cols, :]], axis=1)
                    zp = jnp.zeros((SUB_Q, LANES), BF16)
                    ot_ref[unit] = _dot(vt2, jnp.concatenate([zp, pb] if sub == 0 else [pb, zp], axis=0))
                stat_ref[1, unit] = stat_ref[0, unit]

    def stage4_finish(j):
        r, i = split(j)
        for sub in range(subs):
            q0 = pl.multiple_of((subs * i + sub) * SUB_Q, SUB_Q)
            lse_tile = jnp.zeros((SUB_Q, LANES), F32)
            for hp in range(GROUP_CHUNKS):
                unit = sub * GROUP_CHUNKS + hp
                cols = slice(hp * LANES, (hp + 1) * LANES)
                tile = jnp.where(block_diag, ot_ref[unit] * stat_ref[1, unit, 0:1, :],
                                 stat_ref[1, unit, 1:2, :]).T
                top, bot = tile[0:SUB_Q], tile[SUB_Q:2 * SUB_Q]
                o_ref[r, pl.ds(q0, SUB_Q), cols] = jnp.where(first_head, top, bot).astype(BF16)
                lse_tile = jnp.where((lane % HEAD_DIM) // LSE_LANES == hp,
                                     jnp.where(first_head, bot, top), lse_tile)
            lse_ref[r, pl.ds(q0, SUB_Q), :] = lse_tile

    def trip(j, stages=(4, 3, 2, 1)):
        if 4 in stages:
            stage4_finish(j - 2)
        if 3 in stages:
            stage3_pv(j - 1)
        if 2 in stages:
            stage2_softmax(j)
        if 1 in stages:
            stage1_qk(j + 1)
            stage_vt(j + 1)

    trip(-1, stages=(1,))
    trip(0, stages=(2, 1))
    trip(1, stages=(3, 2, 1))

    def steady(j, c):
        trip(j)
        return c

    lax.fori_loop(2, n_blocks - 1, steady, 0)
    trip(n_blocks - 1, stages=(4, 3, 2))
    trip(n_blocks, stages=(4, 3))
    trip(n_blocks + 1, stages=(4,))


def _attn_call(qkv):
    _, bsz, dilation, seq_len, _ = qkv.shape
    whole = seq_len == KEY_BLOCK
    kw = KEY_BLOCK if whole else 3 * SUB_Q
    units = (KEY_BLOCK // SUB_Q) * GROUP_CHUNKS
    in_spec = lambda kind: pl.BlockSpec((None, None, dilation, seq_len, GROUP_WIDTH),
                                        lambda b: (kind, b, 0, 0, 0))
    out_spec = lambda w: pl.BlockSpec((None, dilation, seq_len, w), lambda b: (b, 0, 0, 0))
    return pl.pallas_call(
        _attn_kernel,
        grid=(bsz,),
        in_specs=[in_spec(0), in_spec(1), in_spec(2)],
        out_specs=[out_spec(GROUP_WIDTH), out_spec(LANES)],
        out_shape=[jax.ShapeDtypeStruct((bsz, dilation, seq_len, GROUP_WIDTH), BF16),
                   jax.ShapeDtypeStruct((bsz, dilation, seq_len, LANES), F32)],
        scratch_shapes=[
            pltpu.VMEM((dilation, seq_len + 2 * SUB_Q, GROUP_WIDTH), BF16),
            pltpu.VMEM((dilation, seq_len // KEY_BLOCK + (0 if whole else 2), GROUP_WIDTH, KEY_BLOCK), BF16),
            pltpu.VMEM((2 if whole else 3, kw, LANES), F32),
            pltpu.VMEM((units, kw, LANES), F32),
            pltpu.VMEM((units, kw, LANES), BF16),
            pltpu.VMEM((units, LANES, LANES), F32),
            pltpu.VMEM((2, units, SUBLANES, LANES), F32)],
        compiler_params=pltpu.CompilerParams(dimension_semantics=("arbitrary",),
                                             vmem_limit_bytes=VMEM_LIMIT),
        name=f"attn_d{dilation}",
    )(qkv, qkv, qkv)


CONV_PAD = 16
CONV_ROWS = 128
CONV_CHUNKS = CONV_CH // LANES


def _conv_stage(u_ref, prev_ref, next_ref, has_prev, has_next, pad_ref):
    for c in range(CONV_CHUNKS):
        cols = slice(c * LANES, (c + 1) * LANES)
        pad_ref[c, 0:CONV_PAD, :] = jnp.where(has_prev, prev_ref[:, cols], 0.0)
        pad_ref[c, CONV_PAD:CONV_PAD + ROW_TILE, :] = u_ref[:, cols]
        pad_ref[c, CONV_PAD + ROW_TILE:2 * CONV_PAD + ROW_TILE, :] = jnp.where(
            has_next, next_ref[:, cols], 0.0)


def _conv_ln_rows(r0, w_ref, b_ref, lnw_ref, lnb_ref, out_ref, pad_ref, conv_ref):
    chunks = [slice(c * LANES, (c + 1) * LANES) for c in range(CONV_CHUNKS)]
    first_tap = CONV_PAD - (CONV_WIDTH - 1) // 2
    half_rows = CONV_ROWS // 2
    for c, cols in enumerate(chunks):
        for parity in range(2):
            acc = jnp.zeros((half_rows, LANES), F32) + b_ref[:, cols]
            for t in range(CONV_WIDTH):
                rows = pl.ds(r0 + parity + first_tap + t, half_rows, stride=2)
                acc = acc + pad_ref[c, rows, :] * w_ref[t:t + 1, cols]
            conv_ref[c, pl.ds(r0 + parity, half_rows, stride=2), :] = acc
    ys = [conv_ref[c, r0:r0 + CONV_ROWS, :] for c in range(CONV_CHUNKS)]
    mu = sum(jnp.sum(y, axis=-1, keepdims=True) for y in ys) * (1.0 / CONV_CH)
    cen = [y - mu for y in ys]
    var = sum(jnp.sum(t * t, axis=-1, keepdims=True) for t in cen) * (1.0 / CONV_CH)
    inv = lax.rsqrt(var + EPS)
    for c, cols in enumerate(chunks):
        y = cen[c] * inv * lnw_ref[:, cols] + lnb_ref[:, cols]
        out_ref[r0:r0 + CONV_ROWS, cols] = (y * jax.nn.sigmoid(y)).astype(BF16)
    return y[0:1, :]


def _zero_after(value):
    bits = value.astype(jnp.int32)
    return lax.shift_right_logical(lax.shift_right_logical(bits, 16), 16).astype(F32)


FF_CHUNK = 256


def _mix_ffn_kernel(x_ref, o0_ref, o1_ref, o2_ref, l0_ref, l1_ref, l2_ref, g_ref,
                    uc_ref, up_ref, un_ref, cw_ref, cb_ref, lnw_ref, lnb_ref,
                    wo_ref, wpw_ref, wout_ref, n2_ref, w1_ref, w2_ref,
                    out_ref, attn_ref, act_ref, perm_ref, pad_ref, conv_ref, uln_ref,
                    *, tiles_per_seq):
    t = pl.program_id(0)
    conv_pos = jnp.minimum(t, pl.num_programs(0) - 2) % tiles_per_seq
    write_slot = t % 2
    read_slot = 1 - write_slot

    @pl.when(t == 0)
    def _init():
        uln_ref[1] = jnp.zeros(uln_ref.shape[1:], BF16)

    _conv_stage(uc_ref, up_ref, un_ref, conv_pos > 0, conv_pos < tiles_per_seq - 1, pad_ref)

    def token_order(ref, group, slab, cols):
        dilation = DILATED_PATTERNS[group][1]
        if dilation == 1:
            return ref[0, :, cols].astype(F32)
        n = ROW_TILE // dilation
        for r in range(dilation):
            perm_ref[slab, pl.ds(r, n, stride=dilation), :] = ref[r, :, cols].astype(F32)
        return perm_ref[slab]

    o_refs = (o0_ref, o1_ref, o2_ref)
    all_lanes = slice(0, LANES)
    n_slabs = GROUP_CHUNKS + 1
    lse = [token_order(ref, g, g * n_slabs + GROUP_CHUNKS, all_lanes)
           for g, ref in enumerate((l0_ref, l1_ref, l2_ref))]
    m = jnp.maximum(jnp.maximum(lse[0], lse[1]), lse[2])
    e = [jnp.exp(l - m) for l in lse]
    inv = 1.0 / (e[0] + e[1] + e[2])
    weights = [(e_g * inv).astype(BF16) for e_g in e]
    src_lane = lax.broadcasted_iota(jnp.int32, (LANES, LANES), 0)
    first_head = lax.broadcasted_iota(jnp.int32, (LANES, LANES), 1) < HEAD_DIM
    for c in range(GROUP_CHUNKS):
        cols = slice(c * LANES, (c + 1) * LANES)
        lane_b = LSE_LANES * c
        expand = jnp.where(src_lane == jnp.where(first_head, lane_b + HEAD_DIM, lane_b),
                           1.0, 0.0).astype(BF16)
        acc = jnp.zeros((ROW_TILE, LANES), F32)
        for g in range(N_GROUPS):
            acc = acc + _dot(weights[g], expand) * token_order(o_refs[g], g, g * n_slabs + c, cols)
        attn_ref[:, cols] = acc.astype(BF16)

    y_a = _dot(attn_ref[...], wo_ref[...])
    y_b = _dot(uln_ref[read_slot], wpw_ref[...])
    z = g_ref[:, 0:D_MODEL] * y_a + g_ref[:, D_MODEL:2 * D_MODEL] * y_b
    x1 = x_ref[...] + _dot(z.astype(BF16), wout_ref[...])

    ms = jnp.mean(x1 * x1, axis=-1, keepdims=True)
    h2 = (x1 * lax.rsqrt(ms + EPS) * n2_ref[...]).astype(BF16)
    ff_chunks = range(0, D_FF, FF_CHUNK)
    conv_blocks = list(range(0, ROW_TILE, CONV_ROWS))
    conv_every = len(ff_chunks) // len(conv_blocks)
    pending = None
    for idx, n in enumerate(ff_chunks):
        gt = _dot(h2, w1_ref[:, n:n + FF_CHUNK])
        up = _dot(h2, w1_ref[:, D_FF + n:D_FF + n + FF_CHUNK])
        act = gt * jax.nn.sigmoid(gt) * up
        if pending is not None and idx % conv_every == conv_every - 1:
            act = act + jnp.concatenate([_zero_after(pending)] * (FF_CHUNK // LANES), axis=1)
            pending = None
        act_ref[:, n:n + FF_CHUNK] = act.astype(BF16)
        if idx % conv_every == 0 and idx // conv_every < len(conv_blocks):
            pending = _conv_ln_rows(conv_blocks[idx // conv_every], cw_ref, cb_ref, lnw_ref,
                                    lnb_ref, uln_ref.at[write_slot], pad_ref, conv_ref)
    out_ref[...] = x1 + _dot(act_ref[...], w2_ref[...])


def _mix_ffn_call(x3, o, lse, gates, u, conv_w, conv_b, ln_w, ln_b, wo, wpw, wout, n2, w1, w2):
    bsz, s, _ = x3.shape
    dilations = [d for _, d in DILATED_PATTERNS]
    tiles_per_seq = s // ROW_TILE
    n_tiles = bsz * tiles_per_seq
    halo_blocks = ROW_TILE // CONV_PAD

    def finish_tile(t):
        tile = jnp.maximum(t - 1, 0)
        return tile // tiles_per_seq, tile % tiles_per_seq

    def conv_tile(t):
        tile = jnp.minimum(t, n_tiles - 1)
        return tile // tiles_per_seq, tile % tiles_per_seq

    def row(width):
        return pl.BlockSpec((None, ROW_TILE, width), lambda t: (*finish_tile(t), 0))

    def group(dilation, width):
        return pl.BlockSpec((None, dilation, ROW_TILE // dilation, width),
                            lambda t: (finish_tile(t)[0], 0, finish_tile(t)[1], 0))

    def halo(offset):
        def index(t):
            b, i = conv_tile(t)
            block = i * halo_blocks - 1 if offset < 0 else (i + 1) * halo_blocks
            return b, jnp.clip(block, 0, s // CONV_PAD - 1), 0
        return pl.BlockSpec((None, CONV_PAD, CONV_CH), index)

    return pl.pallas_call(
        functools.partial(_mix_ffn_kernel, tiles_per_seq=tiles_per_seq),
        grid=(n_tiles + 1,),
        in_specs=[row(D_MODEL)]
        + [group(d, GROUP_WIDTH) for d in dilations]
        + [group(d, LANES) for d in dilations]
        + [row(2 * D_MODEL),
           pl.BlockSpec((None, ROW_TILE, CONV_CH), lambda t: (*conv_tile(t), 0)), halo(-1), halo(1),
           _resident(conv_w.shape), _resident(conv_b.shape), _resident(ln_w.shape),
           _resident(ln_b.shape),
           _resident(wo.shape), _resident(wpw.shape),
           _resident(wout.shape), _resident(n2.shape), _resident(w1.shape), _resident(w2.shape)],
        out_specs=row(D_MODEL),
        out_shape=jax.ShapeDtypeStruct((bsz, s, D_MODEL), F32),
        scratch_shapes=[pltpu.VMEM((ROW_TILE, GROUP_WIDTH), BF16),
                        pltpu.VMEM((ROW_TILE, D_FF), BF16),
                        pltpu.VMEM((N_GROUPS * (GROUP_CHUNKS + 1), ROW_TILE, LANES), F32),
                        pltpu.VMEM((CONV_CHUNKS, ROW_TILE + 2 * CONV_PAD, LANES), F32),
                        pltpu.VMEM((CONV_CHUNKS, ROW_TILE, LANES), F32),
                        pltpu.VMEM((2, ROW_TILE, CONV_CH), BF16)],
        compiler_params=pltpu.CompilerParams(dimension_semantics=("arbitrary",),
                                             vmem_limit_bytes=MIX_VMEM_LIMIT),
        name="mix_ffn",
    )(x3, *o, *lse, gates, u, u, u, conv_w, conv_b, ln_w, ln_b, wo, wpw, wout, n2, w1, w2)


def _rope_table(positions):
    inv_freq = ROPE_THETA ** (-jnp.arange(0, ROT_DIM, 2, dtype=F32) / ROT_DIM)
    ang = positions.astype(F32)[:, None, :] * inv_freq[None, :, None]
    rest = jnp.zeros((ang.shape[0], HEAD_DIM - ROT_DIM, ang.shape[2]), F32)
    return jnp.concatenate([jnp.cos(ang), jnp.sin(ang), rest] * (LANES // HEAD_DIM), axis=1)


def kernel(x, positions, norm1_w, w_in, b_gate, q_norm_w, k_norm_w, w_o_attn, conv_w, conv_b,
           conv_ln_w, conv_ln_b, w_pw_conv, w_out, norm2_w, w_ffn_in, w_ffn_out):
    d = x.shape[-1]
    depth = norm1_w.shape[0]
    rot = _rope_table(positions)
    pair = lambda w: jnp.tile(w.astype(F32), LANES // HEAD_DIM).reshape(1, LANES)

    for l in range(depth):
        qkv0, qkv1, qkv2, u, gates = _proj_call(
            x, norm1_w[l].reshape(1, d), w_in[l].astype(BF16), rot,
            pair(q_norm_w[l]), pair(k_norm_w[l]), b_gate[l].reshape(1, 2 * d))
        outs, lses = zip(*[_attn_call(qkv) for qkv in (qkv0, qkv1, qkv2)])
        x = _mix_ffn_call(
            x, outs, lses, gates, u, conv_w[l], conv_b[l].reshape(1, CONV_CH),
            conv_ln_w[l].reshape(1, CONV_CH), conv_ln_b[l].reshape(1, CONV_CH),
            w_o_attn[l].astype(BF16), w_pw_conv[l].astype(BF16), w_out[l].astype(BF16),
            norm2_w[l].reshape(1, d), w_ffn_in[l].astype(BF16), w_ffn_out[l].astype(BF16))
    return x
```

```python
import functools

import jax
import jax.numpy as jnp
from jax import lax
from jax.experimental import pallas as pl
from jax.experimental.pallas import tpu as pltpu

D_MODEL = 1024
HEAD_DIM = 64
N_SLOT_HEADS = 8
DILATED_PATTERNS = ((128, 1), (512, 4), (2048, 16))
N_GROUPS = len(DILATED_PATTERNS)
GROUP_WIDTH = N_SLOT_HEADS * HEAD_DIM
QKV_WIDTH = N_GROUPS * GROUP_WIDTH
ROPE_THETA = 500000.0
ROT_DIM = HEAD_DIM // 4
CONV_CH = D_MODEL // 2
CONV_WIDTH = 31
D_FF = 2816
EPS = 1e-6
NEG_INF = -1e30

LANES = 128
SUBLANES = 8
SUB_Q = 64
KEY_BLOCK = 128
HALF_SPAN = 64
LSE_LANES = 16
LOG2E = 1.4426950408889634
LN2 = 0.6931471805599453
ROW_TILE = 512
EPI_ROWS = 256
GROUP_CHUNKS = GROUP_WIDTH // LANES
VMEM_LIMIT = 56 * 1024 * 1024
MIX_VMEM_LIMIT = 60 * 1024 * 1024

F32 = jnp.float32
BF16 = jnp.bfloat16


def _dot(a, b):
    return jnp.dot(a, b, preferred_element_type=F32)


def _split_dot(a_f32, b_bf16):
    hi = a_f32.astype(BF16)
    lo = (a_f32 - hi.astype(F32)).astype(BF16)
    return _dot(hi, b_bf16) + _dot(lo, b_bf16)


def _resident(shape):
    nd = len(shape)
    return pl.BlockSpec(shape, lambda *_: (0,) * nd, pipeline_mode=pl.Buffered(1))


def _row_block(width):
    return pl.BlockSpec((None, ROW_TILE, width), lambda b, i: (b, i, 0))


def _group_block(dilation, width, lead=()):
    n_lead = len(lead)
    return pl.BlockSpec(lead + (None, dilation, ROW_TILE // dilation, width),
                        lambda b, i: (0,) * n_lead + (b, 0, i, 0))


def _proj_kernel(x_ref, n1_ref, w_ref, rot_ref, qn_ref, kn_ref,
                 bg_ref, qkv0_ref, qkv1_ref, qkv2_ref, u_ref, g_ref, h_ref, perm_ref, acc_ref,
                 rope_ref):
    xt = x_ref[...]
    ms = jnp.mean(xt * xt, axis=-1, keepdims=True)
    h_ref[...] = (xt * lax.rsqrt(ms + EPS) * n1_ref[...]).astype(BF16)

    row_blocks = [slice(r, r + EPI_ROWS) for r in range(0, ROW_TILE, EPI_ROWS)]
    chunks = [slice(c * LANES, (c + 1) * LANES) for c in range(GROUP_CHUNKS)]

    half_rot = ROT_DIM // 2
    head_lane = lax.broadcasted_iota(jnp.int32, (1, LANES), 1) % HEAD_DIM
    first_half = head_lane < half_rot
    for rows in row_blocks:
        tab = rot_ref[:, rows].T
        rope_ref[0, rows, :] = jnp.where(first_half, tab,
                                         jnp.where(head_lane < ROT_DIM, pltpu.roll(tab, half_rot, 1), 1.0))
        rope_ref[1, rows, :] = jnp.where(first_half, -pltpu.roll(tab, LANES - half_rot, 1),
                                         jnp.where(head_lane < ROT_DIM, tab, 0.0))

    n_proj = [0]

    def proj(c0):
        slot = n_proj[0] % acc_ref.shape[0]
        n_proj[0] += 1
        acc_ref[slot] = _dot(h_ref[...], w_ref[:, c0:c0 + GROUP_WIDTH])
        return acc_ref.at[slot]

    same_head = (lax.broadcasted_iota(jnp.int32, (LANES, LANES), 0) // HEAD_DIM
                 == lax.broadcasted_iota(jnp.int32, (LANES, LANES), 1) // HEAD_DIM)
    seg = jnp.where(same_head, 1.0 / HEAD_DIM, 0.0).astype(BF16)

    def head_norm_rope(t, nw, rows):
        ms = _dot((t * t).astype(BF16), seg)
        y = t * lax.rsqrt(ms + EPS) * nw
        partner = jnp.where(first_half, pltpu.roll(y, LANES - half_rot, 1), pltpu.roll(y, half_rot, 1))
        return y * rope_ref[0, rows, :] + partner * rope_ref[1, rows, :]

    def emit(kind, group, acc, fn):
        out_ref = (qkv0_ref, qkv1_ref, qkv2_ref)[group]
        dilation = DILATED_PATTERNS[group][1]
        n = ROW_TILE // dilation
        for c, cols in enumerate(chunks):
            for rows in row_blocks:
                val = fn(acc[rows, cols], rows)
                if dilation == 1:
                    out_ref[kind, 0, rows, cols] = val.astype(BF16)
                else:
                    perm_ref[c, rows, :] = val
            for r in range(dilation if dilation > 1 else 0):
                out_ref[kind, r, :, cols] = perm_ref[c, pl.ds(r, n, stride=dilation), :].astype(BF16)

    def qk_block(kind, j):
        nw_ref, scale = ((qn_ref, HEAD_DIM ** -0.5 * LOG2E), (kn_ref, 1.0))[kind]
        nw = nw_ref[...] * scale
        acc = proj(kind * QKV_WIDTH + j * GROUP_WIDTH)
        emit(kind, j, acc, lambda t, rows: head_norm_rope(t, nw, rows))

    def v_block(j):
        emit(2, j, proj(2 * QKV_WIDTH + j * GROUP_WIDTH), lambda t, rows: t)

    conv0 = 3 * QKV_WIDTH
    gate0 = conv0 + 2 * CONV_CH

    def gate_block(j):
        lo = j * GROUP_WIDTH
        acc = proj(gate0 + lo)
        for rows in row_blocks:
            g_ref[rows, lo:lo + GROUP_WIDTH] = jax.nn.sigmoid(
                acc[rows, :] + bg_ref[:, lo:lo + GROUP_WIDTH])

    for j in range(N_GROUPS):
        qk_block(0, j)
        v_block(j)
        qk_block(1, j)
        gate_block(j)
    a = proj(conv0)
    b = proj(conv0 + CONV_CH)
    for rows in row_blocks:
        u_ref[rows, :] = a[rows, :] * jax.nn.sigmoid(b[rows, :])
    gate_block(N_GROUPS)


def _proj_call(x3, n1, w_in, rot, qn, kn, bg):
    bsz, s, _ = x3.shape
    in_width = w_in.shape[1]
    qkv_shapes = [jax.ShapeDtypeStruct((3, bsz, d, s // d, GROUP_WIDTH), BF16)
                  for _, d in DILATED_PATTERNS]
    return pl.pallas_call(
        _proj_kernel,
        grid=(bsz, s // ROW_TILE),
        in_specs=[_row_block(D_MODEL), _resident((1, D_MODEL)), _resident((D_MODEL, in_width)),
                  pl.BlockSpec((None, LANES, ROW_TILE), lambda b, i: (b, 0, i)),
                  _resident((1, LANES)), _resident((1, LANES)),
                  _resident((1, 2 * D_MODEL))],
        out_specs=[_group_block(d, GROUP_WIDTH, lead=(3,)) for _, d in DILATED_PATTERNS]
        + [_row_block(CONV_CH), _row_block(2 * D_MODEL)],
        out_shape=qkv_shapes + [jax.ShapeDtypeStruct((bsz, s, CONV_CH), F32),
                                jax.ShapeDtypeStruct((bsz, s, 2 * D_MODEL), F32)],
        scratch_shapes=[pltpu.VMEM((ROW_TILE, D_MODEL), BF16),
                        pltpu.VMEM((GROUP_CHUNKS, ROW_TILE, LANES), F32),
                        pltpu.VMEM((3, ROW_TILE, GROUP_WIDTH), F32),
                        pltpu.VMEM((2, ROW_TILE, LANES), F32)],
        compiler_params=pltpu.CompilerParams(dimension_semantics=("arbitrary", "arbitrary"),
                                             vmem_limit_bytes=VMEM_LIMIT),
        name="proj",
    )(x3, n1, w_in, rot, qn, kn, bg)


def _attn_kernel(q_ref, k_ref, v_ref, o_ref, lse_ref, kpad_ref, vt_ref, bias_ref, s_ref, p_ref,
                 ot_ref, stat_ref):
    n_seq, seq_len, _ = q_ref.shape
    whole = seq_len == KEY_BLOCK
    kw = KEY_BLOCK if whole else 3 * SUB_Q
    n_key_blocks = seq_len // KEY_BLOCK
    n_blocks = n_seq * n_key_blocks
    vt_pad = 0 if whole else 1
    last_sub = seq_len // SUB_Q - 1
    subs = KEY_BLOCK // SUB_Q
    lane = lax.broadcasted_iota(jnp.int32, (1, LANES), 1)
    first_head = lane < HEAD_DIM
    block_diag = (lax.broadcasted_iota(jnp.int32, (LANES, LANES), 0) < HEAD_DIM) == first_head

    @pl.when(pl.program_id(0) == 0)
    def _init():
        row = lax.broadcasted_iota(jnp.int32, (kw, LANES), 0)
        lq = lax.broadcasted_iota(jnp.int32, (kw, LANES), 1) % SUB_Q
        as_bias = lambda valid: jnp.where(valid, 0.0, NEG_INF).astype(F32)
        if whole:
            for case in range(subs):
                bias_ref[case] = as_bias(jnp.abs(lq + case * SUB_Q - row) <= HALF_SPAN)
        else:
            band = jnp.abs(lq + SUB_Q - row) <= HALF_SPAN
            bias_ref[0] = as_bias(band & (row >= SUB_Q))
            bias_ref[1] = as_bias(band)
            bias_ref[2] = as_bias(band & (row < 2 * SUB_Q))
            zeros = jnp.zeros((SUB_Q, GROUP_WIDTH), BF16)
            zero_block = jnp.zeros((GROUP_WIDTH, KEY_BLOCK), BF16)
            for r in range(n_seq):
                kpad_ref[r, 0:SUB_Q, :] = zeros
                kpad_ref[r, SUB_Q + seq_len:2 * SUB_Q + seq_len, :] = zeros
                vt_ref[r, 0] = zero_block
                vt_ref[r, n_key_blocks + 1] = zero_block

    def split(j):
        return j // n_key_blocks, j % n_key_blocks

    if not whole:
        for r in range(n_seq):
            kpad_ref[r, SUB_Q:SUB_Q + seq_len, :] = k_ref[r]

    def stage_vt(j):
        r, b = split(j)
        rows = pl.ds(pl.multiple_of(b * KEY_BLOCK, KEY_BLOCK), KEY_BLOCK)
        for hp in range(GROUP_CHUNKS):
            cols = slice(hp * LANES, (hp + 1) * LANES)
            vt_ref[r, b + vt_pad, cols, :] = v_ref[r, rows, cols].astype(F32).T.astype(BF16)

    def stage1_qk(j):
        r, i = split(j)
        for sub in range(subs):
            q0 = pl.multiple_of((subs * i + sub) * SUB_Q, SUB_Q)
            for hp in range(GROUP_CHUNKS):
                cols = slice(hp * LANES, (hp + 1) * LANES)
                qb = q_ref[r, pl.ds(q0, SUB_Q), cols]
                zq = jnp.zeros_like(qb)
                qv = jnp.concatenate([jnp.where(first_head, qb, zq),
                                      jnp.where(first_head, zq, qb)], axis=0)
                kwin = k_ref[r, :, cols] if whole else kpad_ref[r, pl.ds(q0, kw), cols]
                s_ref[sub * GROUP_CHUNKS + hp] = lax.dot_general(
                    kwin, qv, (((1,), (1,)), ((), ())), preferred_element_type=F32)

    def stage2_softmax(j):
        _, i = split(j)
        for sub in range(subs):
            t = subs * i + sub
            if whole:
                bias = bias_ref[sub]
            else:
                bias = bias_ref[jnp.where(t == 0, 0, jnp.where(t == last_sub, 2, 1))]
            for hp in range(GROUP_CHUNKS):
                unit = sub * GROUP_CHUNKS + hp
                if whole:
                    s = s_ref[unit] + bias
                else:
                    s = jnp.concatenate([s_ref[unit, 0:SUB_Q, :] + bias[0:SUB_Q],
                                         s_ref[unit, SUB_Q:2 * SUB_Q, :],
                                         s_ref[unit, 2 * SUB_Q:kw, :] + bias[2 * SUB_Q:kw]], axis=0)
                m = jnp.max(s, axis=0, keepdims=True)
                p = jnp.exp2(s - m)
                den = jnp.sum(p, axis=0, keepdims=True)
                p_ref[unit] = p.astype(BF16)
                stat_ref[0, unit, 0:1, :] = 1.0 / den
                stat_ref[0, unit, 1:2, :] = (m + jnp.log2(den)) * LN2

    def stage3_pv(j):
        r, i = split(j)
        for sub in range(subs):
            for hp in range(GROUP_CHUNKS):
                unit = sub * GROUP_CHUNKS + hp
                cols = slice(hp * LANES, (hp + 1) * LANES)
                pb = p_ref[unit]
                if whole:
                    ot_ref[unit] = _dot(vt_ref[r, 0, cols, :], pb)
                else:
                    first = i + sub
                    vt2 = jnp.concatenate([vt_ref[r, first, cols, :], vt_ref[r, first + 1, cols, :]], axis=1)
                    zp = jnp.zeros((SUB_Q, LANES), BF16)
                    ot_ref[unit] = _dot(vt2, jnp.concatenate([zp, pb] if sub == 0 else [pb, zp], axis=0))
                stat_ref[1, unit] = stat_ref[0, unit]

    def stage4_finish(j):
        r, i = split(j)
        for sub in range(subs):
            q0 = pl.multiple_of((subs * i + sub) * SUB_Q, SUB_Q)
            lse_tile = jnp.zeros((SUB_Q, LANES), F32)
            for hp in range(GROUP_CHUNKS):
                unit = sub * GROUP_CHUNKS + hp
                cols = slice(hp * LANES, (hp + 1) * LANES)
                tile = jnp.where(block_diag, ot_ref[unit] * stat_ref[1, unit, 0:1, :],
                                 stat_ref[1, unit, 1:2, :]).T
                top, bot = tile[0:SUB_Q], tile[SUB_Q:2 * SUB_Q]
                o_ref[r, pl.ds(q0, SUB_Q), cols] = jnp.where(first_head, top, bot).astype(BF16)
                lse_tile = jnp.where((lane % HEAD_DIM) // LSE_LANES == hp,
                                     jnp.where(first_head, bot, top), lse_tile)
            lse_ref[r, pl.ds(q0, SUB_Q), :] = lse_tile

    def trip(j, stages=(4, 3, 2, 1)):
        if 4 in stages:
            stage4_finish(j - 2)
        if 3 in stages:
            stage3_pv(j - 1)
        if 2 in stages:
            stage2_softmax(j)
        if 1 in stages:
            stage1_qk(j + 1)
            stage_vt(j + 1)

    trip(-1, stages=(1,))
    trip(0, stages=(2, 1))
    trip(1, stages=(3, 2, 1))

    def steady(j, c):
        trip(j)
        return c

    lax.fori_loop(2, n_blocks - 1, steady, 0)
    trip(n_blocks - 1, stages=(4, 3, 2))
    trip(n_blocks, stages=(4, 3))
    trip(n_blocks + 1, stages=(4,))


def _attn_call(qkv):
    _, bsz, dilation, seq_len, _ = qkv.shape
    whole = seq_len == KEY_BLOCK
    kw = KEY_BLOCK if whole else 3 * SUB_Q
    units = (KEY_BLOCK // SUB_Q) * GROUP_CHUNKS
    in_spec = lambda kind: pl.BlockSpec((None, None, dilation, seq_len, GROUP_WIDTH),
                                        lambda b: (kind, b, 0, 0, 0))
    out_spec = lambda w: pl.BlockSpec((None, dilation, seq_len, w), lambda b: (b, 0, 0, 0))
    return pl.pallas_call(
        _attn_kernel,
        grid=(bsz,),
        in_specs=[in_spec(0), in_spec(1), in_spec(2)],
        out_specs=[out_spec(GROUP_WIDTH), out_spec(LANES)],
        out_shape=[jax.ShapeDtypeStruct((bsz, dilation, seq_len, GROUP_WIDTH), BF16),
                   jax.ShapeDtypeStruct((bsz, dilation, seq_len, LANES), F32)],
        scratch_shapes=[
            pltpu.VMEM((dilation, seq_len + 2 * SUB_Q, GROUP_WIDTH), BF16),
            pltpu.VMEM((dilation, seq_len // KEY_BLOCK + (0 if whole else 2), GROUP_WIDTH, KEY_BLOCK), BF16),
            pltpu.VMEM((2 if whole else 3, kw, LANES), F32),
            pltpu.VMEM((units, kw, LANES), F32),
            pltpu.VMEM((units, kw, LANES), BF16),
            pltpu.VMEM((units, LANES, LANES), F32),
            pltpu.VMEM((2, units, SUBLANES, LANES), F32)],
        compiler_params=pltpu.CompilerParams(dimension_semantics=("arbitrary",),
                                             vmem_limit_bytes=VMEM_LIMIT),
        name=f"attn_d{dilation}",
    )(qkv, qkv, qkv)


CONV_PAD = 16
CONV_ROWS = 128
CONV_CHUNKS = CONV_CH // LANES


def _conv_stage(u_ref, prev_ref, next_ref, has_prev, has_next, pad_ref):
    for c in range(CONV_CHUNKS):
        cols = slice(c * LANES, (c + 1) * LANES)
        pad_ref[c, 0:CONV_PAD, :] = jnp.where(has_prev, prev_ref[:, cols], 0.0)
        pad_ref[c, CONV_PAD:CONV_PAD + ROW_TILE, :] = u_ref[:, cols]
        pad_ref[c, CONV_PAD + ROW_TILE:2 * CONV_PAD + ROW_TILE, :] = jnp.where(
            has_next, next_ref[:, cols], 0.0)


def _conv_ln_rows(r0, w_ref, b_ref, lnw_ref, lnb_ref, out_ref, pad_ref, conv_ref):
    chunks = [slice(c * LANES, (c + 1) * LANES) for c in range(CONV_CHUNKS)]
    first_tap = CONV_PAD - (CONV_WIDTH - 1) // 2
    half_rows = CONV_ROWS // 2
    for c, cols in enumerate(chunks):
        for parity in range(2):
            acc = jnp.zeros((half_rows, LANES), F32) + b_ref[:, cols]
            for t in range(CONV_WIDTH):
                rows = pl.ds(r0 + parity + first_tap + t, half_rows, stride=2)
                acc = acc + pad_ref[c, rows, :] * w_ref[t:t + 1, cols]
            conv_ref[c, pl.ds(r0 + parity, half_rows, stride=2), :] = acc
    ys = [conv_ref[c, r0:r0 + CONV_ROWS, :] for c in range(CONV_CHUNKS)]
    mu = sum(jnp.sum(y, axis=-1, keepdims=True) for y in ys) * (1.0 / CONV_CH)
    cen = [y - mu for y in ys]
    var = sum(jnp.sum(t * t, axis=-1, keepdims=True) for t in cen) * (1.0 / CONV_CH)
    inv = lax.rsqrt(var + EPS)
    for c, cols in enumerate(chunks):
        y = cen[c] * inv * lnw_ref[:, cols] + lnb_ref[:, cols]
        out_ref[r0:r0 + CONV_ROWS, cols] = (y * jax.nn.sigmoid(y)).astype(BF16)
    return y[0:1, :]


def _zero_after(value):
    bits = value.astype(jnp.int32)
    return lax.shift_right_logical(lax.shift_right_logical(bits, 16), 16).astype(F32)


FF_CHUNK = 256


def _mix_ffn_kernel(x_ref, o0_ref, o1_ref, o2_ref, l0_ref, l1_ref, l2_ref, g_ref,
                    uc_ref, up_ref, un_ref, cw_ref, cb_ref, lnw_ref, lnb_ref,
                    wo_ref, wpw_ref, wout_ref, n2_ref, w1_ref, w2_ref,
                    out_ref, attn_ref, act_ref, perm_ref, pad_ref, conv_ref, uln_ref,
                    *, tiles_per_seq):
    t = pl.program_id(0)
    conv_pos = jnp.minimum(t, pl.num_programs(0) - 2) % tiles_per_seq
    write_slot = t % 2
    read_slot = 1 - write_slot

    @pl.when(t == 0)
    def _init():
        uln_ref[1] = jnp.zeros(uln_ref.shape[1:], BF16)

    _conv_stage(uc_ref, up_ref, un_ref, conv_pos > 0, conv_pos < tiles_per_seq - 1, pad_ref)

    def token_order(ref, group, slab, cols):
        dilation = DILATED_PATTERNS[group][1]
        if dilation == 1:
            return ref[0, :, cols].astype(F32)
        n = ROW_TILE // dilation
        for r in range(dilation):
            perm_ref[slab, pl.ds(r, n, stride=dilation), :] = ref[r, :, cols].astype(F32)
        return perm_ref[slab]

    o_refs = (o0_ref, o1_ref, o2_ref)
    all_lanes = slice(0, LANES)
    n_slabs = GROUP_CHUNKS + 1
    lse = [token_order(ref, g, g * n_slabs + GROUP_CHUNKS, all_lanes)
           for g, ref in enumerate((l0_ref, l1_ref, l2_ref))]
    m = jnp.maximum(jnp.maximum(lse[0], lse[1]), lse[2])
    e = [jnp.exp(l - m) for l in lse]
    inv = 1.0 / (e[0] + e[1] + e[2])
    weights = [(e_g * inv).astype(BF16) for e_g in e]
    src_lane = lax.broadcasted_iota(jnp.int32, (LANES, LANES), 0)
    first_head = lax.broadcasted_iota(jnp.int32, (LANES, LANES), 1) < HEAD_DIM
    for c in range(GROUP_CHUNKS):
        cols = slice(c * LANES, (c + 1) * LANES)
        lane_b = LSE_LANES * c
        expand = jnp.where(src_lane == jnp.where(first_head, lane_b + HEAD_DIM, lane_b),
                           1.0, 0.0).astype(BF16)
        acc = jnp.zeros((ROW_TILE, LANES), F32)
        for g in range(N_GROUPS):
            acc = acc + _dot(weights[g], expand) * token_order(o_refs[g], g, g * n_slabs + c, cols)
        attn_ref[:, cols] = acc.astype(BF16)

    y_a = _dot(attn_ref[...], wo_ref[...])
    y_b = _dot(uln_ref[read_slot], wpw_ref[...])
    z = g_ref[:, 0:D_MODEL] * y_a + g_ref[:, D_MODEL:2 * D_MODEL] * y_b
    x1 = x_ref[...] + _dot(z.astype(BF16), wout_ref[...])

    ms = jnp.mean(x1 * x1, axis=-1, keepdims=True)
    h2 = (x1 * lax.rsqrt(ms + EPS) * n2_ref[...]).astype(BF16)
    ff_chunks = range(0, D_FF, FF_CHUNK)
    conv_blocks = list(range(0, ROW_TILE, CONV_ROWS))
    conv_every = len(ff_chunks) // len(conv_blocks)
    pending = None
    for idx, n in enumerate(ff_chunks):
        gt = _dot(h2, w1_ref[:, n:n + FF_CHUNK])
        up = _dot(h2, w1_ref[:, D_FF + n:D_FF + n + FF_CHUNK])
        act = gt * jax.nn.sigmoid(gt) * up
        if pending is not None and idx % conv_every == conv_every - 1:
            act = act + jnp.concatenate([_zero_after(pending)] * (FF_CHUNK // LANES), axis=1)
            pending = None
        act_ref[:, n:n + FF_CHUNK] = act.astype(BF16)
        if idx % conv_every == 0 and idx // conv_every < len(conv_blocks):
            pending = _conv_ln_rows(conv_blocks[idx // conv_every], cw_ref, cb_ref, lnw_ref,
                                    lnb_ref, uln_ref.at[write_slot], pad_ref, conv_ref)
    out_ref[...] = x1 + _dot(act_ref[...], w2_ref[...])


def _mix_ffn_call(x3, o, lse, gates, u, conv_w, conv_b, ln_w, ln_b, wo, wpw, wout, n2, w1, w2):
    bsz, s, _ = x3.shape
    dilations = [d for _, d in DILATED_PATTERNS]
    tiles_per_seq = s // ROW_TILE
    n_tiles = bsz * tiles_per_seq
    halo_blocks = ROW_TILE // CONV_PAD

    def finish_tile(t):
        tile = jnp.maximum(t - 1, 0)
        return tile // tiles_per_seq, tile % tiles_per_seq

    def conv_tile(t):
        tile = jnp.minimum(t, n_tiles - 1)
        return tile // tiles_per_seq, tile % tiles_per_seq

    def row(width):
        return pl.BlockSpec((None, ROW_TILE, width), lambda t: (*finish_tile(t), 0))

    def group(dilation, width):
        return pl.BlockSpec((None, dilation, ROW_TILE // dilation, width),
                            lambda t: (finish_tile(t)[0], 0, finish_tile(t)[1], 0))

    def halo(offset):
        def index(t):
            b, i = conv_tile(t)
            block = i * halo_blocks - 1 if offset < 0 else (i + 1) * halo_blocks
            return b, jnp.clip(block, 0, s // CONV_PAD - 1), 0
        return pl.BlockSpec((None, CONV_PAD, CONV_CH), index)

    return pl.pallas_call(
        functools.partial(_mix_ffn_kernel, tiles_per_seq=tiles_per_seq),
        grid=(n_tiles + 1,),
        in_specs=[row(D_MODEL)]
        + [group(d, GROUP_WIDTH) for d in dilations]
        + [group(d, LANES) for d in dilations]
        + [row(2 * D_MODEL),
           pl.BlockSpec((None, ROW_TILE, CONV_CH), lambda t: (*conv_tile(t), 0)), halo(-1), halo(1),
           _resident(conv_w.shape), _resident(conv_b.shape), _resident(ln_w.shape),
           _resident(ln_b.shape),
           _resident(wo.shape), _resident(wpw.shape),
           _resident(wout.shape), _resident(n2.shape), _resident(w1.shape), _resident(w2.shape)],
        out_specs=row(D_MODEL),
        out_shape=jax.ShapeDtypeStruct((bsz, s, D_MODEL), F32),
        scratch_shapes=[pltpu.VMEM((ROW_TILE, GROUP_WIDTH), BF16),
                        pltpu.VMEM((ROW_TILE, D_FF), BF16),
                        pltpu.VMEM((N_GROUPS * (GROUP_CHUNKS + 1), ROW_TILE, LANES), F32),
                        pltpu.VMEM((CONV_CHUNKS, ROW_TILE + 2 * CONV_PAD, LANES), F32),
                        pltpu.VMEM((CONV_CHUNKS, ROW_TILE, LANES), F32),
                        pltpu.VMEM((2, ROW_TILE, CONV_CH), BF16)],
        compiler_params=pltpu.CompilerParams(dimension_semantics=("arbitrary",),
                                             vmem_limit_bytes=MIX_VMEM_LIMIT),
        name="mix_ffn",
    )(x3, *o, *lse, gates, u, u, u, conv_w, conv_b, ln_w, ln_b, wo, wpw, wout, n2, w1, w2)


def _rope_table(positions):
    inv_freq = ROPE_THETA ** (-jnp.arange(0, ROT_DIM, 2, dtype=F32) / ROT_DIM)
    ang = positions.astype(F32)[:, None, :] * inv_freq[None, :, None]
    rest = jnp.zeros((ang.shape[0], HEAD_DIM - ROT_DIM, ang.shape[2]), F32)
    return jnp.concatenate([jnp.cos(ang), jnp.sin(ang), rest] * (LANES // HEAD_DIM), axis=1)


def kernel(x, positions, norm1_w, w_in, b_gate, q_norm_w, k_norm_w, w_o_attn, conv_w, conv_b,
           conv_ln_w, conv_ln_b, w_pw_conv, w_out, norm2_w, w_ffn_in, w_ffn_out):
    d = x.shape[-1]
    depth = norm1_w.shape[0]
    rot = _rope_table(positions)
    pair = lambda w: jnp.tile(w.astype(F32), LANES // HEAD_DIM).reshape(1, LANES)

    for l in range(depth):
        qkv0, qkv1, qkv2, u, gates = _proj_call(
            x, norm1_w[l].reshape(1, d), w_in[l].astype(BF16), rot,
            pair(q_norm_w[l]), pair(k_norm_w[l]), b_gate[l].reshape(1, 2 * d))
        outs, lses = zip(*[_attn_call(qkv) for qkv in (qkv0, qkv1, qkv2)])
        x = _mix_ffn_call(
            x, outs, lses, gates, u, conv_w[l], conv_b[l].reshape(1, CONV_CH),
            conv_ln_w[l].reshape(1, CONV_CH), conv_ln_b[l].reshape(1, CONV_CH),
            w_o_attn[l].astype(BF16), w_pw_conv[l].astype(BF16), w_out[l].astype(BF16),
            norm2_w[l].reshape(1, d), w_ffn_in[l].astype(BF16), w_ffn_out[l].astype(BF16))
    return x
```

```python
import functools

import jax
import jax.numpy as jnp
from jax import lax
from jax.experimental import pallas as pl
from jax.experimental.pallas import tpu as pltpu

D_MODEL = 1024
HEAD_DIM = 64
N_SLOT_HEADS = 8
DILATED_PATTERNS = ((128, 1), (512, 4), (2048, 16))
N_GROUPS = len(DILATED_PATTERNS)
GROUP_WIDTH = N_SLOT_HEADS * HEAD_DIM
QKV_WIDTH = N_GROUPS * GROUP_WIDTH
ROPE_THETA = 500000.0
ROT_DIM = HEAD_DIM // 4
CONV_CH = D_MODEL // 2
CONV_WIDTH = 31
D_FF = 2816
EPS = 1e-6
NEG_INF = -1e30

LANES = 128
SUBLANES = 8
SUB_Q = 64
KEY_BLOCK = 128
HALF_SPAN = 64
LSE_LANES = 16
LOG2E = 1.4426950408889634
LN2 = 0.6931471805599453
ROW_TILE = 512
EPI_ROWS = 256
GROUP_CHUNKS = GROUP_WIDTH // LANES
VMEM_LIMIT = 56 * 1024 * 1024
MIX_VMEM_LIMIT = 60 * 1024 * 1024

F32 = jnp.float32
BF16 = jnp.bfloat16


def _dot(a, b):
    return jnp.dot(a, b, preferred_element_type=F32)


def _split_dot(a_f32, b_bf16):
    hi = a_f32.astype(BF16)
    lo = (a_f32 - hi.astype(F32)).astype(BF16)
    return _dot(hi, b_bf16) + _dot(lo, b_bf16)


def _resident(shape):
    nd = len(shape)
    return pl.BlockSpec(shape, lambda *_: (0,) * nd, pipeline_mode=pl.Buffered(1))


def _row_block(width):
    return pl.BlockSpec((None, ROW_TILE, width), lambda b, i: (b, i, 0))


def _group_block(dilation, width, lead=()):
    n_lead = len(lead)
    return pl.BlockSpec(lead + (None, dilation, ROW_TILE // dilation, width),
                        lambda b, i: (0,) * n_lead + (b, 0, i, 0))


def _proj_kernel(x_ref, n1_ref, w_ref, rot_ref, qn_ref, kn_ref,
                 bg_ref, qkv0_ref, qkv1_ref, qkv2_ref, u_ref, g_ref, h_ref, perm_ref, acc_ref,
                 rope_ref):
    xt = x_ref[...]
    ms = jnp.mean(xt * xt, axis=-1, keepdims=True)
    h_ref[...] = (xt * lax.rsqrt(ms + EPS) * n1_ref[...]).astype(BF16)

    row_blocks = [slice(r, r + EPI_ROWS) for r in range(0, ROW_TILE, EPI_ROWS)]
    chunks = [slice(c * LANES, (c + 1) * LANES) for c in range(GROUP_CHUNKS)]

    half_rot = ROT_DIM // 2
    head_lane = lax.broadcasted_iota(jnp.int32, (1, LANES), 1) % HEAD_DIM
    first_half = head_lane < half_rot
    for rows in row_blocks:
        tab = rot_ref[:, rows].T
        rope_ref[0, rows, :] = jnp.where(first_half, tab,
                                         jnp.where(head_lane < ROT_DIM, pltpu.roll(tab, half_rot, 1), 1.0))
        rope_ref[1, rows, :] = jnp.where(first_half, -pltpu.roll(tab, LANES - half_rot, 1),
                                         jnp.where(head_lane < ROT_DIM, tab, 0.0))

    n_proj = [0]

    def proj(c0):
        slot = n_proj[0] % acc_ref.shape[0]
        n_proj[0] += 1
        acc_ref[slot] = _dot(h_ref[...], w_ref[:, c0:c0 + GROUP_WIDTH])
        return acc_ref.at[slot]

    same_head = (lax.broadcasted_iota(jnp.int32, (LANES, LANES), 0) // HEAD_DIM
                 == lax.broadcasted_iota(jnp.int32, (LANES, LANES), 1) // HEAD_DIM)
    seg = jnp.where(same_head, 1.0 / HEAD_DIM, 0.0).astype(BF16)

    def head_norm_rope(t, nw, rows):
        ms = _dot((t * t).astype(BF16), seg)
        y = t * lax.rsqrt(ms + EPS) * nw
        partner = jnp.where(first_half, pltpu.roll(y, LANES - half_rot, 1), pltpu.roll(y, half_rot, 1))
        return y * rope_ref[0, rows, :] + partner * rope_ref[1, rows, :]

    def emit(kind, group, acc, fn):
        out_ref = (qkv0_ref, qkv1_ref, qkv2_ref)[group]
        dilation = DILATED_PATTERNS[group][1]
        n = ROW_TILE // dilation
        for c, cols in enumerate(chunks):
            for rows in row_blocks:
                val = fn(acc[rows, cols], rows)
                if dilation == 1:
                    out_ref[kind, 0, rows, cols] = val.astype(BF16)
                else:
                    perm_ref[c, rows, :] = val
            for r in range(dilation if dilation > 1 else 0):
                out_ref[kind, r, :, cols] = perm_ref[c, pl.ds(r, n, stride=dilation), :].astype(BF16)

    def qk_block(kind, j):
        nw_ref, scale = ((qn_ref, HEAD_DIM ** -0.5 * LOG2E), (kn_ref, 1.0))[kind]
        nw = nw_ref[...] * scale
        acc = proj(kind * QKV_WIDTH + j * GROUP_WIDTH)
        emit(kind, j, acc, lambda t, rows: head_norm_rope(t, nw, rows))

    def v_block(j):
        emit(2, j, proj(2 * QKV_WIDTH + j * GROUP_WIDTH), lambda t, rows: t)

    conv0 = 3 * QKV_WIDTH
    gate0 = conv0 + 2 * CONV_CH

    def gate_block(j):
        lo = j * GROUP_WIDTH
        acc = proj(gate0 + lo)
        for rows in row_blocks:
            g_ref[rows, lo:lo + GROUP_WIDTH] = jax.nn.sigmoid(
                acc[rows, :] + bg_ref[:, lo:lo + GROUP_WIDTH])

    for j in range(N_GROUPS):
        qk_block(0, j)
        v_block(j)
        qk_block(1, j)
        gate_block(j)
    a = proj(conv0)
    b = proj(conv0 + CONV_CH)
    for rows in row_blocks:
        u_ref[rows, :] = a[rows, :] * jax.nn.sigmoid(b[rows, :])
    gate_block(N_GROUPS)


def _proj_call(x3, n1, w_in, rot, qn, kn, bg):
    bsz, s, _ = x3.shape
    in_width = w_in.shape[1]
    qkv_shapes = [jax.ShapeDtypeStruct((3, bsz, d, s // d, GROUP_WIDTH), BF16)
                  for _, d in DILATED_PATTERNS]
    return pl.pallas_call(
        _proj_kernel,
        grid=(bsz, s // ROW_TILE),
        in_specs=[_row_block(D_MODEL), _resident((1, D_MODEL)), _resident((D_MODEL, in_width)),
                  pl.BlockSpec((None, LANES, ROW_TILE), lambda b, i: (b, 0, i)),
                  _resident((1, LANES)), _resident((1, LANES)),
                  _resident((1, 2 * D_MODEL))],
        out_specs=[_group_block(d, GROUP_WIDTH, lead=(3,)) for _, d in DILATED_PATTERNS]
        + [_row_block(CONV_CH), _row_block(2 * D_MODEL)],
        out_shape=qkv_shapes + [jax.ShapeDtypeStruct((bsz, s, CONV_CH), F32),
                                jax.ShapeDtypeStruct((bsz, s, 2 * D_MODEL), F32)],
        scratch_shapes=[pltpu.VMEM((ROW_TILE, D_MODEL), BF16),
                        pltpu.VMEM((GROUP_CHUNKS, ROW_TILE, LANES), F32),
                        pltpu.VMEM((3, ROW_TILE, GROUP_WIDTH), F32),
                        pltpu.VMEM((2, ROW_TILE, LANES), F32)],
        compiler_params=pltpu.CompilerParams(dimension_semantics=("arbitrary", "arbitrary"),
                                             vmem_limit_bytes=VMEM_LIMIT),
        name="proj",
    )(x3, n1, w_in, rot, qn, kn, bg)


def _attn_kernel(q_ref, k_ref, v_ref, o_ref, lse_ref, kpad_ref, vt_ref, bias_ref, s_ref, p_ref,
                 ot_ref, stat_ref):
    n_seq, seq_len, _ = q_ref.shape
    whole = seq_len == KEY_BLOCK
    kw = KEY_BLOCK if whole else 3 * SUB_Q
    n_key_blocks = seq_len // KEY_BLOCK
    n_blocks = n_seq * n_key_blocks
    vt_pad = 0 if whole else 1
    last_sub = seq_len // SUB_Q - 1
    subs = KEY_BLOCK // SUB_Q
    lane = lax.broadcasted_iota(jnp.int32, (1, LANES), 1)
    first_head = lane < HEAD_DIM
    block_diag = (lax.broadcasted_iota(jnp.int32, (LANES, LANES), 0) < HEAD_DIM) == first_head

    @pl.when(pl.program_id(0) == 0)
    def _init():
        row = lax.broadcasted_iota(jnp.int32, (kw, LANES), 0)
        lq = lax.broadcasted_iota(jnp.int32, (kw, LANES), 1) % SUB_Q
        as_bias = lambda valid: jnp.where(valid, 0.0, NEG_INF).astype(F32)
        if whole:
            for case in range(subs):
                bias_ref[case] = as_bias(jnp.abs(lq + case * SUB_Q - row) <= HALF_SPAN)
        else:
            band = jnp.abs(lq + SUB_Q - row) <= HALF_SPAN
            bias_ref[0] = as_bias(band & (row >= SUB_Q))
            bias_ref[1] = as_bias(band)
            bias_ref[2] = as_bias(band & (row < 2 * SUB_Q))
            zeros = jnp.zeros((SUB_Q, GROUP_WIDTH), BF16)
            zero_block = jnp.zeros((GROUP_WIDTH, KEY_BLOCK), BF16)
            for r in range(n_seq):
                kpad_ref[r, 0:SUB_Q, :] = zeros
                kpad_ref[r, SUB_Q + seq_len:2 * SUB_Q + seq_len, :] = zeros
                vt_ref[r, 0] = zero_block
                vt_ref[r, n_key_blocks + 1] = zero_block

    def split(j):
        return j // n_key_blocks, j % n_key_blocks

    if not whole:
        for r in range(n_seq):
            kpad_ref[r, SUB_Q:SUB_Q + seq_len, :] = k_ref[r]

    def stage_vt(j):
        r, b = split(j)
        rows = pl.ds(pl.multiple_of(b * KEY_BLOCK, KEY_BLOCK), KEY_BLOCK)
        for hp in range(GROUP_CHUNKS):
            cols = slice(hp * LANES, (hp + 1) * LANES)
            vt_ref[r, b + vt_pad, cols, :] = v_ref[r, rows, cols].astype(F32).T.astype(BF16)

    def stage1_qk(j):
        r, i = split(j)
        for sub in range(subs):
            q0 = pl.multiple_of((subs * i + sub) * SUB_Q, SUB_Q)
            for hp in range(GROUP_CHUNKS):
                cols = slice(hp * LANES, (hp + 1) * LANES)
                qb = q_ref[r, pl.ds(q0, SUB_Q), cols]
                zq = jnp.zeros_like(qb)
                qv = jnp.concatenate([jnp.where(first_head, qb, zq),
                                      jnp.where(first_head, zq, qb)], axis=0)
                kwin = k_ref[r, :, cols] if whole else kpad_ref[r, pl.ds(q0, kw), cols]
                s_ref[sub * GROUP_CHUNKS + hp] = lax.dot_general(
                    kwin, qv, (((1,), (1,)), ((), ())), preferred_element_type=F32)

    def stage2_softmax(j):
        _, i = split(j)
        for sub in range(subs):
            t = subs * i + sub
            if whole:
                bias = bias_ref[sub]
            else:
                bias = bias_ref[jnp.where(t == 0, 0, jnp.where(t == last_sub, 2, 1))]
            for hp in range(GROUP_CHUNKS):
                unit = sub * GROUP_CHUNKS + hp
                if whole:
                    s = s_ref[unit] + bias
                else:
                    s = jnp.concatenate([s_ref[unit, 0:SUB_Q, :] + bias[0:SUB_Q],
                                         s_ref[unit, SUB_Q:2 * SUB_Q, :],
                                         s_ref[unit, 2 * SUB_Q:kw, :] + bias[2 * SUB_Q:kw]], axis=0)
                m = jnp.max(s, axis=0, keepdims=True)
                p = jnp.exp2(s - m)
                den = jnp.sum(p, axis=0, keepdims=True)
                p_ref[unit] = p.astype(BF16)
                stat_ref[0, unit, 0:1, :] = 1.0 / den
                stat_ref[0, unit, 1:2, :] = (m + jnp.log2(den)) * LN2

    def stage3_pv(j):
        r, i = split(j)
        for sub in range(subs):
            for hp in range(GROUP_CHUNKS):
                unit = sub * GROUP_CHUNKS + hp
                cols = slice(hp * LANES, (hp + 1) * LANES)
                pb = p_ref[unit]
                if whole:
                    ot_ref[unit] = _dot(vt_ref[r, 0, cols, :], pb)
                else:
                    first = i + sub
                    vt2 = jnp.concatenate([vt_ref[r, first, cols, :], vt_ref[r, first + 1, cols, :]], axis=1)
                    zp = jnp.zeros((SUB_Q, LANES), BF16)
                    ot_ref[unit] = _dot(vt2, jnp.concatenate([zp, pb] if sub == 0 else [pb, zp], axis=0))
                stat_ref[1, unit] = stat_ref[0, unit]

    def stage4_finish(j):
        r, i = split(j)
        for sub in range(subs):
            q0 = pl.multiple_of((subs * i + sub) * SUB_Q, SUB_Q)
            lse_tile = jnp.zeros((SUB_Q, LANES), F32)
            for hp in range(GROUP_CHUNKS):
                unit = sub * GROUP_CHUNKS + hp
                cols = slice(hp * LANES, (hp + 1) * LANES)
                tile = jnp.where(block_diag, ot_ref[unit] * stat_ref[1, unit, 0:1, :],
                                 stat_ref[1, unit, 1:2, :]).T
                top, bot = tile[0:SUB_Q], tile[SUB_Q:2 * SUB_Q]
                o_ref[r, pl.ds(q0, SUB_Q), cols] = jnp.where(first_head, top, bot).astype(BF16)
                lse_tile = jnp.where((lane % HEAD_DIM) // LSE_LANES == hp,
                                     jnp.where(first_head, bot, top), lse_tile)
            lse_ref[r, pl.ds(q0, SUB_Q), :] = lse_tile

    def trip(j, stages=(4, 3, 2, 1)):
        if 4 in stages:
            stage4_finish(j - 2)
        if 3 in stages:
            stage3_pv(j - 1)
        if 2 in stages:
            stage2_softmax(j)
        if 1 in stages:
            stage1_qk(j + 1)
            stage_vt(j + 1)

    trip(-1, stages=(1,))
    trip(0, stages=(2, 1))
    trip(1, stages=(3, 2, 1))

    def steady(j, c):
        trip(j)
        return c

    lax.fori_loop(2, n_blocks - 1, steady, 0)
    trip(n_blocks - 1, stages=(4, 3, 2))
    trip(n_blocks, stages=(4, 3))
    trip(n_blocks + 1, stages=(4,))


def _attn_call(qkv):
    _, bsz, dilation, seq_len, _ = qkv.shape
    whole = seq_len == KEY_BLOCK
    kw = KEY_BLOCK if whole else 3 * SUB_Q
    units = (KEY_BLOCK // SUB_Q) * GROUP_CHUNKS
    in_spec = lambda kind: pl.BlockSpec((None, None, dilation, seq_len, GROUP_WIDTH),
                                        lambda b: (kind, b, 0, 0, 0))
    out_spec = lambda w: pl.BlockSpec((None, dilation, seq_len, w), lambda b: (b, 0, 0, 0))
    return pl.pallas_call(
        _attn_kernel,
        grid=(bsz,),
        in_specs=[in_spec(0), in_spec(1), in_spec(2)],
        out_specs=[out_spec(GROUP_WIDTH), out_spec(LANES)],
        out_shape=[jax.ShapeDtypeStruct((bsz, dilation, seq_len, GROUP_WIDTH), BF16),
                   jax.ShapeDtypeStruct((bsz, dilation, seq_len, LANES), F32)],
        scratch_shapes=[
            pltpu.VMEM((dilation, seq_len + 2 * SUB_Q, GROUP_WIDTH), BF16),
            pltpu.VMEM((dilation, seq_len // KEY_BLOCK + (0 if whole else 2), GROUP_WIDTH, KEY_BLOCK), BF16),
            pltpu.VMEM((2 if whole else 3, kw, LANES), F32),
            pltpu.VMEM((units, kw, LANES), F32),
            pltpu.VMEM((units, kw, LANES), BF16),
            pltpu.VMEM((units, LANES, LANES), F32),
            pltpu.VMEM((2, units, SUBLANES, LANES), F32)],
        compiler_params=pltpu.CompilerParams(dimension_semantics=("arbitrary",),
                                             vmem_limit_bytes=VMEM_LIMIT),
        name=f"attn_d{dilation}",
    )(qkv, qkv, qkv)


CONV_PAD = 16
CONV_ROWS = 64
CONV_CHUNKS = CONV_CH // LANES


def _conv_stage(u_ref, prev_ref, next_ref, has_prev, has_next, pad_ref):
    for c in range(CONV_CHUNKS):
        cols = slice(c * LANES, (c + 1) * LANES)
        pad_ref[c, 0:CONV_PAD, :] = jnp.where(has_prev, prev_ref[:, cols], 0.0)
        pad_ref[c, CONV_PAD:CONV_PAD + ROW_TILE, :] = u_ref[:, cols]
        pad_ref[c, CONV_PAD + ROW_TILE:2 * CONV_PAD + ROW_TILE, :] = jnp.where(
            has_next, next_ref[:, cols], 0.0)


def _conv_ln_rows(r0, w_ref, b_ref, lnw_ref, lnb_ref, out_ref, pad_ref, conv_ref):
    chunks = [slice(c * LANES, (c + 1) * LANES) for c in range(CONV_CHUNKS)]
    first_tap = CONV_PAD - (CONV_WIDTH - 1) // 2
    half_rows = CONV_ROWS // 2
    for c, cols in enumerate(chunks):
        for parity in range(2):
            acc = jnp.zeros((half_rows, LANES), F32) + b_ref[:, cols]
            for t in range(CONV_WIDTH):
                rows = pl.ds(r0 + parity + first_tap + t, half_rows, stride=2)
                acc = acc + pad_ref[c, rows, :] * w_ref[t:t + 1, cols]
            conv_ref[c, pl.ds(r0 + parity, half_rows, stride=2), :] = acc
    ys = [conv_ref[c, r0:r0 + CONV_ROWS, :] for c in range(CONV_CHUNKS)]
    mu = sum(jnp.sum(y, axis=-1, keepdims=True) for y in ys) * (1.0 / CONV_CH)
    cen = [y - mu for y in ys]
    var = sum(jnp.sum(t * t, axis=-1, keepdims=True) for t in cen) * (1.0 / CONV_CH)
    inv = lax.rsqrt(var + EPS)
    for c, cols in enumerate(chunks):
        y = cen[c] * inv * lnw_ref[:, cols] + lnb_ref[:, cols]
        out_ref[r0:r0 + CONV_ROWS, cols] = (y * jax.nn.sigmoid(y)).astype(BF16)
    return y[0:1, :]


def _zero_after(value):
    bits = value.astype(jnp.int32)
    return lax.shift_right_logical(lax.shift_right_logical(bits, 16), 16).astype(F32)


FF_CHUNK = 256


def _mix_ffn_kernel(x_ref, o0_ref, o1_ref, o2_ref, l0_ref, l1_ref, l2_ref, g_ref,
                    uc_ref, up_ref, un_ref, cw_ref, cb_ref, lnw_ref, lnb_ref,
                    wo_ref, wpw_ref, wout_ref, n2_ref, w1_ref, w2_ref,
                    out_ref, attn_ref, act_ref, perm_ref, pad_ref, conv_ref, uln_ref,
                    *, tiles_per_seq):
    t = pl.program_id(0)
    conv_pos = jnp.minimum(t, pl.num_programs(0) - 2) % tiles_per_seq
    write_slot = t % 2
    read_slot = 1 - write_slot

    @pl.when(t == 0)
    def _init():
        uln_ref[1] = jnp.zeros(uln_ref.shape[1:], BF16)

    _conv_stage(uc_ref, up_ref, un_ref, conv_pos > 0, conv_pos < tiles_per_seq - 1, pad_ref)

    def token_order(ref, group, slab, cols):
        dilation = DILATED_PATTERNS[group][1]
        if dilation == 1:
            return ref[0, :, cols].astype(F32)
        n = ROW_TILE // dilation
        for r in range(dilation):
            perm_ref[slab, pl.ds(r, n, stride=dilation), :] = ref[r, :, cols].astype(F32)
        return perm_ref[slab]

    o_refs = (o0_ref, o1_ref, o2_ref)
    all_lanes = slice(0, LANES)
    n_slabs = GROUP_CHUNKS + 1
    lse = [token_order(ref, g, g * n_slabs + GROUP_CHUNKS, all_lanes)
           for g, ref in enumerate((l0_ref, l1_ref, l2_ref))]
    m = jnp.maximum(jnp.maximum(lse[0], lse[1]), lse[2])
    e = [jnp.exp(l - m) for l in lse]
    inv = 1.0 / (e[0] + e[1] + e[2])
    weights = [(e_g * inv).astype(BF16) for e_g in e]
    src_lane = lax.broadcasted_iota(jnp.int32, (LANES, LANES), 0)
    first_head = lax.broadcasted_iota(jnp.int32, (LANES, LANES), 1) < HEAD_DIM
    for c in range(GROUP_CHUNKS):
        cols = slice(c * LANES, (c + 1) * LANES)
        lane_b = LSE_LANES * c
        expand = jnp.where(src_lane == jnp.where(first_head, lane_b + HEAD_DIM, lane_b),
                           1.0, 0.0).astype(BF16)
        acc = jnp.zeros((ROW_TILE, LANES), F32)
        for g in range(N_GROUPS):
            acc = acc + _dot(weights[g], expand) * token_order(o_refs[g], g, g * n_slabs + c, cols)
        attn_ref[:, cols] = acc.astype(BF16)

    y_a = _dot(attn_ref[...], wo_ref[...])
    y_b = _dot(uln_ref[read_slot], wpw_ref[...])
    z = g_ref[:, 0:D_MODEL] * y_a + g_ref[:, D_MODEL:2 * D_MODEL] * y_b
    x1 = x_ref[...] + _dot(z.astype(BF16), wout_ref[...])

    ms = jnp.mean(x1 * x1, axis=-1, keepdims=True)
    h2 = (x1 * lax.rsqrt(ms + EPS) * n2_ref[...]).astype(BF16)
    ff_chunks = range(0, D_FF, FF_CHUNK)
    conv_blocks = list(range(0, ROW_TILE, CONV_ROWS))
    issue_after = [k * (len(ff_chunks) - 1) // len(conv_blocks) for k in range(len(conv_blocks))]
    pending = None
    for idx, n in enumerate(ff_chunks):
        width = min(FF_CHUNK, D_FF - n)
        gt = _dot(h2, w1_ref[:, n:n + width])
        up = _dot(h2, w1_ref[:, D_FF + n:D_FF + n + width])
        act = gt * jax.nn.sigmoid(gt) * up
        if pending is not None:
            act = act + jnp.concatenate([_zero_after(pending)] * (width // LANES), axis=1)
            pending = None
        act_ref[:, n:n + width] = act.astype(BF16)
        for k, r0 in enumerate(conv_blocks):
            if issue_after[k] == idx:
                probe = _conv_ln_rows(r0, cw_ref, cb_ref, lnw_ref, lnb_ref,
                                      uln_ref.at[write_slot], pad_ref, conv_ref)
                pending = probe if pending is None else pending + probe
    out_ref[...] = x1 + _dot(act_ref[...], w2_ref[...])


def _mix_ffn_call(x3, o, lse, gates, u, conv_w, conv_b, ln_w, ln_b, wo, wpw, wout, n2, w1, w2):
    bsz, s, _ = x3.shape
    dilations = [d for _, d in DILATED_PATTERNS]
    tiles_per_seq = s // ROW_TILE
    n_tiles = bsz * tiles_per_seq
    halo_blocks = ROW_TILE // CONV_PAD

    def finish_tile(t):
        tile = jnp.maximum(t - 1, 0)
        return tile // tiles_per_seq, tile % tiles_per_seq

    def conv_tile(t):
        tile = jnp.minimum(t, n_tiles - 1)
        return tile // tiles_per_seq, tile % tiles_per_seq

    def row(width):
        return pl.BlockSpec((None, ROW_TILE, width), lambda t: (*finish_tile(t), 0))

    def group(dilation, width):
        return pl.BlockSpec((None, dilation, ROW_TILE // dilation, width),
                            lambda t: (finish_tile(t)[0], 0, finish_tile(t)[1], 0))

    def halo(offset):
        def index(t):
            b, i = conv_tile(t)
            block = i * halo_blocks - 1 if offset < 0 else (i + 1) * halo_blocks
            return b, jnp.clip(block, 0, s // CONV_PAD - 1), 0
        return pl.BlockSpec((None, CONV_PAD, CONV_CH), index)

    return pl.pallas_call(
        functools.partial(_mix_ffn_kernel, tiles_per_seq=tiles_per_seq),
        grid=(n_tiles + 1,),
        in_specs=[row(D_MODEL)]
        + [group(d, GROUP_WIDTH) for d in dilations]
        + [group(d, LANES) for d in dilations]
        + [row(2 * D_MODEL),
           pl.BlockSpec((None, ROW_TILE, CONV_CH), lambda t: (*conv_tile(t), 0)), halo(-1), halo(1),
           _resident(conv_w.shape), _resident(conv_b.shape), _resident(ln_w.shape),
           _resident(ln_b.shape),
           _resident(wo.shape), _resident(wpw.shape),
           _resident(wout.shape), _resident(n2.shape), _resident(w1.shape), _resident(w2.shape)],
        out_specs=row(D_MODEL),
        out_shape=jax.ShapeDtypeStruct((bsz, s, D_MODEL), F32),
        scratch_shapes=[pltpu.VMEM((ROW_TILE, GROUP_WIDTH), BF16),
                        pltpu.VMEM((ROW_TILE, D_FF), BF16),
                        pltpu.VMEM((N_GROUPS * (GROUP_CHUNKS + 1), ROW_TILE, LANES), F32),
                        pltpu.VMEM((CONV_CHUNKS, ROW_TILE + 2 * CONV_PAD, LANES), F32),
                        pltpu.VMEM((CONV_CHUNKS, ROW_TILE, LANES), F32),
                        pltpu.VMEM((2, ROW_TILE, CONV_CH), BF16)],
        compiler_params=pltpu.CompilerParams(dimension_semantics=("arbitrary",),
                                             vmem_limit_bytes=MIX_VMEM_LIMIT),
        name="mix_ffn",
    )(x3, *o, *lse, gates, u, u, u, conv_w, conv_b, ln_w, ln_b, wo, wpw, wout, n2, w1, w2)


def _rope_table(positions):
    inv_freq = ROPE_THETA ** (-jnp.arange(0, ROT_DIM, 2, dtype=F32) / ROT_DIM)
    ang = positions.astype(F32)[:, None, :] * inv_freq[None, :, None]
    rest = jnp.zeros((ang.shape[0], HEAD_DIM - ROT_DIM, ang.shape[2]), F32)
    return jnp.concatenate([jnp.cos(ang), jnp.sin(ang), rest] * (LANES // HEAD_DIM), axis=1)


def kernel(x, positions, norm1_w, w_in, b_gate, q_norm_w, k_norm_w, w_o_attn, conv_w, conv_b,
           conv_ln_w, conv_ln_b, w_pw_conv, w_out, norm2_w, w_ffn_in, w_ffn_out):
    d = x.shape[-1]
    depth = norm1_w.shape[0]
    rot = _rope_table(positions)
    pair = lambda w: jnp.tile(w.astype(F32), LANES // HEAD_DIM).reshape(1, LANES)

    for l in range(depth):
        qkv0, qkv1, qkv2, u, gates = _proj_call(
            x, norm1_w[l].reshape(1, d), w_in[l].astype(BF16), rot,
            pair(q_norm_w[l]), pair(k_norm_w[l]), b_gate[l].reshape(1, 2 * d))
        outs, lses = zip(*[_attn_call(qkv) for qkv in (qkv0, qkv1, qkv2)])
        x = _mix_ffn_call(
            x, outs, lses, gates, u, conv_w[l], conv_b[l].reshape(1, CONV_CH),
            conv_ln_w[l].reshape(1, CONV_CH), conv_ln_b[l].reshape(1, CONV_CH),
            w_o_attn[l].astype(BF16), w_pw_conv[l].astype(BF16), w_out[l].astype(BF16),
            norm2_w[l].reshape(1, d), w_ffn_in[l].astype(BF16), w_ffn_out[l].astype(BF16))
    return x
```

```python
import functools

import jax
import jax.numpy as jnp
from jax import lax
from jax.experimental import pallas as pl
from jax.experimental.pallas import tpu as pltpu

D_MODEL = 1024
HEAD_DIM = 64
N_SLOT_HEADS = 8
DILATED_PATTERNS = ((128, 1), (512, 4), (2048, 16))
N_GROUPS = len(DILATED_PATTERNS)
GROUP_WIDTH = N_SLOT_HEADS * HEAD_DIM
QKV_WIDTH = N_GROUPS * GROUP_WIDTH
ROPE_THETA = 500000.0
ROT_DIM = HEAD_DIM // 4
CONV_CH = D_MODEL // 2
CONV_WIDTH = 31
D_FF = 2816
EPS = 1e-6
NEG_INF = -1e30

LANES = 128
SUBLANES = 8
SUB_Q = 64
KEY_BLOCK = 128
HALF_SPAN = 64
LSE_LANES = 16
LOG2E = 1.4426950408889634
LN2 = 0.6931471805599453
ROW_TILE = 512
EPI_ROWS = 256
GATHER_STRIDE = 4
GROUP_CHUNKS = GROUP_WIDTH // LANES
VMEM_LIMIT = 56 * 1024 * 1024
MIX_VMEM_LIMIT = 60 * 1024 * 1024

F32 = jnp.float32
BF16 = jnp.bfloat16


def _dot(a, b):
    return jnp.dot(a, b, preferred_element_type=F32)


def _resident(shape):
    nd = len(shape)
    return pl.BlockSpec(shape, lambda *_: (0,) * nd, pipeline_mode=pl.Buffered(1))


def _row_block(width):
    return pl.BlockSpec((None, ROW_TILE, width), lambda b, i: (b, i, 0))


def _group_block(dilation, width, lead=()):
    n_lead = len(lead)
    return pl.BlockSpec(lead + (None, dilation, ROW_TILE // dilation, width),
                        lambda b, i: (0,) * n_lead + (b, 0, i, 0))


def _proj_kernel(x_ref, n1_ref, w_ref, rot_ref, qn_ref, kn_ref,
                 bg_ref, qkv0_ref, qkv1_ref, qkv2_ref, u_ref, g_ref, h_ref, perm_ref, acc_ref,
                 rope_ref, perm2_ref):
    xt = x_ref[...]
    ms = jnp.mean(xt * xt, axis=-1, keepdims=True)
    h_ref[...] = (xt * lax.rsqrt(ms + EPS) * n1_ref[...]).astype(BF16)

    row_blocks = [slice(r, r + EPI_ROWS) for r in range(0, ROW_TILE, EPI_ROWS)]
    chunks = [slice(c * LANES, (c + 1) * LANES) for c in range(GROUP_CHUNKS)]

    half_rot = ROT_DIM // 2
    head_lane = lax.broadcasted_iota(jnp.int32, (1, LANES), 1) % HEAD_DIM
    first_half = head_lane < half_rot
    for rows in row_blocks:
        tab = rot_ref[:, rows].T
        rope_ref[0, rows, :] = jnp.where(first_half, tab,
                                         jnp.where(head_lane < ROT_DIM, pltpu.roll(tab, half_rot, 1), 1.0))
        rope_ref[1, rows, :] = jnp.where(first_half, -pltpu.roll(tab, LANES - half_rot, 1),
                                         jnp.where(head_lane < ROT_DIM, tab, 0.0))

    n_proj = [0]

    def proj(c0):
        slot = n_proj[0] % acc_ref.shape[0]
        n_proj[0] += 1
        acc_ref[slot] = _dot(h_ref[...], w_ref[:, c0:c0 + GROUP_WIDTH])
        return acc_ref.at[slot]

    same_head = (lax.broadcasted_iota(jnp.int32, (LANES, LANES), 0) // HEAD_DIM
                 == lax.broadcasted_iota(jnp.int32, (LANES, LANES), 1) // HEAD_DIM)
    seg = jnp.where(same_head, 1.0 / HEAD_DIM, 0.0).astype(BF16)

    def head_norm_rope(t, nw, rows):
        ms = _dot((t * t).astype(BF16), seg)
        y = t * lax.rsqrt(ms + EPS) * nw
        partner = jnp.where(first_half, pltpu.roll(y, LANES - half_rot, 1), pltpu.roll(y, half_rot, 1))
        return y * rope_ref[0, rows, :] + partner * rope_ref[1, rows, :]

    def emit(kind, group, acc, fn):
        out_ref = (qkv0_ref, qkv1_ref, qkv2_ref)[group]
        dilation = DILATED_PATTERNS[group][1]
        n = ROW_TILE // dilation
        for c, cols in enumerate(chunks):
            for rows in row_blocks:
                val = fn(acc[rows, cols], rows)
                if dilation == 1:
                    out_ref[kind, 0, rows, cols] = val.astype(BF16)
                else:
                    perm_ref[c, rows, :] = val
            if dilation == GATHER_STRIDE:
                for r in range(dilation):
                    out_ref[kind, r, :, cols] = perm_ref[c, pl.ds(r, n, stride=dilation), :].astype(BF16)
            elif dilation == GATHER_STRIDE ** 2:
                q = ROW_TILE // GATHER_STRIDE
                for r1 in range(GATHER_STRIDE):
                    perm2_ref[c, r1 * q:(r1 + 1) * q, :] = perm_ref[c, pl.ds(r1, q, stride=GATHER_STRIDE), :]
                for r1 in range(GATHER_STRIDE):
                    for r2 in range(GATHER_STRIDE):
                        out_ref[kind, GATHER_STRIDE * r2 + r1, :, cols] = perm2_ref[
                            c, pl.ds(r1 * q + r2, n, stride=GATHER_STRIDE), :].astype(BF16)

    def qk_block(kind, j):
        nw_ref, scale = ((qn_ref, HEAD_DIM ** -0.5 * LOG2E), (kn_ref, 1.0))[kind]
        nw = nw_ref[...] * scale
        acc = proj(kind * QKV_WIDTH + j * GROUP_WIDTH)
        emit(kind, j, acc, lambda t, rows: head_norm_rope(t, nw, rows))

    def v_block(j):
        emit(2, j, proj(2 * QKV_WIDTH + j * GROUP_WIDTH), lambda t, rows: t)

    conv0 = 3 * QKV_WIDTH
    gate0 = conv0 + 2 * CONV_CH

    def gate_block(j):
        lo = j * GROUP_WIDTH
        acc = proj(gate0 + lo)
        for rows in row_blocks:
            g_ref[rows, lo:lo + GROUP_WIDTH] = jax.nn.sigmoid(
                acc[rows, :] + bg_ref[:, lo:lo + GROUP_WIDTH])

    for j in range(N_GROUPS):
        qk_block(0, j)
        v_block(j)
        qk_block(1, j)
        gate_block(j)
    a = proj(conv0)
    b = proj(conv0 + CONV_CH)
    for rows in row_blocks:
        u_ref[rows, :] = a[rows, :] * jax.nn.sigmoid(b[rows, :])
    gate_block(N_GROUPS)


def _proj_call(x3, n1, w_in, rot, qn, kn, bg):
    bsz, s, _ = x3.shape
    in_width = w_in.shape[1]
    qkv_shapes = [jax.ShapeDtypeStruct((3, bsz, d, s // d, GROUP_WIDTH), BF16)
                  for _, d in DILATED_PATTERNS]
    return pl.pallas_call(
        _proj_kernel,
        grid=(bsz, s // ROW_TILE),
        in_specs=[_row_block(D_MODEL), _resident((1, D_MODEL)), _resident((D_MODEL, in_width)),
                  pl.BlockSpec((None, LANES, ROW_TILE), lambda b, i: (b, 0, i)),
                  _resident((1, LANES)), _resident((1, LANES)),
                  _resident((1, 2 * D_MODEL))],
        out_specs=[_group_block(d, GROUP_WIDTH, lead=(3,)) for _, d in DILATED_PATTERNS]
        + [_row_block(CONV_CH), _row_block(2 * D_MODEL)],
        out_shape=qkv_shapes + [jax.ShapeDtypeStruct((bsz, s, CONV_CH), F32),
                                jax.ShapeDtypeStruct((bsz, s, 2 * D_MODEL), F32)],
        scratch_shapes=[pltpu.VMEM((ROW_TILE, D_MODEL), BF16),
                        pltpu.VMEM((GROUP_CHUNKS, ROW_TILE, LANES), F32),
                        pltpu.VMEM((3, ROW_TILE, GROUP_WIDTH), F32),
                        pltpu.VMEM((2, ROW_TILE, LANES), F32),
                        pltpu.VMEM((GROUP_CHUNKS, ROW_TILE, LANES), F32)],
        compiler_params=pltpu.CompilerParams(dimension_semantics=("arbitrary", "arbitrary"),
                                             vmem_limit_bytes=VMEM_LIMIT),
        name="proj",
    )(x3, n1, w_in, rot, qn, kn, bg)


def _attn_kernel(q_ref, k_ref, v_ref, o_ref, lse_ref, kpad_ref, vt_ref, bias_ref, s_ref, p_ref,
                 ot_ref, stat_ref):
    n_seq, seq_len, _ = q_ref.shape
    whole = seq_len == KEY_BLOCK
    kw = KEY_BLOCK if whole else 3 * SUB_Q
    n_key_blocks = seq_len // KEY_BLOCK
    n_blocks = n_seq * n_key_blocks
    vt_pad = 0 if whole else 1
    last_sub = seq_len // SUB_Q - 1
    subs = KEY_BLOCK // SUB_Q
    lane = lax.broadcasted_iota(jnp.int32, (1, LANES), 1)
    first_head = lane < HEAD_DIM
    block_diag = (lax.broadcasted_iota(jnp.int32, (LANES, LANES), 0) < HEAD_DIM) == first_head

    @pl.when(pl.program_id(0) == 0)
    def _init():
        row = lax.broadcasted_iota(jnp.int32, (kw, LANES), 0)
        lq = lax.broadcasted_iota(jnp.int32, (kw, LANES), 1) % SUB_Q
        as_bias = lambda valid: jnp.where(valid, 0.0, NEG_INF).astype(F32)
        if whole:
            for case in range(subs):
                bias_ref[case] = as_bias(jnp.abs(lq + case * SUB_Q - row) <= HALF_SPAN)
        else:
            band = jnp.abs(lq + SUB_Q - row) <= HALF_SPAN
            bias_ref[0] = as_bias(band & (row >= SUB_Q))
            bias_ref[1] = as_bias(band)
            bias_ref[2] = as_bias(band & (row < 2 * SUB_Q))
            zeros = jnp.zeros((SUB_Q, GROUP_WIDTH), BF16)
            zero_block = jnp.zeros((GROUP_WIDTH, KEY_BLOCK), BF16)
            for r in range(n_seq):
                kpad_ref[r, 0:SUB_Q, :] = zeros
                kpad_ref[r, SUB_Q + seq_len:2 * SUB_Q + seq_len, :] = zeros
                vt_ref[r, 0] = zero_block
                vt_ref[r, n_key_blocks + 1] = zero_block

    def split(j):
        return j // n_key_blocks, j % n_key_blocks

    if not whole:
        for r in range(n_seq):
            kpad_ref[r, SUB_Q:SUB_Q + seq_len, :] = k_ref[r]

    def stage_vt(j):
        r, b = split(j)
        rows = pl.ds(pl.multiple_of(b * KEY_BLOCK, KEY_BLOCK), KEY_BLOCK)
        for hp in range(GROUP_CHUNKS):
            cols = slice(hp * LANES, (hp + 1) * LANES)
            vt_ref[r, b + vt_pad, cols, :] = v_ref[r, rows, cols].astype(F32).T.astype(BF16)

    def stage1_qk(j, sub_blocks, pairs):
        r, i = split(j)
        for sub in sub_blocks:
            q0 = pl.multiple_of((subs * i + sub) * SUB_Q, SUB_Q)
            for hp in pairs:
                cols = slice(hp * LANES, (hp + 1) * LANES)
                qb = q_ref[r, pl.ds(q0, SUB_Q), cols]
                zq = jnp.zeros_like(qb)
                qv = jnp.concatenate([jnp.where(first_head, qb, zq),
                                      jnp.where(first_head, zq, qb)], axis=0)
                kwin = k_ref[r, :, cols] if whole else kpad_ref[r, pl.ds(q0, kw), cols]
                s_ref[sub * GROUP_CHUNKS + hp] = lax.dot_general(
                    kwin, qv, (((1,), (1,)), ((), ())), preferred_element_type=F32)

    def stage2_softmax(j, sub_blocks, pairs):
        _, i = split(j)
        for sub in sub_blocks:
            t = subs * i + sub
            if whole:
                bias = bias_ref[sub]
            else:
                bias = bias_ref[jnp.where(t == 0, 0, jnp.where(t == last_sub, 2, 1))]
            for hp in pairs:
                unit = sub * GROUP_CHUNKS + hp
                if whole:
                    s = s_ref[unit] + bias
                else:
                    s = jnp.concatenate([s_ref[unit, 0:SUB_Q, :] + bias[0:SUB_Q],
                                         s_ref[unit, SUB_Q:2 * SUB_Q, :],
                                         s_ref[unit, 2 * SUB_Q:kw, :] + bias[2 * SUB_Q:kw]], axis=0)
                m = jnp.max(s, axis=0, keepdims=True)
                p = jnp.exp2(s - m)
                den = jnp.sum(p, axis=0, keepdims=True)
                p_ref[unit] = p.astype(BF16)
                stat_ref[0, unit, 0:1, :] = 1.0 / den
                stat_ref[0, unit, 1:2, :] = (m + jnp.log2(den)) * LN2

    def stage3_pv(j, sub_blocks, pairs):
        r, i = split(j)
        for sub in sub_blocks:
            for hp in pairs:
                unit = sub * GROUP_CHUNKS + hp
                cols = slice(hp * LANES, (hp + 1) * LANES)
                pb = p_ref[unit]
                if whole:
                    ot_ref[unit] = _dot(vt_ref[r, 0, cols, :], pb)
                else:
                    first = i + sub
                    vt2 = jnp.concatenate([vt_ref[r, first, cols, :], vt_ref[r, first + 1, cols, :]], axis=1)
                    zp = jnp.zeros((SUB_Q, LANES), BF16)
                    ot_ref[unit] = _dot(vt2, jnp.concatenate([zp, pb] if sub == 0 else [pb, zp], axis=0))
                stat_ref[1, unit] = stat_ref[0, unit]

    def stage4_finish(j, sub_blocks, pairs):
        r, i = split(j)
        for sub in sub_blocks:
            q0 = pl.multiple_of((subs * i + sub) * SUB_Q, SUB_Q)
            lse_rows = pl.ds(q0, SUB_Q)
            lse_tile = jnp.zeros((SUB_Q, LANES), F32) if pairs[0] == 0 else lse_ref[r, lse_rows, :]
            for hp in pairs:
                unit = sub * GROUP_CHUNKS + hp
                cols = slice(hp * LANES, (hp + 1) * LANES)
                tile = jnp.where(block_diag, ot_ref[unit] * stat_ref[1, unit, 0:1, :],
                                 stat_ref[1, unit, 1:2, :]).T
                top, bot = tile[0:SUB_Q], tile[SUB_Q:2 * SUB_Q]
                o_ref[r, pl.ds(q0, SUB_Q), cols] = jnp.where(first_head, top, bot).astype(BF16)
                lse_tile = jnp.where((lane % HEAD_DIM) // LSE_LANES == hp,
                                     jnp.where(first_head, bot, top), lse_tile)
            lse_ref[r, lse_rows, :] = lse_tile

    def trip(j, stages=(4, 3, 2, 1)):
        pair_groups = [[hp] for hp in range(GROUP_CHUNKS)] if whole else [list(range(GROUP_CHUNKS))]
        for sub in range(subs):
            for pairs in pair_groups:
                if 4 in stages:
                    stage4_finish(j - 2, [sub], pairs)
                if 3 in stages:
                    stage3_pv(j - 1, [sub], pairs)
                if 2 in stages:
                    stage2_softmax(j, [sub], pairs)
                if 1 in stages:
                    stage1_qk(j + 1, [sub], pairs)
        if 1 in stages:
            stage_vt(j + 1)

    trip(-1, stages=(1,))
    trip(0, stages=(2, 1))
    trip(1, stages=(3, 2, 1))

    def steady(j, c):
        trip(j)
        return c

    lax.fori_loop(2, n_blocks - 1, steady, 0)
    trip(n_blocks - 1, stages=(4, 3, 2))
    trip(n_blocks, stages=(4, 3))
    trip(n_blocks + 1, stages=(4,))


def _attn_call(qkv):
    _, bsz, dilation, seq_len, _ = qkv.shape
    whole = seq_len == KEY_BLOCK
    kw = KEY_BLOCK if whole else 3 * SUB_Q
    units = (KEY_BLOCK // SUB_Q) * GROUP_CHUNKS
    in_spec = lambda kind: pl.BlockSpec((None, None, dilation, seq_len, GROUP_WIDTH),
                                        lambda b: (kind, b, 0, 0, 0))
    out_spec = lambda w: pl.BlockSpec((None, dilation, seq_len, w), lambda b: (b, 0, 0, 0))
    return pl.pallas_call(
        _attn_kernel,
        grid=(bsz,),
        in_specs=[in_spec(0), in_spec(1), in_spec(2)],
        out_specs=[out_spec(GROUP_WIDTH), out_spec(LANES)],
        out_shape=[jax.ShapeDtypeStruct((bsz, dilation, seq_len, GROUP_WIDTH), BF16),
                   jax.ShapeDtypeStruct((bsz, dilation, seq_len, LANES), F32)],
        scratch_shapes=[
            pltpu.VMEM((dilation, seq_len + 2 * SUB_Q, GROUP_WIDTH), BF16),
            pltpu.VMEM((dilation, seq_len // KEY_BLOCK + (0 if whole else 2), GROUP_WIDTH, KEY_BLOCK), BF16),
            pltpu.VMEM((2 if whole else 3, kw, LANES), F32),
            pltpu.VMEM((units, kw, LANES), F32),
            pltpu.VMEM((units, kw, LANES), BF16),
            pltpu.VMEM((units, LANES, LANES), F32),
            pltpu.VMEM((2, units, SUBLANES, LANES), F32)],
        compiler_params=pltpu.CompilerParams(dimension_semantics=("arbitrary",),
                                             vmem_limit_bytes=VMEM_LIMIT),
        name=f"attn_d{dilation}",
    )(qkv, qkv, qkv)


CONV_PAD = 16
CONV_ROWS = 64
CONV_CHUNKS = CONV_CH // LANES


def _conv_stage(u_ref, prev_ref, next_ref, has_prev, has_next, pad_ref):
    for c in range(CONV_CHUNKS):
        cols = slice(c * LANES, (c + 1) * LANES)
        pad_ref[c, 0:CONV_PAD, :] = jnp.where(has_prev, prev_ref[:, cols], 0.0)
        pad_ref[c, CONV_PAD:CONV_PAD + ROW_TILE, :] = u_ref[:, cols]
        pad_ref[c, CONV_PAD + ROW_TILE:2 * CONV_PAD + ROW_TILE, :] = jnp.where(
            has_next, next_ref[:, cols], 0.0)


def _conv_ln_rows(r0, w_ref, b_ref, lnw_ref, lnb_ref, out_ref, pad_ref, conv_ref):
    chunks = [slice(c * LANES, (c + 1) * LANES) for c in range(CONV_CHUNKS)]
    first_tap = CONV_PAD - (CONV_WIDTH - 1) // 2
    half_rows = CONV_ROWS // 2
    for c, cols in enumerate(chunks):
        for parity in range(2):
            acc = jnp.zeros((half_rows, LANES), F32) + b_ref[:, cols]
            for t in range(CONV_WIDTH):
                rows = pl.ds(r0 + parity + first_tap + t, half_rows, stride=2)
                acc = acc + pad_ref[c, rows, :] * w_ref[t:t + 1, cols]
            conv_ref[c, pl.ds(r0 + parity, half_rows, stride=2), :] = acc
    ys = [conv_ref[c, r0:r0 + CONV_ROWS, :] for c in range(CONV_CHUNKS)]
    mu = sum(jnp.sum(y, axis=-1, keepdims=True) for y in ys) * (1.0 / CONV_CH)
    cen = [y - mu for y in ys]
    var = sum(jnp.sum(t * t, axis=-1, keepdims=True) for t in cen) * (1.0 / CONV_CH)
    inv = lax.rsqrt(var + EPS)
    for c, cols in enumerate(chunks):
        y = cen[c] * inv * lnw_ref[:, cols] + lnb_ref[:, cols]
        out_ref[r0:r0 + CONV_ROWS, cols] = (y * jax.nn.sigmoid(y)).astype(BF16)
    return y[0:1, :]


def _zero_after(value):
    bits = value.astype(jnp.int32)
    return lax.shift_right_logical(lax.shift_right_logical(bits, 16), 16).astype(F32)


FF_CHUNK = 256


def _mix_ffn_kernel(x_ref, o0_ref, o1_ref, o2_ref, l0_ref, l1_ref, l2_ref, g_ref,
                    uc_ref, up_ref, un_ref, cw_ref, cb_ref, lnw_ref, lnb_ref,
                    wo_ref, wpw_ref, wout_ref, n2_ref, w1_ref, w2_ref,
                    out_ref, attn_ref, act_ref, perm_ref, pad_ref, conv_ref, uln_ref, perm2_ref,
                    *, tiles_per_seq):
    t = pl.program_id(0)
    conv_pos = jnp.minimum(t, pl.num_programs(0) - 2) % tiles_per_seq
    write_slot = t % 2
    read_slot = 1 - write_slot

    @pl.when(t == 0)
    def _init():
        uln_ref[1] = jnp.zeros(uln_ref.shape[1:], BF16)

    _conv_stage(uc_ref, up_ref, un_ref, conv_pos > 0, conv_pos < tiles_per_seq - 1, pad_ref)

    def token_order(ref, group, slab, cols):
        dilation = DILATED_PATTERNS[group][1]
        if dilation == 1:
            return ref[0, :, cols].astype(F32)
        n = ROW_TILE // dilation
        if dilation == GATHER_STRIDE:
            for r in range(dilation):
                perm_ref[slab, pl.ds(r, n, stride=dilation), :] = ref[r, :, cols].astype(F32)
        else:
            stage = slab % perm2_ref.shape[0]
            q = ROW_TILE // GATHER_STRIDE
            for r in range(dilation):
                r2, r1 = divmod(r, GATHER_STRIDE)
                perm2_ref[stage, pl.ds(r1 * q + r2, n, stride=GATHER_STRIDE), :] = ref[r, :, cols].astype(F32)
            for r1 in range(GATHER_STRIDE):
                perm_ref[slab, pl.ds(r1, q, stride=GATHER_STRIDE), :] = perm2_ref[stage, r1 * q:(r1 + 1) * q, :]
        return perm_ref[slab]

    o_refs = (o0_ref, o1_ref, o2_ref)
    all_lanes = slice(0, LANES)
    n_slabs = GROUP_CHUNKS + 1
    lse = [token_order(ref, g, g * n_slabs + GROUP_CHUNKS, all_lanes)
           for g, ref in enumerate((l0_ref, l1_ref, l2_ref))]
    m = jnp.maximum(jnp.maximum(lse[0], lse[1]), lse[2])
    e = [jnp.exp(l - m) for l in lse]
    inv = 1.0 / (e[0] + e[1] + e[2])
    weights = [(e_g * inv).astype(BF16) for e_g in e]
    src_lane = lax.broadcasted_iota(jnp.int32, (LANES, LANES), 0)
    first_head = lax.broadcasted_iota(jnp.int32, (LANES, LANES), 1) < HEAD_DIM
    for c in range(GROUP_CHUNKS):
        cols = slice(c * LANES, (c + 1) * LANES)
        lane_b = LSE_LANES * c
        expand = jnp.where(src_lane == jnp.where(first_head, lane_b + HEAD_DIM, lane_b),
                           1.0, 0.0).astype(BF16)
        acc = jnp.zeros((ROW_TILE, LANES), F32)
        for g in range(N_GROUPS):
            acc = acc + _dot(weights[g], expand) * token_order(o_refs[g], g, g * n_slabs + c, cols)
        attn_ref[:, cols] = acc.astype(BF16)

    y_a = _dot(attn_ref[...], wo_ref[...])
    y_b = _dot(uln_ref[read_slot], wpw_ref[...])
    z = g_ref[:, 0:D_MODEL] * y_a + g_ref[:, D_MODEL:2 * D_MODEL] * y_b
    x1 = x_ref[...] + _dot(z.astype(BF16), wout_ref[...])

    ms = jnp.mean(x1 * x1, axis=-1, keepdims=True)
    h2 = (x1 * lax.rsqrt(ms + EPS) * n2_ref[...]).astype(BF16)
    ff_chunks = range(0, D_FF, FF_CHUNK)
    conv_blocks = list(range(0, ROW_TILE, CONV_ROWS))
    issue_after = [k * (len(ff_chunks) - 1) // len(conv_blocks) for k in range(len(conv_blocks))]
    pending = None
    for idx, n in enumerate(ff_chunks):
        width = min(FF_CHUNK, D_FF - n)
        gt = _dot(h2, w1_ref[:, n:n + width])
        up = _dot(h2, w1_ref[:, D_FF + n:D_FF + n + width])
        act = gt * jax.nn.sigmoid(gt) * up
        if pending is not None:
            act = act + jnp.concatenate([_zero_after(pending)] * (width // LANES), axis=1)
            pending = None
        act_ref[:, n:n + width] = act.astype(BF16)
        for k, r0 in enumerate(conv_blocks):
            if issue_after[k] == idx:
                probe = _conv_ln_rows(r0, cw_ref, cb_ref, lnw_ref, lnb_ref,
                                      uln_ref.at[write_slot], pad_ref, conv_ref)
                pending = probe if pending is None else pending + probe
    out_ref[...] = x1 + _dot(act_ref[...], w2_ref[...])


def _mix_ffn_call(x3, o, lse, gates, u, conv_w, conv_b, ln_w, ln_b, wo, wpw, wout, n2, w1, w2):
    bsz, s, _ = x3.shape
    dilations = [d for _, d in DILATED_PATTERNS]
    tiles_per_seq = s // ROW_TILE
    n_tiles = bsz * tiles_per_seq
    halo_blocks = ROW_TILE // CONV_PAD

    def finish_tile(t):
        tile = jnp.maximum(t - 1, 0)
        return tile // tiles_per_seq, tile % tiles_per_seq

    def conv_tile(t):
        tile = jnp.minimum(t, n_tiles - 1)
        return tile // tiles_per_seq, tile % tiles_per_seq

    def row(width):
        return pl.BlockSpec((None, ROW_TILE, width), lambda t: (*finish_tile(t), 0))

    def group(dilation, width):
        return pl.BlockSpec((None, dilation, ROW_TILE // dilation, width),
                            lambda t: (finish_tile(t)[0], 0, finish_tile(t)[1], 0))

    def halo(offset):
        def index(t):
            b, i = conv_tile(t)
            block = i * halo_blocks - 1 if offset < 0 else (i + 1) * halo_blocks
            return b, jnp.clip(block, 0, s // CONV_PAD - 1), 0
        return pl.BlockSpec((None, CONV_PAD, CONV_CH), index)

    return pl.pallas_call(
        functools.partial(_mix_ffn_kernel, tiles_per_seq=tiles_per_seq),
        grid=(n_tiles + 1,),
        in_specs=[row(D_MODEL)]
        + [group(d, GROUP_WIDTH) for d in dilations]
        + [group(d, LANES) for d in dilations]
        + [row(2 * D_MODEL),
           pl.BlockSpec((None, ROW_TILE, CONV_CH), lambda t: (*conv_tile(t), 0)), halo(-1), halo(1),
           _resident(conv_w.shape), _resident(conv_b.shape), _resident(ln_w.shape),
           _resident(ln_b.shape),
           _resident(wo.shape), _resident(wpw.shape),
           _resident(wout.shape), _resident(n2.shape), _resident(w1.shape), _resident(w2.shape)],
        out_specs=row(D_MODEL),
        out_shape=jax.ShapeDtypeStruct((bsz, s, D_MODEL), F32),
        scratch_shapes=[pltpu.VMEM((ROW_TILE, GROUP_WIDTH), BF16),
                        pltpu.VMEM((ROW_TILE, D_FF), BF16),
                        pltpu.VMEM((N_GROUPS * (GROUP_CHUNKS + 1), ROW_TILE, LANES), F32),
                        pltpu.VMEM((CONV_CHUNKS, ROW_TILE + 2 * CONV_PAD, LANES), F32),
                        pltpu.VMEM((CONV_CHUNKS, ROW_TILE, LANES), F32),
                        pltpu.VMEM((2, ROW_TILE, CONV_CH), BF16),
                        pltpu.VMEM((GROUP_CHUNKS + 1, ROW_TILE, LANES), F32)],
        compiler_params=pltpu.CompilerParams(dimension_semantics=("arbitrary",),
                                             vmem_limit_bytes=MIX_VMEM_LIMIT),
        name="mix_ffn",
    )(x3, *o, *lse, gates, u, u, u, conv_w, conv_b, ln_w, ln_b, wo, wpw, wout, n2, w1, w2)


def _rope_table(positions):
    inv_freq = ROPE_THETA ** (-jnp.arange(0, ROT_DIM, 2, dtype=F32) / ROT_DIM)
    ang = positions.astype(F32)[:, None, :] * inv_freq[None, :, None]
    rest = jnp.zeros((ang.shape[0], HEAD_DIM - ROT_DIM, ang.shape[2]), F32)
    return jnp.concatenate([jnp.cos(ang), jnp.sin(ang), rest] * (LANES // HEAD_DIM), axis=1)


def kernel(x, positions, norm1_w, w_in, b_gate, q_norm_w, k_norm_w, w_o_attn, conv_w, conv_b,
           conv_ln_w, conv_ln_b, w_pw_conv, w_out, norm2_w, w_ffn_in, w_ffn_out):
    d = x.shape[-1]
    depth = norm1_w.shape[0]
    rot = _rope_table(positions)
    pair = lambda w: jnp.tile(w.astype(F32), LANES // HEAD_DIM).reshape(1, LANES)

    for l in range(depth):
        qkv0, qkv1, qkv2, u, gates = _proj_call(
            x, norm1_w[l].reshape(1, d), w_in[l].astype(BF16), rot,
            pair(q_norm_w[l]), pair(k_norm_w[l]), b_gate[l].reshape(1, 2 * d))
        outs, lses = zip(*[_attn_call(qkv) for qkv in (qkv0, qkv1, qkv2)])
        x = _mix_ffn_call(
            x, outs, lses, gates, u, conv_w[l], conv_b[l].reshape(1, CONV_CH),
            conv_ln_w[l].reshape(1, CONV_CH), conv_ln_b[l].reshape(1, CONV_CH),
            w_o_attn[l].astype(BF16), w_pw_conv[l].astype(BF16), w_out[l].astype(BF16),
            norm2_w[l].reshape(1, d), w_ffn_in[l].astype(BF16), w_ffn_out[l].astype(BF16))
    return x
```

```python
import functools

import jax
import jax.numpy as jnp
from jax import lax
from jax.experimental import pallas as pl
from jax.experimental.pallas import tpu as pltpu

D_MODEL = 1024
HEAD_DIM = 64
N_SLOT_HEADS = 8
DILATED_PATTERNS = ((128, 1), (512, 4), (2048, 16))
N_GROUPS = len(DILATED_PATTERNS)
GROUP_WIDTH = N_SLOT_HEADS * HEAD_DIM
QKV_WIDTH = N_GROUPS * GROUP_WIDTH
ROPE_THETA = 500000.0
ROT_DIM = HEAD_DIM // 4
CONV_CH = D_MODEL // 2
CONV_WIDTH = 31
D_FF = 2816
EPS = 1e-6
NEG_INF = -1e30

LANES = 128
SUBLANES = 8
SUB_Q = 64
KEY_BLOCK = 128
HALF_SPAN = 64
LSE_LANES = 16
LOG2E = 1.4426950408889634
LN2 = 0.6931471805599453
ROW_TILE = 512
EPI_ROWS = 256
GATHER_STRIDE = 4
GROUP_CHUNKS = GROUP_WIDTH // LANES
VMEM_LIMIT = 56 * 1024 * 1024
MIX_VMEM_LIMIT = 60 * 1024 * 1024

F32 = jnp.float32
BF16 = jnp.bfloat16


def _dot(a, b):
    return jnp.dot(a, b, preferred_element_type=F32)


def _resident(shape):
    nd = len(shape)
    return pl.BlockSpec(shape, lambda *_: (0,) * nd, pipeline_mode=pl.Buffered(1))


def _row_block(width):
    return pl.BlockSpec((None, ROW_TILE, width), lambda b, i: (b, i, 0))


def _group_block(dilation, width, lead=()):
    n_lead = len(lead)
    return pl.BlockSpec(lead + (None, dilation, ROW_TILE // dilation, width),
                        lambda b, i: (0,) * n_lead + (b, 0, i, 0))


def _proj_kernel(x_ref, n1_ref, w_ref, rot_ref, qn_ref, kn_ref,
                 bg_ref, qkv0_ref, qkv1_ref, qkv2_ref, u_ref, g_ref, h_ref, perm_ref, acc_ref,
                 rope_ref, perm2_ref):
    xt = x_ref[...]
    ms = jnp.mean(xt * xt, axis=-1, keepdims=True)
    h_ref[...] = (xt * lax.rsqrt(ms + EPS) * n1_ref[...]).astype(BF16)

    row_blocks = [slice(r, r + EPI_ROWS) for r in range(0, ROW_TILE, EPI_ROWS)]
    chunks = [slice(c * LANES, (c + 1) * LANES) for c in range(GROUP_CHUNKS)]

    half_rot = ROT_DIM // 2
    head_lane = lax.broadcasted_iota(jnp.int32, (1, LANES), 1) % HEAD_DIM
    first_half = head_lane < half_rot
    for rows in row_blocks:
        tab = rot_ref[:, rows].T
        rope_ref[0, rows, :] = jnp.where(first_half, tab,
                                         jnp.where(head_lane < ROT_DIM, pltpu.roll(tab, half_rot, 1), 1.0))
        rope_ref[1, rows, :] = jnp.where(first_half, -pltpu.roll(tab, LANES - half_rot, 1),
                                         jnp.where(head_lane < ROT_DIM, tab, 0.0))

    n_proj = [0]

    def proj(c0):
        slot = n_proj[0] % acc_ref.shape[0]
        n_proj[0] += 1
        acc_ref[slot] = _dot(h_ref[...], w_ref[:, c0:c0 + GROUP_WIDTH])
        return acc_ref.at[slot]

    same_head = (lax.broadcasted_iota(jnp.int32, (LANES, LANES), 0) // HEAD_DIM
                 == lax.broadcasted_iota(jnp.int32, (LANES, LANES), 1) // HEAD_DIM)
    seg = jnp.where(same_head, 1.0 / HEAD_DIM, 0.0).astype(BF16)

    def head_norm_rope(t, nw, rows):
        ms = _dot((t * t).astype(BF16), seg)
        y = t * lax.rsqrt(ms + EPS) * nw
        partner = jnp.where(first_half, pltpu.roll(y, LANES - half_rot, 1), pltpu.roll(y, half_rot, 1))
        return y * rope_ref[0, rows, :] + partner * rope_ref[1, rows, :]

    def emit(kind, group, acc, fn):
        out_ref = (qkv0_ref, qkv1_ref, qkv2_ref)[group]
        dilation = DILATED_PATTERNS[group][1]
        n = ROW_TILE // dilation
        for c, cols in enumerate(chunks):
            for rows in row_blocks:
                val = fn(acc[rows, cols], rows)
                if dilation == 1:
                    out_ref[kind, 0, rows, cols] = val.astype(BF16)
                else:
                    perm_ref[c, rows, :] = val
            if dilation == GATHER_STRIDE:
                for r in range(dilation):
                    out_ref[kind, r, :, cols] = perm_ref[c, pl.ds(r, n, stride=dilation), :].astype(BF16)
            elif dilation == GATHER_STRIDE ** 2:
                q = ROW_TILE // GATHER_STRIDE
                for r1 in range(GATHER_STRIDE):
                    perm2_ref[c, r1 * q:(r1 + 1) * q, :] = perm_ref[c, pl.ds(r1, q, stride=GATHER_STRIDE), :]
                for r1 in range(GATHER_STRIDE):
                    for r2 in range(GATHER_STRIDE):
                        out_ref[kind, GATHER_STRIDE * r2 + r1, :, cols] = perm2_ref[
                            c, pl.ds(r1 * q + r2, n, stride=GATHER_STRIDE), :].astype(BF16)

    def qk_block(kind, j):
        nw_ref, scale = ((qn_ref, HEAD_DIM ** -0.5 * LOG2E), (kn_ref, 1.0))[kind]
        nw = nw_ref[...] * scale
        acc = proj(kind * QKV_WIDTH + j * GROUP_WIDTH)
        emit(kind, j, acc, lambda t, rows: head_norm_rope(t, nw, rows))

    def v_block(j):
        emit(2, j, proj(2 * QKV_WIDTH + j * GROUP_WIDTH), lambda t, rows: t)

    conv0 = 3 * QKV_WIDTH
    gate0 = conv0 + 2 * CONV_CH

    def gate_block(j):
        lo = j * GROUP_WIDTH
        acc = proj(gate0 + lo)
        for rows in row_blocks:
            g_ref[rows, lo:lo + GROUP_WIDTH] = jax.nn.sigmoid(
                acc[rows, :] + bg_ref[:, lo:lo + GROUP_WIDTH])

    for j in range(N_GROUPS):
        qk_block(0, j)
        v_block(j)
        qk_block(1, j)
        gate_block(j)
    a = proj(conv0)
    b = proj(conv0 + CONV_CH)
    for rows in row_blocks:
        u_ref[rows, :] = a[rows, :] * jax.nn.sigmoid(b[rows, :])
    gate_block(N_GROUPS)


def _proj_call(x3, n1, w_in, rot, qn, kn, bg):
    bsz, s, _ = x3.shape
    in_width = w_in.shape[1]
    qkv_shapes = [jax.ShapeDtypeStruct((3, bsz, d, s // d, GROUP_WIDTH), BF16)
                  for _, d in DILATED_PATTERNS]
    return pl.pallas_call(
        _proj_kernel,
        grid=(bsz, s // ROW_TILE),
        in_specs=[_row_block(D_MODEL), _resident((1, D_MODEL)), _resident((D_MODEL, in_width)),
                  pl.BlockSpec((None, LANES, ROW_TILE), lambda b, i: (b, 0, i)),
                  _resident((1, LANES)), _resident((1, LANES)),
                  _resident((1, 2 * D_MODEL))],
        out_specs=[_group_block(d, GROUP_WIDTH, lead=(3,)) for _, d in DILATED_PATTERNS]
        + [_row_block(CONV_CH), _row_block(2 * D_MODEL)],
        out_shape=qkv_shapes + [jax.ShapeDtypeStruct((bsz, s, CONV_CH), F32),
                                jax.ShapeDtypeStruct((bsz, s, 2 * D_MODEL), F32)],
        scratch_shapes=[pltpu.VMEM((ROW_TILE, D_MODEL), BF16),
                        pltpu.VMEM((GROUP_CHUNKS, ROW_TILE, LANES), F32),
                        pltpu.VMEM((3, ROW_TILE, GROUP_WIDTH), F32),
                        pltpu.VMEM((2, ROW_TILE, LANES), F32),
                        pltpu.VMEM((GROUP_CHUNKS, ROW_TILE, LANES), F32)],
        compiler_params=pltpu.CompilerParams(dimension_semantics=("arbitrary", "arbitrary"),
                                             vmem_limit_bytes=VMEM_LIMIT),
        name="proj",
    )(x3, n1, w_in, rot, qn, kn, bg)


def _attn_kernel(q_ref, k_ref, v_ref, o_ref, lse_ref, kpad_ref, vt_ref, bias_ref, s_ref, p_ref,
                 ot_ref, stat_ref):
    n_seq, seq_len, _ = q_ref.shape
    whole = seq_len == KEY_BLOCK
    kw = KEY_BLOCK if whole else 3 * SUB_Q
    n_key_blocks = seq_len // KEY_BLOCK
    n_blocks = n_seq * n_key_blocks
    vt_pad = 0 if whole else 1
    last_sub = seq_len // SUB_Q - 1
    subs = KEY_BLOCK // SUB_Q
    lane = lax.broadcasted_iota(jnp.int32, (1, LANES), 1)
    first_head = lane < HEAD_DIM
    block_diag = (lax.broadcasted_iota(jnp.int32, (LANES, LANES), 0) < HEAD_DIM) == first_head

    @pl.when(pl.program_id(0) == 0)
    def _init():
        row = lax.broadcasted_iota(jnp.int32, (kw, LANES), 0)
        lq = lax.broadcasted_iota(jnp.int32, (kw, LANES), 1) % SUB_Q
        as_bias = lambda valid: jnp.where(valid, 0.0, NEG_INF).astype(F32)
        if whole:
            for case in range(subs):
                bias_ref[case] = as_bias(jnp.abs(lq + case * SUB_Q - row) <= HALF_SPAN)
        else:
            band = jnp.abs(lq + SUB_Q - row) <= HALF_SPAN
            bias_ref[0] = as_bias(band & (row >= SUB_Q))
            bias_ref[1] = as_bias(band)
            bias_ref[2] = as_bias(band & (row < 2 * SUB_Q))
            zeros = jnp.zeros((SUB_Q, GROUP_WIDTH), BF16)
            zero_block = jnp.zeros((GROUP_WIDTH, KEY_BLOCK), BF16)
            for r in range(n_seq):
                kpad_ref[r, 0:SUB_Q, :] = zeros
                kpad_ref[r, SUB_Q + seq_len:2 * SUB_Q + seq_len, :] = zeros
                vt_ref[r, 0] = zero_block
                vt_ref[r, n_key_blocks + 1] = zero_block

    def split(j):
        return j // n_key_blocks, j % n_key_blocks

    if not whole:
        for r in range(n_seq):
            kpad_ref[r, SUB_Q:SUB_Q + seq_len, :] = k_ref[r]

    def stage_vt(j):
        r, b = split(j)
        rows = pl.ds(pl.multiple_of(b * KEY_BLOCK, KEY_BLOCK), KEY_BLOCK)
        for hp in range(GROUP_CHUNKS):
            cols = slice(hp * LANES, (hp + 1) * LANES)
            vt_ref[r, b + vt_pad, cols, :] = v_ref[r, rows, cols].T

    def stage1_qk(j, sub_blocks, pairs):
        r, i = split(j)
        for sub in sub_blocks:
            q0 = pl.multiple_of((subs * i + sub) * SUB_Q, SUB_Q)
            for hp in pairs:
                cols = slice(hp * LANES, (hp + 1) * LANES)
                qb = q_ref[r, pl.ds(q0, SUB_Q), cols]
                zq = jnp.zeros_like(qb)
                qv = jnp.concatenate([jnp.where(first_head, qb, zq),
                                      jnp.where(first_head, zq, qb)], axis=0)
                kwin = k_ref[r, :, cols] if whole else kpad_ref[r, pl.ds(q0, kw), cols]
                s_ref[sub * GROUP_CHUNKS + hp] = lax.dot_general(
                    kwin, qv, (((1,), (1,)), ((), ())), preferred_element_type=F32)

    def stage2_softmax(j, sub_blocks, pairs):
        _, i = split(j)
        for sub in sub_blocks:
            t = subs * i + sub
            if whole:
                bias = bias_ref[sub]
            else:
                bias = bias_ref[jnp.where(t == 0, 0, jnp.where(t == last_sub, 2, 1))]
            for hp in pairs:
                unit = sub * GROUP_CHUNKS + hp
                if whole:
                    s = s_ref[unit] + bias
                else:
                    s = jnp.concatenate([s_ref[unit, 0:SUB_Q, :] + bias[0:SUB_Q],
                                         s_ref[unit, SUB_Q:2 * SUB_Q, :],
                                         s_ref[unit, 2 * SUB_Q:kw, :] + bias[2 * SUB_Q:kw]], axis=0)
                m = jnp.max(s, axis=0, keepdims=True)
                p = jnp.exp2(s - m)
                den = jnp.sum(p, axis=0, keepdims=True)
                p_ref[unit] = p.astype(BF16)
                stat_ref[0, unit, 0:1, :] = 1.0 / den
                stat_ref[0, unit, 1:2, :] = (m + jnp.log2(den)) * LN2

    def stage3_pv(j, sub_blocks, pairs):
        r, i = split(j)
        for sub in sub_blocks:
            for hp in pairs:
                unit = sub * GROUP_CHUNKS + hp
                cols = slice(hp * LANES, (hp + 1) * LANES)
                pb = p_ref[unit]
                if whole:
                    ot_ref[unit] = _dot(vt_ref[r, 0, cols, :], pb)
                else:
                    first = i + sub
                    vt2 = jnp.concatenate([vt_ref[r, first, cols, :], vt_ref[r, first + 1, cols, :]], axis=1)
                    zp = jnp.zeros((SUB_Q, LANES), BF16)
                    ot_ref[unit] = _dot(vt2, jnp.concatenate([zp, pb] if sub == 0 else [pb, zp], axis=0))
                stat_ref[1, unit] = stat_ref[0, unit]

    def stage4_finish(j, sub_blocks, pairs):
        r, i = split(j)
        for sub in sub_blocks:
            q0 = pl.multiple_of((subs * i + sub) * SUB_Q, SUB_Q)
            lse_rows = pl.ds(q0, SUB_Q)
            lse_tile = jnp.zeros((SUB_Q, LANES), F32) if pairs[0] == 0 else lse_ref[r, lse_rows, :]
            for hp in pairs:
                unit = sub * GROUP_CHUNKS + hp
                cols = slice(hp * LANES, (hp + 1) * LANES)
                tile = jnp.where(block_diag, ot_ref[unit] * stat_ref[1, unit, 0:1, :],
                                 stat_ref[1, unit, 1:2, :]).T
                top, bot = tile[0:SUB_Q], tile[SUB_Q:2 * SUB_Q]
                o_ref[r, pl.ds(q0, SUB_Q), cols] = jnp.where(first_head, top, bot).astype(BF16)
                lse_tile = jnp.where((lane % HEAD_DIM) // LSE_LANES == hp,
                                     jnp.where(first_head, bot, top), lse_tile)
            lse_ref[r, lse_rows, :] = lse_tile

    def trip(j, stages=(4, 3, 2, 1)):
        pair_groups = [[hp] for hp in range(GROUP_CHUNKS)] if whole else [list(range(GROUP_CHUNKS))]
        for sub in range(subs):
            for pairs in pair_groups:
                if 4 in stages:
                    stage4_finish(j - 2, [sub], pairs)
                if 3 in stages:
                    stage3_pv(j - 1, [sub], pairs)
                if 2 in stages:
                    stage2_softmax(j, [sub], pairs)
                if 1 in stages:
                    stage1_qk(j + 1, [sub], pairs)
        if 1 in stages:
            stage_vt(j + 1)

    trip(-1, stages=(1,))
    trip(0, stages=(2, 1))
    trip(1, stages=(3, 2, 1))

    def steady(j, c):
        trip(j)
        return c

    lax.fori_loop(2, n_blocks - 1, steady, 0)
    trip(n_blocks - 1, stages=(4, 3, 2))
    trip(n_blocks, stages=(4, 3))
    trip(n_blocks + 1, stages=(4,))


def _attn_call(qkv):
    _, bsz, dilation, seq_len, _ = qkv.shape
    whole = seq_len == KEY_BLOCK
    kw = KEY_BLOCK if whole else 3 * SUB_Q
    units = (KEY_BLOCK // SUB_Q) * GROUP_CHUNKS
    in_spec = lambda kind: pl.BlockSpec((None, None, dilation, seq_len, GROUP_WIDTH),
                                        lambda b: (kind, b, 0, 0, 0))
    out_spec = lambda w: pl.BlockSpec((None, dilation, seq_len, w), lambda b: (b, 0, 0, 0))
    return pl.pallas_call(
        _attn_kernel,
        grid=(bsz,),
        in_specs=[in_spec(0), in_spec(1), in_spec(2)],
        out_specs=[out_spec(GROUP_WIDTH), out_spec(LANES)],
        out_shape=[jax.ShapeDtypeStruct((bsz, dilation, seq_len, GROUP_WIDTH), BF16),
                   jax.ShapeDtypeStruct((bsz, dilation, seq_len, LANES), F32)],
        scratch_shapes=[
            pltpu.VMEM((dilation, seq_len + 2 * SUB_Q, GROUP_WIDTH), BF16),
            pltpu.VMEM((dilation, seq_len // KEY_BLOCK + (0 if whole else 2), GROUP_WIDTH, KEY_BLOCK), BF16),
            pltpu.VMEM((2 if whole else 3, kw, LANES), F32),
            pltpu.VMEM((units, kw, LANES), F32),
            pltpu.VMEM((units, kw, LANES), BF16),
            pltpu.VMEM((units, LANES, LANES), F32),
            pltpu.VMEM((2, units, SUBLANES, LANES), F32)],
        compiler_params=pltpu.CompilerParams(dimension_semantics=("arbitrary",),
                                             vmem_limit_bytes=VMEM_LIMIT),
        name=f"attn_d{dilation}",
    )(qkv, qkv, qkv)


CONV_PAD = 16
CONV_ROWS = 64
CONV_CHUNKS = CONV_CH // LANES


def _conv_stage(u_ref, prev_ref, next_ref, has_prev, has_next, pad_ref):
    for c in range(CONV_CHUNKS):
        cols = slice(c * LANES, (c + 1) * LANES)
        pad_ref[c, 0:CONV_PAD, :] = jnp.where(has_prev, prev_ref[:, cols], 0.0)
        pad_ref[c, CONV_PAD:CONV_PAD + ROW_TILE, :] = u_ref[:, cols]
        pad_ref[c, CONV_PAD + ROW_TILE:2 * CONV_PAD + ROW_TILE, :] = jnp.where(
            has_next, next_ref[:, cols], 0.0)


def _conv_ln_rows(r0, w_ref, b_ref, lnw_ref, lnb_ref, out_ref, pad_ref, conv_ref):
    chunks = [slice(c * LANES, (c + 1) * LANES) for c in range(CONV_CHUNKS)]
    first_tap = CONV_PAD - (CONV_WIDTH - 1) // 2
    half_rows = CONV_ROWS // 2
    for c, cols in enumerate(chunks):
        for parity in range(2):
            acc = jnp.zeros((half_rows, LANES), F32) + b_ref[:, cols]
            for t in range(CONV_WIDTH):
                rows = pl.ds(r0 + parity + first_tap + t, half_rows, stride=2)
                acc = acc + pad_ref[c, rows, :] * w_ref[t:t + 1, cols]
            conv_ref[c, pl.ds(r0 + parity, half_rows, stride=2), :] = acc
    ys = [conv_ref[c, r0:r0 + CONV_ROWS, :] for c in range(CONV_CHUNKS)]
    mu = sum(jnp.sum(y, axis=-1, keepdims=True) for y in ys) * (1.0 / CONV_CH)
    cen = [y - mu for y in ys]
    var = sum(jnp.sum(t * t, axis=-1, keepdims=True) for t in cen) * (1.0 / CONV_CH)
    inv = lax.rsqrt(var + EPS)
    for c, cols in enumerate(chunks):
        y = cen[c] * inv * lnw_ref[:, cols] + lnb_ref[:, cols]
        out_ref[r0:r0 + CONV_ROWS, cols] = (y * jax.nn.sigmoid(y)).astype(BF16)
    return y[0:1, :]


def _zero_after(value):
    bits = value.astype(jnp.int32)
    return lax.shift_right_logical(lax.shift_right_logical(bits, 16), 16).astype(F32)


FF_CHUNK = 256


def _mix_ffn_kernel(x_ref, o0_ref, o1_ref, o2_ref, l0_ref, l1_ref, l2_ref, g_ref,
                    uc_ref, up_ref, un_ref, cw_ref, cb_ref, lnw_ref, lnb_ref,
                    wo_ref, wpw_ref, wout_ref, n2_ref, w1_ref, w2_ref,
                    out_ref, attn_ref, act_ref, perm_ref, pad_ref, conv_ref, uln_ref, perm2_ref,
                    *, tiles_per_seq):
    t = pl.program_id(0)
    conv_pos = jnp.minimum(t, pl.num_programs(0) - 2) % tiles_per_seq
    write_slot = t % 2
    read_slot = 1 - write_slot

    @pl.when(t == 0)
    def _init():
        uln_ref[1] = jnp.zeros(uln_ref.shape[1:], BF16)

    _conv_stage(uc_ref, up_ref, un_ref, conv_pos > 0, conv_pos < tiles_per_seq - 1, pad_ref)

    def token_order(ref, group, slab, cols):
        dilation = DILATED_PATTERNS[group][1]
        if dilation == 1:
            return ref[0, :, cols].astype(F32)
        n = ROW_TILE // dilation
        if dilation == GATHER_STRIDE:
            for r in range(dilation):
                perm_ref[slab, pl.ds(r, n, stride=dilation), :] = ref[r, :, cols].astype(F32)
        else:
            stage = slab % perm2_ref.shape[0]
            q = ROW_TILE // GATHER_STRIDE
            for r in range(dilation):
                r2, r1 = divmod(r, GATHER_STRIDE)
                perm2_ref[stage, pl.ds(r1 * q + r2, n, stride=GATHER_STRIDE), :] = ref[r, :, cols].astype(F32)
            for r1 in range(GATHER_STRIDE):
                perm_ref[slab, pl.ds(r1, q, stride=GATHER_STRIDE), :] = perm2_ref[stage, r1 * q:(r1 + 1) * q, :]
        return perm_ref[slab]

    o_refs = (o0_ref, o1_ref, o2_ref)
    all_lanes = slice(0, LANES)
    n_slabs = GROUP_CHUNKS + 1
    lse = [token_order(ref, g, g * n_slabs + GROUP_CHUNKS, all_lanes)
           for g, ref in enumerate((l0_ref, l1_ref, l2_ref))]
    m = jnp.maximum(jnp.maximum(lse[0], lse[1]), lse[2])
    e = [jnp.exp(l - m) for l in lse]
    inv = 1.0 / (e[0] + e[1] + e[2])
    weights = [(e_g * inv).astype(BF16) for e_g in e]
    src_lane = lax.broadcasted_iota(jnp.int32, (LANES, LANES), 0)
    first_head = lax.broadcasted_iota(jnp.int32, (LANES, LANES), 1) < HEAD_DIM
    for c in range(GROUP_CHUNKS):
        cols = slice(c * LANES, (c + 1) * LANES)
        lane_b = LSE_LANES * c
        expand = jnp.where(src_lane == jnp.where(first_head, lane_b + HEAD_DIM, lane_b),
                           1.0, 0.0).astype(BF16)
        acc = jnp.zeros((ROW_TILE, LANES), F32)
        for g in range(N_GROUPS):
            acc = acc + _dot(weights[g], expand) * token_order(o_refs[g], g, g * n_slabs + c, cols)
        attn_ref[:, cols] = acc.astype(BF16)

    y_a = _dot(attn_ref[...], wo_ref[...])
    y_b = _dot(uln_ref[read_slot], wpw_ref[...])
    z = g_ref[:, 0:D_MODEL] * y_a + g_ref[:, D_MODEL:2 * D_MODEL] * y_b
    x1 = x_ref[...] + _dot(z.astype(BF16), wout_ref[...])

    ms = jnp.mean(x1 * x1, axis=-1, keepdims=True)
    h2 = (x1 * lax.rsqrt(ms + EPS) * n2_ref[...]).astype(BF16)
    ff_chunks = range(0, D_FF, FF_CHUNK)
    conv_blocks = list(range(0, ROW_TILE, CONV_ROWS))
    issue_after = [k * (len(ff_chunks) - 1) // len(conv_blocks) for k in range(len(conv_blocks))]
    pending = None
    for idx, n in enumerate(ff_chunks):
        width = min(FF_CHUNK, D_FF - n)
        gt = _dot(h2, w1_ref[:, n:n + width])
        up = _dot(h2, w1_ref[:, D_FF + n:D_FF + n + width])
        act = gt * jax.nn.sigmoid(gt) * up
        if pending is not None:
            act = act + jnp.concatenate([_zero_after(pending)] * (width // LANES), axis=1)
            pending = None
        act_ref[:, n:n + width] = act.astype(BF16)
        for k, r0 in enumerate(conv_blocks):
            if issue_after[k] == idx:
                probe = _conv_ln_rows(r0, cw_ref, cb_ref, lnw_ref, lnb_ref,
                                      uln_ref.at[write_slot], pad_ref, conv_ref)
                pending = probe if pending is None else pending + probe
    out_ref[...] = x1 + _dot(act_ref[...], w2_ref[...])


def _mix_ffn_call(x3, o, lse, gates, u, conv_w, conv_b, ln_w, ln_b, wo, wpw, wout, n2, w1, w2):
    bsz, s, _ = x3.shape
    dilations = [d for _, d in DILATED_PATTERNS]
    tiles_per_seq = s // ROW_TILE
    n_tiles = bsz * tiles_per_seq
    halo_blocks = ROW_TILE // CONV_PAD

    def finish_tile(t):
        tile = jnp.maximum(t - 1, 0)
        return tile // tiles_per_seq, tile % tiles_per_seq

    def conv_tile(t):
        tile = jnp.minimum(t, n_tiles - 1)
        return tile // tiles_per_seq, tile % tiles_per_seq

    def row(width):
        return pl.BlockSpec((None, ROW_TILE, width), lambda t: (*finish_tile(t), 0))

    def group(dilation, width):
        return pl.BlockSpec((None, dilation, ROW_TILE // dilation, width),
                            lambda t: (finish_tile(t)[0], 0, finish_tile(t)[1], 0))

    def halo(offset):
        def index(t):
            b, i = conv_tile(t)
            block = i * halo_blocks - 1 if offset < 0 else (i + 1) * halo_blocks
            return b, jnp.clip(block, 0, s // CONV_PAD - 1), 0
        return pl.BlockSpec((None, CONV_PAD, CONV_CH), index)

    return pl.pallas_call(
        functools.partial(_mix_ffn_kernel, tiles_per_seq=tiles_per_seq),
        grid=(n_tiles + 1,),
        in_specs=[row(D_MODEL)]
        + [group(d, GROUP_WIDTH) for d in dilations]
        + [group(d, LANES) for d in dilations]
        + [row(2 * D_MODEL),
           pl.BlockSpec((None, ROW_TILE, CONV_CH), lambda t: (*conv_tile(t), 0)), halo(-1), halo(1),
           _resident(conv_w.shape), _resident(conv_b.shape), _resident(ln_w.shape),
           _resident(ln_b.shape),
           _resident(wo.shape), _resident(wpw.shape),
           _resident(wout.shape), _resident(n2.shape), _resident(w1.shape), _resident(w2.shape)],
        out_specs=row(D_MODEL),
        out_shape=jax.ShapeDtypeStruct((bsz, s, D_MODEL), F32),
        scratch_shapes=[pltpu.VMEM((ROW_TILE, GROUP_WIDTH), BF16),
                        pltpu.VMEM((ROW_TILE, D_FF), BF16),
                        pltpu.VMEM((N_GROUPS * (GROUP_CHUNKS + 1), ROW_TILE, LANES), F32),
                        pltpu.VMEM((CONV_CHUNKS, ROW_TILE + 2 * CONV_PAD, LANES), F32),
                        pltpu.VMEM((CONV_CHUNKS, ROW_TILE, LANES), F32),
                        pltpu.VMEM((2, ROW_TILE, CONV_CH), BF16),
                        pltpu.VMEM((GROUP_CHUNKS + 1, ROW_TILE, LANES), F32)],
        compiler_params=pltpu.CompilerParams(dimension_semantics=("arbitrary",),
                                             vmem_limit_bytes=MIX_VMEM_LIMIT),
        name="mix_ffn",
    )(x3, *o, *lse, gates, u, u, u, conv_w, conv_b, ln_w, ln_b, wo, wpw, wout, n2, w1, w2)


def _rope_table(positions):
    inv_freq = ROPE_THETA ** (-jnp.arange(0, ROT_DIM, 2, dtype=F32) / ROT_DIM)
    ang = positions.astype(F32)[:, None, :] * inv_freq[None, :, None]
    rest = jnp.zeros((ang.shape[0], HEAD_DIM - ROT_DIM, ang.shape[2]), F32)
    return jnp.concatenate([jnp.cos(ang), jnp.sin(ang), rest] * (LANES // HEAD_DIM), axis=1)


def kernel(x, positions, norm1_w, w_in, b_gate, q_norm_w, k_norm_w, w_o_attn, conv_w, conv_b,
           conv_ln_w, conv_ln_b, w_pw_conv, w_out, norm2_w, w_ffn_in, w_ffn_out):
    d = x.shape[-1]
    depth = norm1_w.shape[0]
    rot = _rope_table(positions)
    pair = lambda w: jnp.tile(w.astype(F32), LANES // HEAD_DIM).reshape(1, LANES)

    for l in range(depth):
        qkv0, qkv1, qkv2, u, gates = _proj_call(
            x, norm1_w[l].reshape(1, d), w_in[l].astype(BF16), rot,
            pair(q_norm_w[l]), pair(k_norm_w[l]), b_gate[l].reshape(1, 2 * d))
        outs, lses = zip(*[_attn_call(qkv) for qkv in (qkv0, qkv1, qkv2)])
        x = _mix_ffn_call(
            x, outs, lses, gates, u, conv_w[l], conv_b[l].reshape(1, CONV_CH),
            conv_ln_w[l].reshape(1, CONV_CH), conv_ln_b[l].reshape(1, CONV_CH),
            w_o_attn[l].astype(BF16), w_pw_conv[l].astype(BF16), w_out[l].astype(BF16),
            norm2_w[l].reshape(1, d), w_ffn_in[l].astype(BF16), w_ffn_out[l].astype(BF16))
    return x
```

```python
import functools

import jax
import jax.numpy as jnp
from jax import lax
from jax.experimental import pallas as pl
from jax.experimental.pallas import tpu as pltpu

D_MODEL = 1024
HEAD_DIM = 64
N_SLOT_HEADS = 8
DILATED_PATTERNS = ((128, 1), (512, 4), (2048, 16))
N_GROUPS = len(DILATED_PATTERNS)
GROUP_WIDTH = N_SLOT_HEADS * HEAD_DIM
QKV_WIDTH = N_GROUPS * GROUP_WIDTH
ROPE_THETA = 500000.0
ROT_DIM = HEAD_DIM // 4
CONV_CH = D_MODEL // 2
CONV_WIDTH = 31
D_FF = 2816
EPS = 1e-6
NEG_INF = -1e30

LANES = 128
SUBLANES = 8
SUB_Q = 64
KEY_BLOCK = 128
HALF_SPAN = 64
LSE_LANES = 16
ATTN_BATCH = 2
LOG2E = 1.4426950408889634
LN2 = 0.6931471805599453
ROW_TILE = 512
EPI_ROWS = 256
GATHER_STRIDE = 4
GROUP_CHUNKS = GROUP_WIDTH // LANES
VMEM_LIMIT = 56 * 1024 * 1024
MIX_VMEM_LIMIT = 60 * 1024 * 1024

F32 = jnp.float32
BF16 = jnp.bfloat16


def _dot(a, b):
    return jnp.dot(a, b, preferred_element_type=F32)


def _resident(shape):
    nd = len(shape)
    return pl.BlockSpec(shape, lambda *_: (0,) * nd, pipeline_mode=pl.Buffered(1))


def _row_block(width):
    return pl.BlockSpec((None, ROW_TILE, width), lambda b, i: (b, i, 0))


def _group_block(dilation, width, lead=()):
    n_lead = len(lead)
    return pl.BlockSpec(lead + (None, dilation, ROW_TILE // dilation, width),
                        lambda b, i: (0,) * n_lead + (b, 0, i, 0))


def _proj_kernel(x_ref, n1_ref, w_ref, rot_ref, qn_ref, kn_ref,
                 bg_ref, qkv0_ref, qkv1_ref, qkv2_ref, u_ref, g_ref, h_ref, perm_ref, acc_ref,
                 rope_ref, perm2_ref):
    xt = x_ref[...]
    ms = jnp.mean(xt * xt, axis=-1, keepdims=True)
    h_ref[...] = (xt * lax.rsqrt(ms + EPS) * n1_ref[...]).astype(BF16)

    row_blocks = [slice(r, r + EPI_ROWS) for r in range(0, ROW_TILE, EPI_ROWS)]
    chunks = [slice(c * LANES, (c + 1) * LANES) for c in range(GROUP_CHUNKS)]

    half_rot = ROT_DIM // 2
    head_lane = lax.broadcasted_iota(jnp.int32, (1, LANES), 1) % HEAD_DIM
    first_half = head_lane < half_rot
    for rows in row_blocks:
        tab = rot_ref[:, rows].T
        rope_ref[0, rows, :] = jnp.where(first_half, tab,
                                         jnp.where(head_lane < ROT_DIM, pltpu.roll(tab, half_rot, 1), 1.0))
        rope_ref[1, rows, :] = jnp.where(first_half, -pltpu.roll(tab, LANES - half_rot, 1),
                                         jnp.where(head_lane < ROT_DIM, tab, 0.0))

    n_proj = [0]

    def proj(c0):
        slot = n_proj[0] % acc_ref.shape[0]
        n_proj[0] += 1
        acc_ref[slot] = _dot(h_ref[...], w_ref[:, c0:c0 + GROUP_WIDTH])
        return acc_ref.at[slot]

    same_head = (lax.broadcasted_iota(jnp.int32, (LANES, LANES), 0) // HEAD_DIM
                 == lax.broadcasted_iota(jnp.int32, (LANES, LANES), 1) // HEAD_DIM)
    seg = jnp.where(same_head, 1.0 / HEAD_DIM, 0.0).astype(BF16)

    def head_norm_rope(t, nw, rows):
        ms = _dot((t * t).astype(BF16), seg)
        y = t * lax.rsqrt(ms + EPS) * nw
        partner = jnp.where(first_half, pltpu.roll(y, LANES - half_rot, 1), pltpu.roll(y, half_rot, 1))
        return y * rope_ref[0, rows, :] + partner * rope_ref[1, rows, :]

    def emit(kind, group, acc, fn):
        out_ref = (qkv0_ref, qkv1_ref, qkv2_ref)[group]
        dilation = DILATED_PATTERNS[group][1]
        n = ROW_TILE // dilation
        for c, cols in enumerate(chunks):
            for rows in row_blocks:
                val = fn(acc[rows, cols], rows)
                if dilation == 1:
                    out_ref[kind, 0, rows, cols] = val.astype(BF16)
                else:
                    perm_ref[c, rows, :] = val
            if dilation == GATHER_STRIDE:
                for r in range(dilation):
                    out_ref[kind, r, :, cols] = perm_ref[c, pl.ds(r, n, stride=dilation), :].astype(BF16)
            elif dilation == GATHER_STRIDE ** 2:
                q = ROW_TILE // GATHER_STRIDE
                for r1 in range(GATHER_STRIDE):
                    perm2_ref[c, r1 * q:(r1 + 1) * q, :] = perm_ref[c, pl.ds(r1, q, stride=GATHER_STRIDE), :]
                for r1 in range(GATHER_STRIDE):
                    for r2 in range(GATHER_STRIDE):
                        out_ref[kind, GATHER_STRIDE * r2 + r1, :, cols] = perm2_ref[
                            c, pl.ds(r1 * q + r2, n, stride=GATHER_STRIDE), :].astype(BF16)

    def qk_block(kind, j):
        nw_ref, scale = ((qn_ref, HEAD_DIM ** -0.5 * LOG2E), (kn_ref, 1.0))[kind]
        nw = nw_ref[...] * scale
        acc = proj(kind * QKV_WIDTH + j * GROUP_WIDTH)
        emit(kind, j, acc, lambda t, rows: head_norm_rope(t, nw, rows))

    def v_block(j):
        emit(2, j, proj(2 * QKV_WIDTH + j * GROUP_WIDTH), lambda t, rows: t)

    conv0 = 3 * QKV_WIDTH
    gate0 = conv0 + 2 * CONV_CH

    def gate_block(j):
        lo = j * GROUP_WIDTH
        acc = proj(gate0 + lo)
        for rows in row_blocks:
            g_ref[rows, lo:lo + GROUP_WIDTH] = jax.nn.sigmoid(
                acc[rows, :] + bg_ref[:, lo:lo + GROUP_WIDTH])

    for j in range(N_GROUPS):
        qk_block(0, j)
        v_block(j)
        qk_block(1, j)
        gate_block(j)
    a = proj(conv0)
    b = proj(conv0 + CONV_CH)
    for rows in row_blocks:
        u_ref[rows, :] = a[rows, :] * jax.nn.sigmoid(b[rows, :])
    gate_block(N_GROUPS)


def _proj_call(x3, n1, w_in, rot, qn, kn, bg):
    bsz, s, _ = x3.shape
    in_width = w_in.shape[1]
    qkv_shapes = [jax.ShapeDtypeStruct((3, bsz, d, s // d, GROUP_WIDTH), BF16)
                  for _, d in DILATED_PATTERNS]
    return pl.pallas_call(
        _proj_kernel,
        grid=(bsz, s // ROW_TILE),
        in_specs=[_row_block(D_MODEL), _resident((1, D_MODEL)), _resident((D_MODEL, in_width)),
                  pl.BlockSpec((None, LANES, ROW_TILE), lambda b, i: (b, 0, i)),
                  _resident((1, LANES)), _resident((1, LANES)),
                  _resident((1, 2 * D_MODEL))],
        out_specs=[_group_block(d, GROUP_WIDTH, lead=(3,)) for _, d in DILATED_PATTERNS]
        + [_row_block(CONV_CH), _row_block(2 * D_MODEL)],
        out_shape=qkv_shapes + [jax.ShapeDtypeStruct((bsz, s, CONV_CH), F32),
                                jax.ShapeDtypeStruct((bsz, s, 2 * D_MODEL), F32)],
        scratch_shapes=[pltpu.VMEM((ROW_TILE, D_MODEL), BF16),
                        pltpu.VMEM((GROUP_CHUNKS, ROW_TILE, LANES), F32),
                        pltpu.VMEM((3, ROW_TILE, GROUP_WIDTH), F32),
                        pltpu.VMEM((2, ROW_TILE, LANES), F32),
                        pltpu.VMEM((GROUP_CHUNKS, ROW_TILE, LANES), F32)],
        compiler_params=pltpu.CompilerParams(dimension_semantics=("arbitrary", "arbitrary"),
                                             vmem_limit_bytes=VMEM_LIMIT),
        name="proj",
    )(x3, n1, w_in, rot, qn, kn, bg)


def _attn_kernel(q_ref, k_ref, v_ref, o_ref, lse_ref, kpad_ref, vt_ref, bias_ref, s_ref, p_ref,
                 ot_ref, stat_ref):
    n_seq, seq_len, _ = q_ref.shape
    whole = seq_len == KEY_BLOCK
    kw = KEY_BLOCK if whole else 3 * SUB_Q
    n_key_blocks = seq_len // KEY_BLOCK
    n_blocks = n_seq * n_key_blocks
    vt_pad = 0 if whole else 1
    last_sub = seq_len // SUB_Q - 1
    subs = KEY_BLOCK // SUB_Q
    lane = lax.broadcasted_iota(jnp.int32, (1, LANES), 1)
    first_head = lane < HEAD_DIM
    block_diag = (lax.broadcasted_iota(jnp.int32, (LANES, LANES), 0) < HEAD_DIM) == first_head

    @pl.when(pl.program_id(0) == 0)
    def _init():
        row = lax.broadcasted_iota(jnp.int32, (kw, LANES), 0)
        lq = lax.broadcasted_iota(jnp.int32, (kw, LANES), 1) % SUB_Q
        as_bias = lambda valid: jnp.where(valid, 0.0, NEG_INF).astype(F32)
        if whole:
            for case in range(subs):
                bias_ref[case] = as_bias(jnp.abs(lq + case * SUB_Q - row) <= HALF_SPAN)
        else:
            band = jnp.abs(lq + SUB_Q - row) <= HALF_SPAN
            bias_ref[0] = as_bias(band & (row >= SUB_Q))
            bias_ref[1] = as_bias(band)
            bias_ref[2] = as_bias(band & (row < 2 * SUB_Q))
            zeros = jnp.zeros((SUB_Q, GROUP_WIDTH), BF16)
            zero_block = jnp.zeros((GROUP_WIDTH, KEY_BLOCK), BF16)
            for r in range(n_seq):
                kpad_ref[r, 0:SUB_Q, :] = zeros
                kpad_ref[r, SUB_Q + seq_len:2 * SUB_Q + seq_len, :] = zeros
                vt_ref[r, 0] = zero_block
                vt_ref[r, n_key_blocks + 1] = zero_block

    def split(j):
        return j // n_key_blocks, j % n_key_blocks

    if not whole:
        for r in range(n_seq):
            kpad_ref[r, SUB_Q:SUB_Q + seq_len, :] = k_ref[r]

    def stage_vt(j):
        r, b = split(j)
        rows = pl.ds(pl.multiple_of(b * KEY_BLOCK, KEY_BLOCK), KEY_BLOCK)
        for hp in range(GROUP_CHUNKS):
            cols = slice(hp * LANES, (hp + 1) * LANES)
            vt_ref[r, b + vt_pad, cols, :] = v_ref[r, rows, cols].T

    def stage1_qk(j, sub_blocks, pairs):
        r, i = split(j)
        for sub in sub_blocks:
            q0 = pl.multiple_of((subs * i + sub) * SUB_Q, SUB_Q)
            for hp in pairs:
                cols = slice(hp * LANES, (hp + 1) * LANES)
                qb = q_ref[r, pl.ds(q0, SUB_Q), cols]
                zq = jnp.zeros_like(qb)
                qv = jnp.concatenate([jnp.where(first_head, qb, zq),
                                      jnp.where(first_head, zq, qb)], axis=0)
                kwin = k_ref[r, :, cols] if whole else kpad_ref[r, pl.ds(q0, kw), cols]
                s_ref[sub * GROUP_CHUNKS + hp] = lax.dot_general(
                    kwin, qv, (((1,), (1,)), ((), ())), preferred_element_type=F32)

    def stage2_softmax(j, sub_blocks, pairs):
        _, i = split(j)
        for sub in sub_blocks:
            t = subs * i + sub
            if whole:
                bias = bias_ref[sub]
            else:
                bias = bias_ref[jnp.where(t == 0, 0, jnp.where(t == last_sub, 2, 1))]
            for hp in pairs:
                unit = sub * GROUP_CHUNKS + hp
                if whole:
                    s = s_ref[unit] + bias
                else:
                    s = jnp.concatenate([s_ref[unit, 0:SUB_Q, :] + bias[0:SUB_Q],
                                         s_ref[unit, SUB_Q:2 * SUB_Q, :],
                                         s_ref[unit, 2 * SUB_Q:kw, :] + bias[2 * SUB_Q:kw]], axis=0)
                m = jnp.max(s, axis=0, keepdims=True)
                p = jnp.exp2(s - m)
                den = jnp.sum(p, axis=0, keepdims=True)
                p_ref[unit] = p.astype(BF16)
                stat_ref[0, unit, 0:1, :] = 1.0 / den
                stat_ref[0, unit, 1:2, :] = (m + jnp.log2(den)) * LN2

    def stage3_pv(j, sub_blocks, pairs):
        r, i = split(j)
        for sub in sub_blocks:
            for hp in pairs:
                unit = sub * GROUP_CHUNKS + hp
                cols = slice(hp * LANES, (hp + 1) * LANES)
                pb = p_ref[unit]
                if whole:
                    ot_ref[unit] = _dot(vt_ref[r, 0, cols, :], pb)
                else:
                    first = i + sub
                    vt2 = jnp.concatenate([vt_ref[r, first, cols, :], vt_ref[r, first + 1, cols, :]], axis=1)
                    zp = jnp.zeros((SUB_Q, LANES), BF16)
                    ot_ref[unit] = _dot(vt2, jnp.concatenate([zp, pb] if sub == 0 else [pb, zp], axis=0))
                stat_ref[1, unit] = stat_ref[0, unit]

    def stage4_finish(j, sub_blocks, pairs):
        r, i = split(j)
        for sub in sub_blocks:
            q0 = pl.multiple_of((subs * i + sub) * SUB_Q, SUB_Q)
            lse_rows = pl.ds(q0, SUB_Q)
            lse_tile = jnp.zeros((SUB_Q, LANES), F32) if pairs[0] == 0 else lse_ref[r, lse_rows, :]
            for hp in pairs:
                unit = sub * GROUP_CHUNKS + hp
                cols = slice(hp * LANES, (hp + 1) * LANES)
                tile = jnp.where(block_diag, ot_ref[unit] * stat_ref[1, unit, 0:1, :],
                                 stat_ref[1, unit, 1:2, :]).T
                top, bot = tile[0:SUB_Q], tile[SUB_Q:2 * SUB_Q]
                o_ref[r, pl.ds(q0, SUB_Q), cols] = jnp.where(first_head, top, bot).astype(BF16)
                lse_tile = jnp.where((lane % HEAD_DIM) // LSE_LANES == hp,
                                     jnp.where(first_head, bot, top), lse_tile)
            lse_ref[r, lse_rows, :] = lse_tile

    def trip(j, stages=(4, 3, 2, 1)):
        pair_groups = [[hp] for hp in range(GROUP_CHUNKS)] if whole else [list(range(GROUP_CHUNKS))]
        for sub in range(subs):
            for pairs in pair_groups:
                if 4 in stages:
                    stage4_finish(j - 2, [sub], pairs)
                if 3 in stages:
                    stage3_pv(j - 1, [sub], pairs)
                if 2 in stages:
                    stage2_softmax(j, [sub], pairs)
                if 1 in stages:
                    stage1_qk(j + 1, [sub], pairs)
        if 1 in stages:
            stage_vt(j + 1)

    trip(-1, stages=(1,))
    trip(0, stages=(2, 1))
    trip(1, stages=(3, 2, 1))

    def steady(j, c):
        trip(j)
        return c

    lax.fori_loop(2, n_blocks - 1, steady, 0)
    trip(n_blocks - 1, stages=(4, 3, 2))
    trip(n_blocks, stages=(4, 3))
    trip(n_blocks + 1, stages=(4,))


def _attn_call(qkv):
    _, full_bsz, full_dilation, seq_len, _ = qkv.shape
    bsz, dilation = full_bsz // ATTN_BATCH, full_dilation * ATTN_BATCH
    qkv = qkv.reshape(3, bsz, dilation, seq_len, GROUP_WIDTH)
    whole = seq_len == KEY_BLOCK
    kw = KEY_BLOCK if whole else 3 * SUB_Q
    units = (KEY_BLOCK // SUB_Q) * GROUP_CHUNKS
    in_spec = lambda kind: pl.BlockSpec((None, None, dilation, seq_len, GROUP_WIDTH),
                                        lambda b: (kind, b, 0, 0, 0))
    out_spec = lambda w: pl.BlockSpec((None, dilation, seq_len, w), lambda b: (b, 0, 0, 0))
    o, lse = pl.pallas_call(
        _attn_kernel,
        grid=(bsz,),
        in_specs=[in_spec(0), in_spec(1), in_spec(2)],
        out_specs=[out_spec(GROUP_WIDTH), out_spec(LANES)],
        out_shape=[jax.ShapeDtypeStruct((bsz, dilation, seq_len, GROUP_WIDTH), BF16),
                   jax.ShapeDtypeStruct((bsz, dilation, seq_len, LANES), F32)],
        scratch_shapes=[
            pltpu.VMEM((dilation, seq_len + 2 * SUB_Q, GROUP_WIDTH), BF16),
            pltpu.VMEM((dilation, seq_len // KEY_BLOCK + (0 if whole else 2), GROUP_WIDTH, KEY_BLOCK), BF16),
            pltpu.VMEM((2 if whole else 3, kw, LANES), F32),
            pltpu.VMEM((units, kw, LANES), F32),
            pltpu.VMEM((units, kw, LANES), BF16),
            pltpu.VMEM((units, LANES, LANES), F32),
            pltpu.VMEM((2, units, SUBLANES, LANES), F32)],
        compiler_params=pltpu.CompilerParams(dimension_semantics=("arbitrary",),
                                             vmem_limit_bytes=VMEM_LIMIT),
        name=f"attn_d{full_dilation}",
    )(qkv, qkv, qkv)
    return (o.reshape(full_bsz, full_dilation, seq_len, GROUP_WIDTH),
            lse.reshape(full_bsz, full_dilation, seq_len, LANES))


CONV_PAD = 16
CONV_ROWS = 64
CONV_CHUNKS = CONV_CH // LANES


def _conv_stage(u_ref, prev_ref, next_ref, has_prev, has_next, pad_ref):
    for c in range(CONV_CHUNKS):
        cols = slice(c * LANES, (c + 1) * LANES)
        pad_ref[c, 0:CONV_PAD, :] = jnp.where(has_prev, prev_ref[:, cols], 0.0)
        pad_ref[c, CONV_PAD:CONV_PAD + ROW_TILE, :] = u_ref[:, cols]
        pad_ref[c, CONV_PAD + ROW_TILE:2 * CONV_PAD + ROW_TILE, :] = jnp.where(
            has_next, next_ref[:, cols], 0.0)


def _conv_ln_rows(r0, w_ref, b_ref, lnw_ref, lnb_ref, out_ref, pad_ref, conv_ref):
    chunks = [slice(c * LANES, (c + 1) * LANES) for c in range(CONV_CHUNKS)]
    first_tap = CONV_PAD - (CONV_WIDTH - 1) // 2
    half_rows = CONV_ROWS // 2
    for c, cols in enumerate(chunks):
        for parity in range(2):
            acc = jnp.zeros((half_rows, LANES), F32) + b_ref[:, cols]
            for t in range(CONV_WIDTH):
                rows = pl.ds(r0 + parity + first_tap + t, half_rows, stride=2)
                acc = acc + pad_ref[c, rows, :] * w_ref[t:t + 1, cols]
            conv_ref[c, pl.ds(r0 + parity, half_rows, stride=2), :] = acc
    ys = [conv_ref[c, r0:r0 + CONV_ROWS, :] for c in range(CONV_CHUNKS)]
    mu = sum(jnp.sum(y, axis=-1, keepdims=True) for y in ys) * (1.0 / CONV_CH)
    cen = [y - mu for y in ys]
    var = sum(jnp.sum(t * t, axis=-1, keepdims=True) for t in cen) * (1.0 / CONV_CH)
    inv = lax.rsqrt(var + EPS)
    for c, cols in enumerate(chunks):
        y = cen[c] * inv * lnw_ref[:, cols] + lnb_ref[:, cols]
        out_ref[r0:r0 + CONV_ROWS, cols] = (y * jax.nn.sigmoid(y)).astype(BF16)
    return y[0:1, :]


def _zero_after(value):
    bits = value.astype(jnp.int32)
    return lax.shift_right_logical(lax.shift_right_logical(bits, 16), 16).astype(F32)


FF_CHUNK = 256


def _mix_ffn_kernel(x_ref, o0_ref, o1_ref, o2_ref, l0_ref, l1_ref, l2_ref, g_ref,
                    uc_ref, up_ref, un_ref, cw_ref, cb_ref, lnw_ref, lnb_ref,
                    wo_ref, wpw_ref, wout_ref, n2_ref, w1_ref, w2_ref,
                    out_ref, attn_ref, act_ref, perm_ref, pad_ref, conv_ref, uln_ref, perm2_ref,
                    *, tiles_per_seq):
    t = pl.program_id(0)
    conv_pos = jnp.minimum(t, pl.num_programs(0) - 2) % tiles_per_seq
    write_slot = t % 2
    read_slot = 1 - write_slot

    @pl.when(t == 0)
    def _init():
        uln_ref[1] = jnp.zeros(uln_ref.shape[1:], BF16)

    _conv_stage(uc_ref, up_ref, un_ref, conv_pos > 0, conv_pos < tiles_per_seq - 1, pad_ref)

    def token_order(ref, group, slab, cols):
        dilation = DILATED_PATTERNS[group][1]
        if dilation == 1:
            return ref[0, :, cols].astype(F32)
        n = ROW_TILE // dilation
        if dilation == GATHER_STRIDE:
            for r in range(dilation):
                perm_ref[slab, pl.ds(r, n, stride=dilation), :] = ref[r, :, cols].astype(F32)
        else:
            stage = slab % perm2_ref.shape[0]
            q = ROW_TILE // GATHER_STRIDE
            for r in range(dilation):
                r2, r1 = divmod(r, GATHER_STRIDE)
                perm2_ref[stage, pl.ds(r1 * q + r2, n, stride=GATHER_STRIDE), :] = ref[r, :, cols].astype(F32)
            for r1 in range(GATHER_STRIDE):
                perm_ref[slab, pl.ds(r1, q, stride=GATHER_STRIDE), :] = perm2_ref[stage, r1 * q:(r1 + 1) * q, :]
        return perm_ref[slab]

    o_refs = (o0_ref, o1_ref, o2_ref)
    all_lanes = slice(0, LANES)
    n_slabs = GROUP_CHUNKS + 1
    lse = [token_order(ref, g, g * n_slabs + GROUP_CHUNKS, all_lanes)
           for g, ref in enumerate((l0_ref, l1_ref, l2_ref))]
    m = jnp.maximum(jnp.maximum(lse[0], lse[1]), lse[2])
    e = [jnp.exp(l - m) for l in lse]
    inv = 1.0 / (e[0] + e[1] + e[2])
    weights = [(e_g * inv).astype(BF16) for e_g in e]
    src_lane = lax.broadcasted_iota(jnp.int32, (LANES, LANES), 0)
    first_head = lax.broadcasted_iota(jnp.int32, (LANES, LANES), 1) < HEAD_DIM
    for c in range(GROUP_CHUNKS):
        cols = slice(c * LANES, (c + 1) * LANES)
        lane_b = LSE_LANES * c
        expand = jnp.where(src_lane == jnp.where(first_head, lane_b + HEAD_DIM, lane_b),
                           1.0, 0.0).astype(BF16)
        acc = jnp.zeros((ROW_TILE, LANES), F32)
        for g in range(N_GROUPS):
            acc = acc + _dot(weights[g], expand) * token_order(o_refs[g], g, g * n_slabs + c, cols)
        attn_ref[:, cols] = acc.astype(BF16)

    y_a = _dot(attn_ref[...], wo_ref[...])
    y_b = _dot(uln_ref[read_slot], wpw_ref[...])
    z = g_ref[:, 0:D_MODEL] * y_a + g_ref[:, D_MODEL:2 * D_MODEL] * y_b
    x1 = x_ref[...] + _dot(z.astype(BF16), wout_ref[...])

    ms = jnp.mean(x1 * x1, axis=-1, keepdims=True)
    h2 = (x1 * lax.rsqrt(ms + EPS) * n2_ref[...]).astype(BF16)
    ff_chunks = range(0, D_FF, FF_CHUNK)
    conv_blocks = list(range(0, ROW_TILE, CONV_ROWS))
    issue_after = [k * (len(ff_chunks) - 1) // len(conv_blocks) for k in range(len(conv_blocks))]
    pending = None
    for idx, n in enumerate(ff_chunks):
        width = min(FF_CHUNK, D_FF - n)
        gt = _dot(h2, w1_ref[:, n:n + width])
        up = _dot(h2, w1_ref[:, D_FF + n:D_FF + n + width])
        act = gt * jax.nn.sigmoid(gt) * up
        if pending is not None:
            act = act + jnp.concatenate([_zero_after(pending)] * (width // LANES), axis=1)
            pending = None
        act_ref[:, n:n + width] = act.astype(BF16)
        for k, r0 in enumerate(conv_blocks):
            if issue_after[k] == idx:
                probe = _conv_ln_rows(r0, cw_ref, cb_ref, lnw_ref, lnb_ref,
                                      uln_ref.at[write_slot], pad_ref, conv_ref)
                pending = probe if pending is None else pending + probe
    out_ref[...] = x1 + _dot(act_ref[...], w2_ref[...])


def _mix_ffn_call(x3, o, lse, gates, u, conv_w, conv_b, ln_w, ln_b, wo, wpw, wout, n2, w1, w2):
    bsz, s, _ = x3.shape
    dilations = [d for _, d in DILATED_PATTERNS]
    tiles_per_seq = s // ROW_TILE
    n_tiles = bsz * tiles_per_seq
    halo_blocks = ROW_TILE // CONV_PAD

    def finish_tile(t):
        tile = jnp.maximum(t - 1, 0)
        return tile // tiles_per_seq, tile % tiles_per_seq

    def conv_tile(t):
        tile = jnp.minimum(t, n_tiles - 1)
        return tile // tiles_per_seq, tile % tiles_per_seq

    def row(width):
        return pl.BlockSpec((None, ROW_TILE, width), lambda t: (*finish_tile(t), 0))

    def group(dilation, width):
        return pl.BlockSpec((None, dilation, ROW_TILE // dilation, width),
                            lambda t: (finish_tile(t)[0], 0, finish_tile(t)[1], 0))

    def halo(offset):
        def index(t):
            b, i = conv_tile(t)
            block = i * halo_blocks - 1 if offset < 0 else (i + 1) * halo_blocks
            return b, jnp.clip(block, 0, s // CONV_PAD - 1), 0
        return pl.BlockSpec((None, CONV_PAD, CONV_CH), index)

    return pl.pallas_call(
        functools.partial(_mix_ffn_kernel, tiles_per_seq=tiles_per_seq),
        grid=(n_tiles + 1,),
        in_specs=[row(D_MODEL)]
        + [group(d, GROUP_WIDTH) for d in dilations]
        + [group(d, LANES) for d in dilations]
        + [row(2 * D_MODEL),
           pl.BlockSpec((None, ROW_TILE, CONV_CH), lambda t: (*conv_tile(t), 0)), halo(-1), halo(1),
           _resident(conv_w.shape), _resident(conv_b.shape), _resident(ln_w.shape),
           _resident(ln_b.shape),
           _resident(wo.shape), _resident(wpw.shape),
           _resident(wout.shape), _resident(n2.shape), _resident(w1.shape), _resident(w2.shape)],
        out_specs=row(D_MODEL),
        out_shape=jax.ShapeDtypeStruct((bsz, s, D_MODEL), F32),
        scratch_shapes=[pltpu.VMEM((ROW_TILE, GROUP_WIDTH), BF16),
                        pltpu.VMEM((ROW_TILE, D_FF), BF16),
                        pltpu.VMEM((N_GROUPS * (GROUP_CHUNKS + 1), ROW_TILE, LANES), F32),
                        pltpu.VMEM((CONV_CHUNKS, ROW_TILE + 2 * CONV_PAD, LANES), F32),
                        pltpu.VMEM((CONV_CHUNKS, ROW_TILE, LANES), F32),
                        pltpu.VMEM((2, ROW_TILE, CONV_CH), BF16),
                        pltpu.VMEM((GROUP_CHUNKS + 1, ROW_TILE, LANES), F32)],
        compiler_params=pltpu.CompilerParams(dimension_semantics=("arbitrary",),
                                             vmem_limit_bytes=MIX_VMEM_LIMIT),
        name="mix_ffn",
    )(x3, *o, *lse, gates, u, u, u, conv_w, conv_b, ln_w, ln_b, wo, wpw, wout, n2, w1, w2)


def _rope_table(positions):
    inv_freq = ROPE_THETA ** (-jnp.arange(0, ROT_DIM, 2, dtype=F32) / ROT_DIM)
    ang = positions.astype(F32)[:, None, :] * inv_freq[None, :, None]
    rest = jnp.zeros((ang.shape[0], HEAD_DIM - ROT_DIM, ang.shape[2]), F32)
    return jnp.concatenate([jnp.cos(ang), jnp.sin(ang), rest] * (LANES // HEAD_DIM), axis=1)


def kernel(x, positions, norm1_w, w_in, b_gate, q_norm_w, k_norm_w, w_o_attn, conv_w, conv_b,
           conv_ln_w, conv_ln_b, w_pw_conv, w_out, norm2_w, w_ffn_in, w_ffn_out):
    d = x.shape[-1]
    depth = norm1_w.shape[0]
    rot = _rope_table(positions)
    pair = lambda w: jnp.tile(w.astype(F32), LANES // HEAD_DIM).reshape(1, LANES)

    for l in range(depth):
        qkv0, qkv1, qkv2, u, gates = _proj_call(
            x, norm1_w[l].reshape(1, d), w_in[l].astype(BF16), rot,
            pair(q_norm_w[l]), pair(k_norm_w[l]), b_gate[l].reshape(1, 2 * d))
        outs, lses = zip(*[_attn_call(qkv) for qkv in (qkv0, qkv1, qkv2)])
        x = _mix_ffn_call(
            x, outs, lses, gates, u, conv_w[l], conv_b[l].reshape(1, CONV_CH),
            conv_ln_w[l].reshape(1, CONV_CH), conv_ln_b[l].reshape(1, CONV_CH),
            w_o_attn[l].astype(BF16), w_pw_conv[l].astype(BF16), w_out[l].astype(BF16),
            norm2_w[l].reshape(1, d), w_ffn_in[l].astype(BF16), w_ffn_out[l].astype(BF16))
    return x
```

```python
import functools

import jax
import jax.numpy as jnp
from jax import lax
from jax.experimental import pallas as pl
from jax.experimental.pallas import tpu as pltpu

D_MODEL = 1024
HEAD_DIM = 64
N_SLOT_HEADS = 8
DILATED_PATTERNS = ((128, 1), (512, 4), (2048, 16))
N_GROUPS = len(DILATED_PATTERNS)
GROUP_WIDTH = N_SLOT_HEADS * HEAD_DIM
QKV_WIDTH = N_GROUPS * GROUP_WIDTH
ROPE_THETA = 500000.0
ROT_DIM = HEAD_DIM // 4
CONV_CH = D_MODEL // 2
CONV_WIDTH = 31
D_FF = 2816
EPS = 1e-6
NEG_INF = -1e30

LANES = 128
SUBLANES = 8
SUB_Q = 64
KEY_BLOCK = 128
HALF_SPAN = 64
LSE_LANES = 16
LOG2E = 1.4426950408889634
LN2 = 0.6931471805599453
ROW_TILE = 512
EPI_ROWS = 256
GATHER_STRIDE = 4
GROUP_CHUNKS = GROUP_WIDTH // LANES
VMEM_LIMIT = 56 * 1024 * 1024
MIX_VMEM_LIMIT = 60 * 1024 * 1024

F32 = jnp.float32
BF16 = jnp.bfloat16


def _dot(a, b):
    return jnp.dot(a, b, preferred_element_type=F32)


def _resident(shape):
    nd = len(shape)
    return pl.BlockSpec(shape, lambda *_: (0,) * nd, pipeline_mode=pl.Buffered(1))


def _row_block(width):
    return pl.BlockSpec((None, ROW_TILE, width), lambda b, i: (b, i, 0))


def _group_block(dilation, width, lead=()):
    n_lead = len(lead)
    return pl.BlockSpec(lead + (None, dilation, ROW_TILE // dilation, width),
                        lambda b, i: (0,) * n_lead + (b, 0, i, 0))


def _proj_kernel(x_ref, n1_ref, w_ref, rot_ref, qn_ref, kn_ref,
                 bg_ref, qkv0_ref, qkv1_ref, qkv2_ref, u_ref, g_ref, h_ref, perm_ref, acc_ref,
                 rope_ref, perm2_ref):
    xt = x_ref[...]
    ms = jnp.mean(xt * xt, axis=-1, keepdims=True)
    h_ref[...] = (xt * lax.rsqrt(ms + EPS) * n1_ref[...]).astype(BF16)

    row_blocks = [slice(r, r + EPI_ROWS) for r in range(0, ROW_TILE, EPI_ROWS)]
    chunks = [slice(c * LANES, (c + 1) * LANES) for c in range(GROUP_CHUNKS)]

    half_rot = ROT_DIM // 2
    head_lane = lax.broadcasted_iota(jnp.int32, (1, LANES), 1) % HEAD_DIM
    first_half = head_lane < half_rot
    for rows in row_blocks:
        tab = rot_ref[:, rows].T
        rope_ref[0, rows, :] = jnp.where(first_half, tab,
                                         jnp.where(head_lane < ROT_DIM, pltpu.roll(tab, half_rot, 1), 1.0))
        rope_ref[1, rows, :] = jnp.where(first_half, -pltpu.roll(tab, LANES - half_rot, 1),
                                         jnp.where(head_lane < ROT_DIM, tab, 0.0))

    n_proj = [0]

    def proj(c0):
        slot = n_proj[0] % acc_ref.shape[0]
        n_proj[0] += 1
        acc_ref[slot] = _dot(h_ref[...], w_ref[:, c0:c0 + GROUP_WIDTH])
        return acc_ref.at[slot]

    same_head = (lax.broadcasted_iota(jnp.int32, (LANES, LANES), 0) // HEAD_DIM
                 == lax.broadcasted_iota(jnp.int32, (LANES, LANES), 1) // HEAD_DIM)
    seg = jnp.where(same_head, 1.0 / HEAD_DIM, 0.0).astype(BF16)

    def head_norm_rope(t, nw, rows):
        ms = _dot((t * t).astype(BF16), seg)
        y = t * lax.rsqrt(ms + EPS) * nw
        partner = jnp.where(first_half, pltpu.roll(y, LANES - half_rot, 1), pltpu.roll(y, half_rot, 1))
        return y * rope_ref[0, rows, :] + partner * rope_ref[1, rows, :]

    def emit(kind, group, acc, fn):
        out_ref = (qkv0_ref, qkv1_ref, qkv2_ref)[group]
        dilation = DILATED_PATTERNS[group][1]
        n = ROW_TILE // dilation
        for c, cols in enumerate(chunks):
            for rows in row_blocks:
                val = fn(acc[rows, cols], rows)
                if dilation == 1:
                    out_ref[kind, 0, rows, cols] = val.astype(BF16)
                else:
                    perm_ref[c, rows, :] = val
            if dilation == GATHER_STRIDE:
                for r in range(dilation):
                    out_ref[kind, r, :, cols] = perm_ref[c, pl.ds(r, n, stride=dilation), :].astype(BF16)
            elif dilation == GATHER_STRIDE ** 2:
                q = ROW_TILE // GATHER_STRIDE
                for r1 in range(GATHER_STRIDE):
                    perm2_ref[c, r1 * q:(r1 + 1) * q, :] = perm_ref[c, pl.ds(r1, q, stride=GATHER_STRIDE), :]
                for r1 in range(GATHER_STRIDE):
                    for r2 in range(GATHER_STRIDE):
                        out_ref[kind, GATHER_STRIDE * r2 + r1, :, cols] = perm2_ref[
                            c, pl.ds(r1 * q + r2, n, stride=GATHER_STRIDE), :].astype(BF16)

    def qk_block(kind, j):
        nw_ref, scale = ((qn_ref, HEAD_DIM ** -0.5 * LOG2E), (kn_ref, 1.0))[kind]
        nw = nw_ref[...] * scale
        acc = proj(kind * QKV_WIDTH + j * GROUP_WIDTH)
        emit(kind, j, acc, lambda t, rows: head_norm_rope(t, nw, rows))

    def v_block(j):
        emit(2, j, proj(2 * QKV_WIDTH + j * GROUP_WIDTH), lambda t, rows: t)

    conv0 = 3 * QKV_WIDTH
    gate0 = conv0 + 2 * CONV_CH

    def gate_block(j):
        lo = j * GROUP_WIDTH
        acc = proj(gate0 + lo)
        for rows in row_blocks:
            g_ref[rows, lo:lo + GROUP_WIDTH] = jax.nn.sigmoid(
                acc[rows, :] + bg_ref[:, lo:lo + GROUP_WIDTH])

    for j in range(N_GROUPS):
        qk_block(0, j)
        v_block(j)
        qk_block(1, j)
        gate_block(j)
    a = proj(conv0)
    b = proj(conv0 + CONV_CH)
    for rows in row_blocks:
        u_ref[rows, :] = a[rows, :] * jax.nn.sigmoid(b[rows, :])
    gate_block(N_GROUPS)


def _proj_call(x3, n1, w_in, rot, qn, kn, bg):
    bsz, s, _ = x3.shape
    in_width = w_in.shape[1]
    qkv_shapes = [jax.ShapeDtypeStruct((3, bsz, d, s // d, GROUP_WIDTH), BF16)
                  for _, d in DILATED_PATTERNS]
    return pl.pallas_call(
        _proj_kernel,
        grid=(bsz, s // ROW_TILE),
        in_specs=[_row_block(D_MODEL), _resident((1, D_MODEL)), _resident((D_MODEL, in_width)),
                  pl.BlockSpec((None, LANES, ROW_TILE), lambda b, i: (b, 0, i)),
                  _resident((1, LANES)), _resident((1, LANES)),
                  _resident((1, 2 * D_MODEL))],
        out_specs=[_group_block(d, GROUP_WIDTH, lead=(3,)) for _, d in DILATED_PATTERNS]
        + [_row_block(CONV_CH), _row_block(2 * D_MODEL)],
        out_shape=qkv_shapes + [jax.ShapeDtypeStruct((bsz, s, CONV_CH), F32),
                                jax.ShapeDtypeStruct((bsz, s, 2 * D_MODEL), F32)],
        scratch_shapes=[pltpu.VMEM((ROW_TILE, D_MODEL), BF16),
                        pltpu.VMEM((GROUP_CHUNKS, ROW_TILE, LANES), F32),
                        pltpu.VMEM((3, ROW_TILE, GROUP_WIDTH), F32),
                        pltpu.VMEM((2, ROW_TILE, LANES), F32),
                        pltpu.VMEM((GROUP_CHUNKS, ROW_TILE, LANES), F32)],
        compiler_params=pltpu.CompilerParams(dimension_semantics=("arbitrary", "arbitrary"),
                                             vmem_limit_bytes=VMEM_LIMIT),
        name="proj",
    )(x3, n1, w_in, rot, qn, kn, bg)


def _attn_kernel(q_ref, k_ref, v_ref, o_ref, lse_ref, kpad_ref, vt_ref, bias_ref, s_ref, p_ref,
                 ot_ref, stat_ref):
    n_seq, seq_len, _ = q_ref.shape
    whole = seq_len == KEY_BLOCK
    kw = KEY_BLOCK if whole else 3 * SUB_Q
    n_key_blocks = seq_len // KEY_BLOCK
    n_blocks = n_seq * n_key_blocks
    vt_pad = 0 if whole else 1
    last_sub = seq_len // SUB_Q - 1
    subs = KEY_BLOCK // SUB_Q
    lane = lax.broadcasted_iota(jnp.int32, (1, LANES), 1)
    first_head = lane < HEAD_DIM
    block_diag = (lax.broadcasted_iota(jnp.int32, (LANES, LANES), 0) < HEAD_DIM) == first_head

    @pl.when(pl.program_id(0) == 0)
    def _init():
        row = lax.broadcasted_iota(jnp.int32, (kw, LANES), 0)
        lq = lax.broadcasted_iota(jnp.int32, (kw, LANES), 1) % SUB_Q
        as_bias = lambda valid: jnp.where(valid, 0.0, NEG_INF).astype(F32)
        if whole:
            for case in range(subs):
                bias_ref[case] = as_bias(jnp.abs(lq + case * SUB_Q - row) <= HALF_SPAN)
        else:
            band = jnp.abs(lq + SUB_Q - row) <= HALF_SPAN
            bias_ref[0] = as_bias(band & (row >= SUB_Q))
            bias_ref[1] = as_bias(band)
            bias_ref[2] = as_bias(band & (row < 2 * SUB_Q))
            zeros = jnp.zeros((SUB_Q, GROUP_WIDTH), BF16)
            zero_block = jnp.zeros((GROUP_WIDTH, KEY_BLOCK), BF16)
            for r in range(n_seq):
                kpad_ref[r, 0:SUB_Q, :] = zeros
                kpad_ref[r, SUB_Q + seq_len:2 * SUB_Q + seq_len, :] = zeros
                vt_ref[r, 0] = zero_block
                vt_ref[r, n_key_blocks + 1] = zero_block

    def split(j):
        return j // n_key_blocks, j % n_key_blocks

    if not whole:
        for r in range(n_seq):
            kpad_ref[r, SUB_Q:SUB_Q + seq_len, :] = k_ref[r]

    def stage_vt(j):
        r, b = split(j)
        rows = pl.ds(pl.multiple_of(b * KEY_BLOCK, KEY_BLOCK), KEY_BLOCK)
        for hp in range(GROUP_CHUNKS):
            cols = slice(hp * LANES, (hp + 1) * LANES)
            vt_ref[r, b + vt_pad, cols, :] = v_ref[r, rows, cols].T

    def stage1_qk(j, sub_blocks, pairs):
        r, i = split(j)
        for sub in sub_blocks:
            q0 = pl.multiple_of((subs * i + sub) * SUB_Q, SUB_Q)
            for hp in pairs:
                cols = slice(hp * LANES, (hp + 1) * LANES)
                qb = q_ref[r, pl.ds(q0, SUB_Q), cols]
                zq = jnp.zeros_like(qb)
                qv = jnp.concatenate([jnp.where(first_head, qb, zq),
                                      jnp.where(first_head, zq, qb)], axis=0)
                kwin = k_ref[r, :, cols] if whole else kpad_ref[r, pl.ds(q0, kw), cols]
                s_ref[sub * GROUP_CHUNKS + hp] = lax.dot_general(
                    kwin, qv, (((1,), (1,)), ((), ())), preferred_element_type=F32)

    def stage2_softmax(j, sub_blocks, pairs):
        _, i = split(j)
        for sub in sub_blocks:
            t = subs * i + sub
            if whole:
                bias = bias_ref[sub]
            else:
                bias = bias_ref[jnp.where(t == 0, 0, jnp.where(t == last_sub, 2, 1))]
            for hp in pairs:
                unit = sub * GROUP_CHUNKS + hp
                if whole:
                    s = s_ref[unit] + bias
                else:
                    s = jnp.concatenate([s_ref[unit, 0:SUB_Q, :] + bias[0:SUB_Q],
                                         s_ref[unit, SUB_Q:2 * SUB_Q, :],
                                         s_ref[unit, 2 * SUB_Q:kw, :] + bias[2 * SUB_Q:kw]], axis=0)
                m = jnp.max(s, axis=0, keepdims=True)
                p = jnp.exp2(s - m)
                den = jnp.sum(p, axis=0, keepdims=True)
                p_ref[unit] = p.astype(BF16)
                stat_ref[0, unit, 0:1, :] = 1.0 / den
                stat_ref[0, unit, 1:2, :] = (m + jnp.log2(den)) * LN2

    def stage3_pv(j, sub_blocks, pairs):
        r, i = split(j)
        for sub in sub_blocks:
            for hp in pairs:
                unit = sub * GROUP_CHUNKS + hp
                cols = slice(hp * LANES, (hp + 1) * LANES)
                pb = p_ref[unit]
                if whole:
                    ot_ref[unit] = _dot(vt_ref[r, 0, cols, :], pb)
                else:
                    first = i + sub
                    vt2 = jnp.concatenate([vt_ref[r, first, cols, :], vt_ref[r, first + 1, cols, :]], axis=1)
                    zp = jnp.zeros((SUB_Q, LANES), BF16)
                    ot_ref[unit] = _dot(vt2, jnp.concatenate([zp, pb] if sub == 0 else [pb, zp], axis=0))
                stat_ref[1, unit] = stat_ref[0, unit]

    def stage4_finish(j, sub_blocks, pairs):
        r, i = split(j)
        for sub in sub_blocks:
            q0 = pl.multiple_of((subs * i + sub) * SUB_Q, SUB_Q)
            lse_rows = pl.ds(q0, SUB_Q)
            lse_tile = jnp.zeros((SUB_Q, LANES), F32) if pairs[0] == 0 else lse_ref[r, lse_rows, :]
            for hp in pairs:
                unit = sub * GROUP_CHUNKS + hp
                cols = slice(hp * LANES, (hp + 1) * LANES)
                tile = jnp.where(block_diag, ot_ref[unit] * stat_ref[1, unit, 0:1, :],
                                 stat_ref[1, unit, 1:2, :]).T
                top, bot = tile[0:SUB_Q], tile[SUB_Q:2 * SUB_Q]
                o_ref[r, pl.ds(q0, SUB_Q), cols] = jnp.where(first_head, top, bot).astype(BF16)
                lse_tile = jnp.where((lane % HEAD_DIM) // LSE_LANES == hp,
                                     jnp.where(first_head, bot, top), lse_tile)
            lse_ref[r, lse_rows, :] = lse_tile

    def trip(j, stages=(4, 3, 2, 1)):
        pair_groups = [[hp] for hp in range(GROUP_CHUNKS)] if whole else [list(range(GROUP_CHUNKS))]
        for sub in range(subs):
            for pairs in pair_groups:
                if 4 in stages:
                    stage4_finish(j - 2, [sub], pairs)
                if 3 in stages:
                    stage3_pv(j - 1, [sub], pairs)
                if 2 in stages:
                    stage2_softmax(j, [sub], pairs)
                if 1 in stages:
                    stage1_qk(j + 1, [sub], pairs)
        if 1 in stages:
            stage_vt(j + 1)

    trip(-1, stages=(1,))
    trip(0, stages=(2, 1))
    trip(1, stages=(3, 2, 1))

    def steady(j, c):
        trip(j)
        return c

    lax.fori_loop(2, n_blocks - 1, steady, 0)
    trip(n_blocks - 1, stages=(4, 3, 2))
    trip(n_blocks, stages=(4, 3))
    trip(n_blocks + 1, stages=(4,))


def _attn_call(qkv):
    _, bsz, dilation, seq_len, _ = qkv.shape
    whole = seq_len == KEY_BLOCK
    kw = KEY_BLOCK if whole else 3 * SUB_Q
    units = (KEY_BLOCK // SUB_Q) * GROUP_CHUNKS
    in_spec = lambda kind: pl.BlockSpec((None, None, dilation, seq_len, GROUP_WIDTH),
                                        lambda b: (kind, b, 0, 0, 0))
    out_spec = lambda w: pl.BlockSpec((None, dilation, seq_len, w), lambda b: (b, 0, 0, 0))
    return pl.pallas_call(
        _attn_kernel,
        grid=(bsz,),
        in_specs=[in_spec(0), in_spec(1), in_spec(2)],
        out_specs=[out_spec(GROUP_WIDTH), out_spec(LANES)],
        out_shape=[jax.ShapeDtypeStruct((bsz, dilation, seq_len, GROUP_WIDTH), BF16),
                   jax.ShapeDtypeStruct((bsz, dilation, seq_len, LANES), F32)],
        scratch_shapes=[
            pltpu.VMEM((dilation, seq_len + 2 * SUB_Q, GROUP_WIDTH), BF16),
            pltpu.VMEM((dilation, seq_len // KEY_BLOCK + (0 if whole else 2), GROUP_WIDTH, KEY_BLOCK), BF16),
            pltpu.VMEM((2 if whole else 3, kw, LANES), F32),
            pltpu.VMEM((units, kw, LANES), F32),
            pltpu.VMEM((units, kw, LANES), BF16),
            pltpu.VMEM((units, LANES, LANES), F32),
            pltpu.VMEM((2, units, SUBLANES, LANES), F32)],
        compiler_params=pltpu.CompilerParams(dimension_semantics=("arbitrary",),
                                             vmem_limit_bytes=VMEM_LIMIT),
        name=f"attn_d{dilation}",
    )(qkv, qkv, qkv)


CONV_PAD = 16
CONV_ROWS = 64
CONV_CHUNKS = CONV_CH // LANES


def _conv_stage(u_ref, prev_ref, next_ref, has_prev, has_next, pad_ref):
    for c in range(CONV_CHUNKS):
        cols = slice(c * LANES, (c + 1) * LANES)
        pad_ref[c, 0:CONV_PAD, :] = jnp.where(has_prev, prev_ref[:, cols], 0.0)
        pad_ref[c, CONV_PAD:CONV_PAD + ROW_TILE, :] = u_ref[:, cols]
        pad_ref[c, CONV_PAD + ROW_TILE:2 * CONV_PAD + ROW_TILE, :] = jnp.where(
            has_next, next_ref[:, cols], 0.0)


def _conv_ln_rows(r0, w_ref, b_ref, lnw_ref, lnb_ref, out_ref, pad_ref, conv_ref):
    chunks = [slice(c * LANES, (c + 1) * LANES) for c in range(CONV_CHUNKS)]
    first_tap = CONV_PAD - (CONV_WIDTH - 1) // 2
    half_rows = CONV_ROWS // 2
    for c, cols in enumerate(chunks):
        for parity in range(2):
            acc = jnp.zeros((half_rows, LANES), F32) + b_ref[:, cols]
            for t in range(CONV_WIDTH):
                rows = pl.ds(r0 + parity + first_tap + t, half_rows, stride=2)
                acc = acc + pad_ref[c, rows, :] * w_ref[t:t + 1, cols]
            conv_ref[c, pl.ds(r0 + parity, half_rows, stride=2), :] = acc
    ys = [conv_ref[c, r0:r0 + CONV_ROWS, :] for c in range(CONV_CHUNKS)]
    mu = sum(jnp.sum(y, axis=-1, keepdims=True) for y in ys) * (1.0 / CONV_CH)
    cen = [y - mu for y in ys]
    var = sum(jnp.sum(t * t, axis=-1, keepdims=True) for t in cen) * (1.0 / CONV_CH)
    inv = lax.rsqrt(var + EPS)
    for c, cols in enumerate(chunks):
        y = cen[c] * inv * lnw_ref[:, cols] + lnb_ref[:, cols]
        out_ref[r0:r0 + CONV_ROWS, cols] = (y * jax.nn.sigmoid(y)).astype(BF16)
    return y[0:1, :]


def _zero_after(value):
    bits = value.astype(jnp.int32)
    return lax.shift_right_logical(lax.shift_right_logical(bits, 16), 16).astype(F32)


FF_CHUNK = 256


def _mix_ffn_kernel(x_ref, o0_ref, o1_ref, o2_ref, l0_ref, l1_ref, l2_ref, g_ref,
                    uc_ref, up_ref, un_ref, cw_ref, cb_ref, lnw_ref, lnb_ref,
                    wo_ref, wpw_ref, wout_ref, n2_ref, w1_ref, w2_ref,
                    out_ref, attn_ref, act_ref, perm_ref, pad_ref, conv_ref, uln_ref, perm2_ref,
                    *, tiles_per_seq):
    t = pl.program_id(0)
    conv_pos = jnp.minimum(t, pl.num_programs(0) - 2) % tiles_per_seq
    write_slot = t % 2
    read_slot = 1 - write_slot

    @pl.when(t == 0)
    def _init():
        uln_ref[1] = jnp.zeros(uln_ref.shape[1:], BF16)

    _conv_stage(uc_ref, up_ref, un_ref, conv_pos > 0, conv_pos < tiles_per_seq - 1, pad_ref)

    def token_order(ref, group, slab, cols):
        dilation = DILATED_PATTERNS[group][1]
        if dilation == 1:
            return ref[0, :, cols].astype(F32)
        n = ROW_TILE // dilation
        if dilation == GATHER_STRIDE:
            for r in range(dilation):
                perm_ref[slab, pl.ds(r, n, stride=dilation), :] = ref[r, :, cols].astype(F32)
        else:
            stage = slab % perm2_ref.shape[0]
            q = ROW_TILE // GATHER_STRIDE
            for r in range(dilation):
                r2, r1 = divmod(r, GATHER_STRIDE)
                perm2_ref[stage, pl.ds(r1 * q + r2, n, stride=GATHER_STRIDE), :] = ref[r, :, cols].astype(F32)
            for r1 in range(GATHER_STRIDE):
                perm_ref[slab, pl.ds(r1, q, stride=GATHER_STRIDE), :] = perm2_ref[stage, r1 * q:(r1 + 1) * q, :]
        return perm_ref[slab]

    o_refs = (o0_ref, o1_ref, o2_ref)
    all_lanes = slice(0, LANES)
    n_slabs = GROUP_CHUNKS + 1
    lse = [token_order(ref, g, g * n_slabs + GROUP_CHUNKS, all_lanes)
           for g, ref in enumerate((l0_ref, l1_ref, l2_ref))]
    m = jnp.maximum(jnp.maximum(lse[0], lse[1]), lse[2])
    e = [jnp.exp(l - m) for l in lse]
    inv = 1.0 / (e[0] + e[1] + e[2])
    weights = [(e_g * inv).astype(BF16) for e_g in e]
    src_lane = lax.broadcasted_iota(jnp.int32, (LANES, LANES), 0)
    first_head = lax.broadcasted_iota(jnp.int32, (LANES, LANES), 1) < HEAD_DIM
    for c in range(GROUP_CHUNKS):
        cols = slice(c * LANES, (c + 1) * LANES)
        lane_b = LSE_LANES * c
        expand = jnp.where(src_lane == jnp.where(first_head, lane_b + HEAD_DIM, lane_b),
                           1.0, 0.0).astype(BF16)
        acc = jnp.zeros((ROW_TILE, LANES), F32)
        for g in range(N_GROUPS):
            acc = acc + _dot(weights[g], expand) * token_order(o_refs[g], g, g * n_slabs + c, cols)
        attn_ref[:, cols] = acc.astype(BF16)

    y_a = _dot(attn_ref[...], wo_ref[...])
    y_b = _dot(uln_ref[read_slot], wpw_ref[...])
    halves = [slice(r, r + EPI_ROWS) for r in range(0, ROW_TILE, EPI_ROWS)]
    for rows in halves:
        act_ref[rows, 0:D_MODEL] = (g_ref[rows, 0:D_MODEL] * y_a[rows, :]
                                    + g_ref[rows, D_MODEL:2 * D_MODEL] * y_b[rows, :]).astype(BF16)
    dx = _dot(act_ref[:, 0:D_MODEL], wout_ref[...])
    x1_rows, h2_rows = [], []
    for rows in halves:
        xh = x_ref[rows, :] + dx[rows, :]
        ms = jnp.mean(xh * xh, axis=-1, keepdims=True)
        x1_rows.append(xh)
        h2_rows.append((xh * lax.rsqrt(ms + EPS) * n2_ref[...]).astype(BF16))
    x1 = jnp.concatenate(x1_rows, axis=0)
    h2 = jnp.concatenate(h2_rows, axis=0)
    ff_chunks = range(0, D_FF, FF_CHUNK)
    conv_blocks = list(range(0, ROW_TILE, CONV_ROWS))
    issue_after = [k * (len(ff_chunks) - 1) // len(conv_blocks) for k in range(len(conv_blocks))]
    pending = None
    for idx, n in enumerate(ff_chunks):
        width = min(FF_CHUNK, D_FF - n)
        gt = _dot(h2, w1_ref[:, n:n + width])
        up = _dot(h2, w1_ref[:, D_FF + n:D_FF + n + width])
        act = gt * jax.nn.sigmoid(gt) * up
        if pending is not None:
            act = act + jnp.concatenate([_zero_after(pending)] * (width // LANES), axis=1)
            pending = None
        act_ref[:, n:n + width] = act.astype(BF16)
        for k, r0 in enumerate(conv_blocks):
            if issue_after[k] == idx:
                probe = _conv_ln_rows(r0, cw_ref, cb_ref, lnw_ref, lnb_ref,
                                      uln_ref.at[write_slot], pad_ref, conv_ref)
                pending = probe if pending is None else pending + probe
    out_ref[...] = x1 + _dot(act_ref[...], w2_ref[...])


def _mix_ffn_call(x3, o, lse, gates, u, conv_w, conv_b, ln_w, ln_b, wo, wpw, wout, n2, w1, w2):
    bsz, s, _ = x3.shape
    dilations = [d for _, d in DILATED_PATTERNS]
    tiles_per_seq = s // ROW_TILE
    n_tiles = bsz * tiles_per_seq
    halo_blocks = ROW_TILE // CONV_PAD

    def finish_tile(t):
        tile = jnp.maximum(t - 1, 0)
        return tile // tiles_per_seq, tile % tiles_per_seq

    def conv_tile(t):
        tile = jnp.minimum(t, n_tiles - 1)
        return tile // tiles_per_seq, tile % tiles_per_seq

    def row(width):
        return pl.BlockSpec((None, ROW_TILE, width), lambda t: (*finish_tile(t), 0))

    def group(dilation, width):
        return pl.BlockSpec((None, dilation, ROW_TILE // dilation, width),
                            lambda t: (finish_tile(t)[0], 0, finish_tile(t)[1], 0))

    def halo(offset):
        def index(t):
            b, i = conv_tile(t)
            block = i * halo_blocks - 1 if offset < 0 else (i + 1) * halo_blocks
            return b, jnp.clip(block, 0, s // CONV_PAD - 1), 0
        return pl.BlockSpec((None, CONV_PAD, CONV_CH), index)

    return pl.pallas_call(
        functools.partial(_mix_ffn_kernel, tiles_per_seq=tiles_per_seq),
        grid=(n_tiles + 1,),
        in_specs=[row(D_MODEL)]
        + [group(d, GROUP_WIDTH) for d in dilations]
        + [group(d, LANES) for d in dilations]
        + [row(2 * D_MODEL),
           pl.BlockSpec((None, ROW_TILE, CONV_CH), lambda t: (*conv_tile(t), 0)), halo(-1), halo(1),
           _resident(conv_w.shape), _resident(conv_b.shape), _resident(ln_w.shape),
           _resident(ln_b.shape),
           _resident(wo.shape), _resident(wpw.shape),
           _resident(wout.shape), _resident(n2.shape), _resident(w1.shape), _resident(w2.shape)],
        out_specs=row(D_MODEL),
        out_shape=jax.ShapeDtypeStruct((bsz, s, D_MODEL), F32),
        scratch_shapes=[pltpu.VMEM((ROW_TILE, GROUP_WIDTH), BF16),
                        pltpu.VMEM((ROW_TILE, D_FF), BF16),
                        pltpu.VMEM((N_GROUPS * (GROUP_CHUNKS + 1), ROW_TILE, LANES), F32),
                        pltpu.VMEM((CONV_CHUNKS, ROW_TILE + 2 * CONV_PAD, LANES), F32),
                        pltpu.VMEM((CONV_CHUNKS, ROW_TILE, LANES), F32),
                        pltpu.VMEM((2, ROW_TILE, CONV_CH), BF16),
                        pltpu.VMEM((GROUP_CHUNKS + 1, ROW_TILE, LANES), F32)],
        compiler_params=pltpu.CompilerParams(dimension_semantics=("arbitrary",),
                                             vmem_limit_bytes=MIX_VMEM_LIMIT),
        name="mix_ffn",
    )(x3, *o, *lse, gates, u, u, u, conv_w, conv_b, ln_w, ln_b, wo, wpw, wout, n2, w1, w2)


def _rope_table(positions):
    inv_freq = ROPE_THETA ** (-jnp.arange(0, ROT_DIM, 2, dtype=F32) / ROT_DIM)
    ang = positions.astype(F32)[:, None, :] * inv_freq[None, :, None]
    rest = jnp.zeros((ang.shape[0], HEAD_DIM - ROT_DIM, ang.shape[2]), F32)
    return jnp.concatenate([jnp.cos(ang), jnp.sin(ang), rest] * (LANES // HEAD_DIM), axis=1)


def kernel(x, positions, norm1_w, w_in, b_gate, q_norm_w, k_norm_w, w_o_attn, conv_w, conv_b,
           conv_ln_w, conv_ln_b, w_pw_conv, w_out, norm2_w, w_ffn_in, w_ffn_out):
    d = x.shape[-1]
    depth = norm1_w.shape[0]
    rot = _rope_table(positions)
    pair = lambda w: jnp.tile(w.astype(F32), LANES // HEAD_DIM).reshape(1, LANES)

    for l in range(depth):
        qkv0, qkv1, qkv2, u, gates = _proj_call(
            x, norm1_w[l].reshape(1, d), w_in[l].astype(BF16), rot,
            pair(q_norm_w[l]), pair(k_norm_w[l]), b_gate[l].reshape(1, 2 * d))
        outs, lses = zip(*[_attn_call(qkv) for qkv in (qkv0, qkv1, qkv2)])
        x = _mix_ffn_call(
            x, outs, lses, gates, u, conv_w[l], conv_b[l].reshape(1, CONV_CH),
            conv_ln_w[l].reshape(1, CONV_CH), conv_ln_b[l].reshape(1, CONV_CH),
            w_o_attn[l].astype(BF16), w_pw_conv[l].astype(BF16), w_out[l].astype(BF16),
            norm2_w[l].reshape(1, d), w_ffn_in[l].astype(BF16), w_ffn_out[l].astype(BF16))
    return x
```
